```python
import math
import jax, jax.numpy as jnp
from jax import lax
import numpy as np

D_MODEL = 1024
BATCH = 8
SEQ = 8192
DEPTH = 1

GRID_W = 64
CTX_LEN = 256
MIX_WIDTH = D_MODEL
SSM_WIDTH = MIX_WIDTH // 2
SSM_GROUP = 16
SSM_GROUPS = SSM_WIDTH // SSM_GROUP
SSM_STATE = 64
ATTN_WIDTH = MIX_WIDTH - SSM_WIDTH
DIFF_HEADS = 4
DIFF_HEAD_DIM = ATTN_WIDTH // (2 * DIFF_HEADS)
ROT_FREQS = DIFF_HEAD_DIM // 4
ROPE_BASE = 10000.0
IN_WIDTH = SSM_WIDTH + 3 * ATTN_WIDTH
D_FF = 128 * ((8 * D_MODEL // 3 + 127) // 128)
Q_BLOCK = 128
N_MOD = 9
EPS = 1e-6

kernel_name = 'hymba_s5_diffattn_macaron_dit'


def rms_norm(x, g):
    xf = x.astype(jnp.float32)
    y = xf * lax.rsqrt(jnp.mean(xf * xf, axis=-1, keepdims=True) + EPS)
    return (y * g.astype(jnp.float32)).astype(x.dtype)


def adaln(h, g, mod, i):
    return rms_norm(h, g) * (1.0 + mod[:, :, 3 * i + 1]) + mod[:, :, 3 * i]


def swiglu(h, w_in, w_out):
    gate, up = jnp.split(h @ w_in, 2, axis=-1)
    return (jax.nn.silu(gate) * up) @ w_out


def axial_rope_tables(rows):
    row = jnp.repeat(jnp.arange(rows, dtype=jnp.float32), GRID_W)
    col = jnp.tile(jnp.arange(GRID_W, dtype=jnp.float32), rows)
    inv_freq = ROPE_BASE ** (-jnp.arange(ROT_FREQS, dtype=jnp.float32) / ROT_FREQS)
    ang = jnp.stack([row[:, None] * inv_freq, col[:, None] * inv_freq], axis=1)
    return jnp.cos(ang), jnp.sin(ang)


def apply_axial_rope(t, cos, sin):
    ts = t.astype(jnp.float32).reshape(*t.shape[:-1], 2, 2, ROT_FREQS)
    x1 = ts[..., 0, :]
    x2 = ts[..., 1, :]
    cb = cos[None, :, None, None]
    sb = sin[None, :, None, None]
    out = jnp.stack([x1 * cb - x2 * sb, x2 * cb + x1 * sb], axis=-2)
    return out.reshape(t.shape).astype(t.dtype)


def cmul(ar, ai, br, bi):
    return ar * br - ai * bi, ar * bi + ai * br


def zoh_discretise(a_re, a_im, log_dt, b_re, b_im):
    dt = jnp.exp(log_dt)[:, None]
    mag = jnp.exp(dt * a_re)
    abar_re = mag * jnp.cos(dt * a_im)
    abar_im = mag * jnp.sin(dt * a_im)
    zr = abar_re - 1.0
    zi = abar_im
    den = a_re * a_re + a_im * a_im
    coef_re = (zr * a_re + zi * a_im) / den
    coef_im = (zi * a_re - zr * a_im) / den
    bbar_re, bbar_im = cmul(coef_re[..., None], coef_im[..., None], b_re, b_im)
    return abar_re, abar_im, bbar_re, bbar_im


def diag_scan(abar_re, abar_im, bu_re, bu_im, reverse):
    length = bu_re.shape[1]
    a_re = jnp.broadcast_to(abar_re, (1, length) + abar_re.shape)
    a_im = jnp.broadcast_to(abar_im, (1, length) + abar_im.shape)

    def combine(e1, e2):
        a1r, a1i, b1r, b1i = e1
        a2r, a2i, b2r, b2i = e2
        ar, ai = cmul(a2r, a2i, a1r, a1i)
        br, bi = cmul(a2r, a2i, b1r, b1i)
        return ar, ai, br + b2r, bi + b2i

    return lax.associative_scan(combine, (a_re, a_im, bu_re, bu_im), axis=1, reverse=reverse)


def s5_readout(h_re, h_im, c_re, c_im):
    return jnp.einsum('blgp,ghp->blgh', h_re, c_re) - jnp.einsum('blgp,ghp->blgh', h_im, c_im)


def s5_bidirectional(u_lat, u_ctx, a_re, a_im, log_dt, b_re, b_im, c_re, c_im, d_skip,
                     w_glu, b_glu, with_ctx_out):
    f32 = jnp.float32
    out_dtype = u_lat.dtype
    u_lat = u_lat.astype(f32)
    u_ctx = u_ctx.astype(f32)
    d_skip = d_skip.astype(f32)
    y_lat = d_skip * u_lat
    y_ctx = d_skip * u_ctx if with_ctx_out else None
    for direction in range(2):
        reverse = direction == 1
        abr, abi, bbr, bbi = zoh_discretise(a_re[direction].astype(f32), a_im[direction].astype(f32),
                                            log_dt[direction].astype(f32),
                                            b_re[direction].astype(f32), b_im[direction].astype(f32))
        cr = c_re[direction].astype(f32)
        ci = c_im[direction].astype(f32)
        buc_r = jnp.einsum('blgh,gph->blgp', u_ctx, bbr)
        buc_i = jnp.einsum('blgh,gph->blgp', u_ctx, bbi)
        _, _, hc_r, hc_i = diag_scan(abr, abi, buc_r, buc_i, reverse)
        end = 0 if reverse else -1
        h0_r = hc_r[:, end][:, None]
        h0_i = hc_i[:, end][:, None]
        bul_r = jnp.einsum('blgh,gph->blgp', u_lat, bbr)
        bul_i = jnp.einsum('blgh,gph->blgp', u_lat, bbi)
        ap_r, ap_i, hl_r, hl_i = diag_scan(abr, abi, bul_r, bul_i, reverse)
        carry_r, carry_i = cmul(ap_r, ap_i, h0_r, h0_i)
        y_lat = y_lat + s5_readout(hl_r + carry_r, hl_i + carry_i, cr, ci)
        if with_ctx_out:
            y_ctx = y_ctx + s5_readout(hc_r, hc_i, cr, ci)

    def glu(y):
        g = jax.nn.gelu(y).reshape(*y.shape[:2], SSM_WIDTH)
        return g * jax.nn.sigmoid(g @ w_glu.astype(f32) + b_glu.astype(f32))

    out_ctx = glu(y_ctx).astype(out_dtype) if with_ctx_out else None
    return glu(y_lat).astype(out_dtype), out_ctx


def diff_softmax_attend(q, k, v, lam):
    s = jnp.einsum('bqhcd,bkhcd->bhcqk', q, k).astype(jnp.float32) * (DIFF_HEAD_DIM ** -0.5)
    p = jax.nn.softmax(s, axis=-1)
    p_diff = p[:, :, 0] - lam * p[:, :, 1]
    return jnp.einsum('bhqk,bkhe->bqhe', p_diff.astype(v.dtype), v)


def differential_attention(q_lat, k_lat, v_lat, q_ctx, k_ctx, v_ctx, lam_q, lam_k, subln_g,
                           lam_init, with_ctx_out):
    f32 = jnp.float32
    lam = (jnp.exp(jnp.sum(lam_q[0].astype(f32) * lam_k[0].astype(f32)))
           - jnp.exp(jnp.sum(lam_q[1].astype(f32) * lam_k[1].astype(f32))) + lam_init)
    B, L = q_lat.shape[:2]
    k_all = jnp.concatenate([k_lat, k_ctx], axis=1)
    v_all = jnp.concatenate([v_lat, v_ctx], axis=1)
    nb = L // Q_BLOCK
    q_blocks = q_lat.reshape(B, nb, Q_BLOCK, DIFF_HEADS, 2, DIFF_HEAD_DIM).swapaxes(0, 1)
    o = lax.map(lambda qb: diff_softmax_attend(qb, k_all, v_all, lam), q_blocks)
    o_lat = o.swapaxes(0, 1).reshape(B, L, DIFF_HEADS, 2 * DIFF_HEAD_DIM)

    def finish(out):
        return (rms_norm(out, subln_g) * (1.0 - lam_init)).reshape(*out.shape[:2], ATTN_WIDTH)

    out_ctx = finish(diff_softmax_attend(q_ctx, k_ctx, v_ctx, lam)) if with_ctx_out else None
    return finish(o_lat), out_ctx


def split_mixer_inputs(p):
    b, n, _ = p.shape
    o1 = SSM_WIDTH
    o2 = o1 + ATTN_WIDTH
    o3 = o2 + ATTN_WIDTH
    u = p[..., :o1].reshape(b, n, SSM_GROUPS, SSM_GROUP)
    q = p[..., o1:o2].reshape(b, n, DIFF_HEADS, 2, DIFF_HEAD_DIM)
    k = p[..., o2:o3].reshape(b, n, DIFF_HEADS, 2, DIFF_HEAD_DIM)
    v = p[..., o3:].reshape(b, n, DIFF_HEADS, 2 * DIFF_HEAD_DIM)
    return u, q, k, v


def hybrid_mixer(h_lat, h_ctx, cos, sin, w_in, w_out, a_re, a_im, log_dt, b_re, b_im, c_re, c_im,
                 d_skip, w_glu, b_glu, lam_q, lam_k, subln_g, lam_init, with_ctx_out):
    u_lat, q_lat, k_lat, v_lat = split_mixer_inputs(h_lat @ w_in)
    u_ctx, q_ctx, k_ctx, v_ctx = split_mixer_inputs(h_ctx @ w_in)
    q_lat = apply_axial_rope(q_lat, cos, sin)
    k_lat = apply_axial_rope(k_lat, cos, sin)
    s_lat, s_ctx = s5_bidirectional(u_lat, u_ctx, a_re, a_im, log_dt, b_re, b_im, c_re, c_im,
                                    d_skip, w_glu, b_glu, with_ctx_out)
    a_lat, a_ctx = differential_attention(q_lat, k_lat, v_lat, q_ctx, k_ctx, v_ctx, lam_q, lam_k,
                                          subln_g, lam_init, with_ctx_out)
    y_lat = jnp.concatenate([s_lat, a_lat], axis=-1) @ w_out
    y_ctx = jnp.concatenate([s_ctx, a_ctx], axis=-1) @ w_out if with_ctx_out else None
    return y_lat, y_ctx


def setup_inputs(seed: int = 0) -> dict:
    key = jax.random.key(seed)
    ks = jax.random.split(key, 26)
    f32 = jnp.float32
    nrm = lambda k, shape, s: jax.random.normal(k, shape, f32) * s
    D, G, P, H = D_MODEL, SSM_GROUPS, SSM_STATE, SSM_GROUP
    a_im0 = math.pi * jnp.arange(P, dtype=f32)
    log_lo, log_hi = math.log(0.001), math.log(0.1)
    return {
        'x': nrm(ks[0], (BATCH, SEQ, D), 1.0),
        'c': nrm(ks[1], (BATCH, D), 1.0),
        'ctx': nrm(ks[2], (BATCH, CTX_LEN, D), 1.0),
        'c_ctx': nrm(ks[3], (D,), 1.0),
        'w_mod': nrm(ks[4], (DEPTH, D, N_MOD * D), 0.5 * D ** -0.5),
        'b_mod': nrm(ks[5], (DEPTH, N_MOD * D), 0.02),
        'norm_g': 1.0 + nrm(ks[6], (DEPTH, 3, D), 0.02),
        'ffn_w_in': nrm(ks[7], (DEPTH, 2, D, 2 * D_FF), D ** -0.5),
        'ffn_w_out': nrm(ks[8], (DEPTH, 2, D_FF, D), D_FF ** -0.5),
        'w_in': nrm(ks[9], (DEPTH, D, IN_WIDTH), D ** -0.5),
        'w_out': nrm(ks[10], (DEPTH, MIX_WIDTH, D), MIX_WIDTH ** -0.5),
        'ssm_a_re': -0.5 + nrm(ks[11], (DEPTH, 2, G, P), 0.01),
        'ssm_a_im': a_im0 + nrm(ks[12], (DEPTH, 2, G, P), 0.01),
        'ssm_log_dt': log_lo + (log_hi - log_lo) * jax.random.uniform(ks[13], (DEPTH, 2, G), f32),
        'ssm_b_re': nrm(ks[14], (DEPTH, 2, G, P, H), (2 * H) ** -0.5),
        'ssm_b_im': nrm(ks[15], (DEPTH, 2, G, P, H), (2 * H) ** -0.5),
        'ssm_c_re': nrm(ks[16], (DEPTH, 2, G, H, P), (2 * P) ** -0.5),
        'ssm_c_im': nrm(ks[17], (DEPTH, 2, G, H, P), (2 * P) ** -0.5),
        'ssm_d': nrm(ks[18], (DEPTH, G, H), 1.0),
        'w_glu': nrm(ks[19], (DEPTH, SSM_WIDTH, SSM_WIDTH), SSM_WIDTH ** -0.5),
        'b_glu': nrm(ks[20], (DEPTH, SSM_WIDTH), 0.02),
        'lam_q': nrm(ks[21], (DEPTH, 2, DIFF_HEAD_DIM), 0.1),
        'lam_k': nrm(ks[22], (DEPTH, 2, DIFF_HEAD_DIM), 0.1),
        'subln_g': 1.0 + nrm(ks[23], (DEPTH, 2 * DIFF_HEAD_DIM), 0.02),
        'final_g': 1.0 + nrm(ks[24], (D,), 0.02),
    }


def reference(x, c, ctx, c_ctx, w_mod, b_mod, norm_g, ffn_w_in, ffn_w_out, w_in, w_out,
              ssm_a_re, ssm_a_im, ssm_log_dt, ssm_b_re, ssm_b_im, ssm_c_re, ssm_c_im, ssm_d,
              w_glu, b_glu, lam_q, lam_k, subln_g, final_g):
    B, L, _ = x.shape
    rows = L // GRID_W
    cos, sin = axial_rope_tables(rows)
    silu_c = jax.nn.silu(c)
    silu_cc = jax.nn.silu(c_ctx)
    for l in range(DEPTH):
        update_ctx = l < DEPTH - 1
        lam_init = 0.8 - 0.6 * math.exp(-0.3 * l)
        mod_lat = (silu_c @ w_mod[l] + b_mod[l]).reshape(B, 1, N_MOD, D_MODEL)
        mod_ctx = (silu_cc @ w_mod[l] + b_mod[l]).reshape(1, 1, N_MOD, D_MODEL)
        x = x + 0.5 * mod_lat[:, :, 2] * swiglu(adaln(x, norm_g[l, 0], mod_lat, 0),
                                                ffn_w_in[l, 0], ffn_w_out[l, 0])
        ctx = ctx + 0.5 * mod_ctx[:, :, 2] * swiglu(adaln(ctx, norm_g[l, 0], mod_ctx, 0),
                                                    ffn_w_in[l, 0], ffn_w_out[l, 0])
        y_lat, y_ctx = hybrid_mixer(adaln(x, norm_g[l, 1], mod_lat, 1),
                                    adaln(ctx, norm_g[l, 1], mod_ctx, 1),
                                    cos, sin, w_in[l], w_out[l], ssm_a_re[l], ssm_a_im[l],
                                    ssm_log_dt[l], ssm_b_re[l], ssm_b_im[l], ssm_c_re[l],
                                    ssm_c_im[l], ssm_d[l], w_glu[l], b_glu[l], lam_q[l], lam_k[l],
                                    subln_g[l], lam_init, update_ctx)
        x = x + mod_lat[:, :, 5] * y_lat
        x = x + 0.5 * mod_lat[:, :, 8] * swiglu(adaln(x, norm_g[l, 2], mod_lat, 2),
                                                ffn_w_in[l, 1], ffn_w_out[l, 1])
        if update_ctx:
            ctx = ctx + mod_ctx[:, :, 5] * y_ctx
            ctx = ctx + 0.5 * mod_ctx[:, :, 8] * swiglu(adaln(ctx, norm_g[l, 2], mod_ctx, 2),
                                                        ffn_w_in[l, 1], ffn_w_out[l, 1])
    return rms_norm(x, final_g)
```

```python
import functools
import math

import jax
import jax.numpy as jnp
from jax import lax
from jax.experimental import pallas as pl
from jax.experimental.pallas import tpu as pltpu

F32 = jnp.float32
BF16 = jnp.bfloat16

EPS = 1e-6
N_MOD = 9
GRID_W = 64
ROPE_BASE = 10000.0
DIFF_HEADS = 4
SUBLANES = 8
LANES = 128
MXU_DIM = 256
VMEM_LIMIT = 56 * 1024 * 1024


def _cparams(*sem):
    return pltpu.CompilerParams(dimension_semantics=sem, vmem_limit_bytes=VMEM_LIMIT)


def _const_spec(shape):
    nd = len(shape)
    return pl.BlockSpec(shape, lambda *_: (0,) * nd, pipeline_mode=pl.Buffered(1))


def _rms_adaln(x, g, shift, scale):
    ms = jnp.mean(x * x, axis=-1, keepdims=True)
    return (x * lax.rsqrt(ms + EPS) * g) * (1.0 + scale) + shift


def _mod_kernel(c_ref, w_ref, b_ref, o_ref):
    c = c_ref[...]
    sc = c * jax.nn.sigmoid(c)
    o_ref[...] = jnp.dot(sc, w_ref[...], precision=lax.Precision.HIGHEST,
                         preferred_element_type=F32) + b_ref[...]


def _mod_call(c_rows, w_mod, b_mod):
    rows, d = c_rows.shape
    n = w_mod.shape[1]
    bn = d
    return pl.pallas_call(
        _mod_kernel,
        grid=(n // bn,),
        in_specs=[pl.BlockSpec((rows, d), lambda j: (0, 0)),
                  pl.BlockSpec((d, bn), lambda j: (0, j)),
                  pl.BlockSpec((1, bn), lambda j: (0, j))],
        out_specs=pl.BlockSpec((rows, bn), lambda j: (0, j)),
        out_shape=jax.ShapeDtypeStruct((rows, n), F32),
        name="mod",
        compiler_params=_cparams("arbitrary"),
    )(c_rows, w_mod, b_mod.reshape(1, n))


def _ffn_kernel(x_ref, mod_ref, g_ref, win_ref, wout_ref, *rest, mod_i, d_ff, fc, final):
    if final:
        fg_ref, o_ref, acc_ref = rest
    else:
        o_ref, acc_ref = rest
    x = x_ref[...]
    shift = mod_ref[3 * mod_i:3 * mod_i + 1, :]
    scale = mod_ref[3 * mod_i + 1:3 * mod_i + 2, :]
    gate = mod_ref[3 * mod_i + 2:3 * mod_i + 3, :]
    h = _rms_adaln(x, g_ref[...], shift, scale).astype(BF16)
    for j in range(d_ff // fc):
        gt = jnp.dot(h, win_ref[:, j * fc:(j + 1) * fc], preferred_element_type=F32)
        up = jnp.dot(h, win_ref[:, d_ff + j * fc:d_ff + (j + 1) * fc], preferred_element_type=F32)
        act = (gt * jax.nn.sigmoid(gt) * up).astype(BF16)
        part = jnp.dot(act, wout_ref[j * fc:(j + 1) * fc, :], preferred_element_type=F32)
        if j == 0:
            acc_ref[...] = part
        else:
            acc_ref[...] += part
    y = x + (0.5 * gate) * acc_ref[...]
    if final:
        ms = jnp.mean(y * y, axis=-1, keepdims=True)
        y = y * lax.rsqrt(ms + EPS) * fg_ref[...]
    o_ref[...] = y


def _ffn_call(x, mod, g, w_in, w_out, *, mod_i, mod_row, tm, final_g=None):
    b, t, d = x.shape
    d_ff = w_out.shape[0]
    fc = MXU_DIM
    assert t % tm == 0 and d_ff % fc == 0
    final = final_g is not None
    in_specs = [pl.BlockSpec((None, tm, d), lambda bi, ti: (bi, ti, 0)),
                pl.BlockSpec((None, N_MOD, d), lambda bi, ti: (mod_row(bi), 0, 0)),
                _const_spec((1, d)), _const_spec(w_in.shape), _const_spec(w_out.shape)]
    args = [x, mod, g, w_in, w_out]
    if final:
        in_specs.append(_const_spec((1, d)))
        args.append(final_g)
    return pl.pallas_call(
        functools.partial(_ffn_kernel, mod_i=mod_i, d_ff=d_ff, fc=fc, final=final),
        grid=(b, t // tm),
        in_specs=in_specs,
        out_specs=pl.BlockSpec((None, tm, d), lambda bi, ti: (bi, ti, 0)),
        out_shape=jax.ShapeDtypeStruct((b, t, d), F32),
        scratch_shapes=[pltpu.VMEM((tm, d), F32)],
        name="ffn_final" if final else "ffn",
        compiler_params=_cparams("parallel", "parallel"),
    )(*args)


def _rope(t, cos, sin_signed, first_half):
    outs = []
    for j in range(t.shape[1] // LANES):
        tj = t[:, j * LANES:(j + 1) * LANES]
        partner = jnp.where(first_half, pltpu.roll(tj, LANES - 16, 1), pltpu.roll(tj, 16, 1))
        outs.append(tj * cos + partner * sin_signed)
    return jnp.concatenate(outs, axis=1)


def _inproj_kernel(x_ref, mod_ref, g_ref, w_ref, *rest, rope, width, q_scale):
    if rope:
        cos_ref, sin_ref, u_ref, q_ref, k_ref, v_ref = rest
    else:
        u_ref, k_ref, v_ref = rest
    h = _rms_adaln(x_ref[...], g_ref[...], mod_ref[3:4, :], mod_ref[4:5, :]).astype(BF16)
    u_ref[...] = jnp.dot(h, w_ref[:, 0:width], preferred_element_type=F32)
    k = jnp.dot(h, w_ref[:, 2 * width:3 * width], preferred_element_type=F32)
    if rope:
        q = jnp.dot(h, w_ref[:, width:2 * width], preferred_element_type=F32)
        cos = cos_ref[...]
        sin_signed = sin_ref[...]
        lane = lax.broadcasted_iota(jnp.int32, (1, LANES), 1)
        first_half = (lane % 32) < 16
        q_ref[...] = (_rope(q, cos, sin_signed, first_half) * q_scale).astype(BF16)
        k = _rope(k, cos, sin_signed, first_half)
    k_ref[...] = k.astype(BF16)
    v_ref[...] = jnp.dot(h, w_ref[:, 3 * width:4 * width], preferred_element_type=F32).astype(BF16)


def _inproj_call(x, mod, g, w, *, mod_row, tm, rope_tabs=None, q_scale=1.0):
    b, t, d = x.shape
    width = w.shape[1] // 4
    assert t % tm == 0
    rope = rope_tabs is not None
    tok = lambda bi, ti: (bi, ti, 0)
    in_specs = [pl.BlockSpec((None, tm, d), tok),
                pl.BlockSpec((None, N_MOD, d), lambda bi, ti: (mod_row(bi), 0, 0)),
                _const_spec((1, d)), _const_spec(w.shape)]
    args = [x, mod, g, w]
    out_specs = [pl.BlockSpec((None, tm, width), tok)]
    out_shape = [jax.ShapeDtypeStruct((b, t, width), F32)]
    n_bf16 = 2
    if rope:
        in_specs += [pl.BlockSpec((tm, LANES), lambda bi, ti: (ti, 0))] * 2
        args += list(rope_tabs)
        n_bf16 = 3
    out_specs += [pl.BlockSpec((None, tm, width), tok)] * n_bf16
    out_shape += [jax.ShapeDtypeStruct((b, t, width), BF16)] * n_bf16
    return pl.pallas_call(
        functools.partial(_inproj_kernel, rope=rope, width=width, q_scale=q_scale),
        grid=(b, t // tm),
        in_specs=in_specs, out_specs=out_specs, out_shape=out_shape,
        name="inproj_rope" if rope else "inproj",
        compiler_params=_cparams("parallel", "parallel"),
    )(*args)


def _rope_tables(length, head_dim):
    n_freq = head_dim // 4
    pos = jnp.arange(length, dtype=jnp.int32)
    row = (pos // GRID_W).astype(F32)
    col = (pos % GRID_W).astype(F32)
    inv_freq = ROPE_BASE ** (-jnp.arange(n_freq, dtype=F32) / n_freq)
    ang = jnp.stack([row[:, None] * inv_freq, col[:, None] * inv_freq], axis=1)
    cos = jnp.broadcast_to(jnp.cos(ang)[:, :, None, :], (length, 2, 2, n_freq))
    sign = jnp.array([-1.0, 1.0], F32)[None, None, :, None]
    sin = jnp.sin(ang)[:, :, None, :] * sign
    reps = LANES // head_dim
    cos = jnp.tile(cos.reshape(length, head_dim), (1, reps))
    sin = jnp.tile(jnp.broadcast_to(sin, (length, 2, 2, n_freq)).reshape(length, head_dim), (1, reps))
    return cos, sin


def _s5_prep_kernel(are_ref, aim_ref, ldt_ref, bre_ref, bim_ref, tab_ref, obre_ref, obim_ref):
    n_dir = are_ref.shape[0]
    n_state = are_ref.shape[1]
    sub = lax.broadcasted_iota(jnp.int32, (SUBLANES, n_state), 0)
    for d in range(n_dir):
        a_re = are_ref[d:d + 1, :]
        a_im = aim_ref[d:d + 1, :]
        dt = jnp.exp(ldt_ref[d:d + 1, :])
        mag = jnp.exp(dt * a_re)
        abar_re = mag * jnp.cos(dt * a_im)
        abar_im = mag * jnp.sin(dt * a_im)
        zr = abar_re - 1.0
        zi = abar_im
        den = a_re * a_re + a_im * a_im
        coef_re = (zr * a_re + zi * a_im) / den
        coef_im = (zi * a_re - zr * a_im) / den
        b_re = bre_ref[d]
        b_im = bim_ref[d]
        obre_ref[d] = coef_re * b_re - coef_im * b_im
        obim_ref[d] = coef_re * b_im + coef_im * b_re

        def power(k):
            m = jnp.exp(k * (dt * a_re))
            return m * jnp.cos(k * (dt * a_im)), m * jnp.sin(k * (dt * a_im))

        reverse = d == 1
        for si, k in enumerate((1, 2, 4)):
            valid = (sub <= SUBLANES - 1 - k) if reverse else (sub >= k)
            pr, pi = power(jnp.full((SUBLANES, n_state), float(k), F32))
            tab_ref[d, 2 * si] = jnp.where(valid, pr, 0.0)
            tab_ref[d, 2 * si + 1] = jnp.where(valid, pi, 0.0)
        steps = (SUBLANES - sub) if reverse else (sub + 1)
        pr, pi = power(steps.astype(F32))
        tab_ref[d, 6] = pr
        tab_ref[d, 7] = pi


def _s5_prep_call(a_re, a_im, log_dt, b_re_t, b_im_t):
    n_dir, n_state = a_re.shape
    h = b_re_t.shape[1]
    return pl.pallas_call(
        _s5_prep_kernel,
        out_shape=[jax.ShapeDtypeStruct((n_dir, 8, SUBLANES, n_state), F32),
                   jax.ShapeDtypeStruct((n_dir, h, n_state), F32),
                   jax.ShapeDtypeStruct((n_dir, h, n_state), F32)],
        name="s5_prep",
    )(a_re, a_im, log_dt, b_re_t, b_im_t)


S5_CHUNKS_PER_LOOP = 2


def _s5_kernel(uc_ref, ul_ref, bre_ref, bim_ref, cre_ref, cim_ref, tab_ref, y_ref,
               hre, him, carry, *, nc, tm, reverse):
    i = pl.program_id(1)
    n_in = bre_ref.shape[0]
    n_state = bre_ref.shape[1]
    kblocks = n_in // MXU_DIM
    sw = n_state // kblocks

    @pl.when(i == 0)
    def _():
        carry[...] = jnp.zeros_like(carry)

    def bu(u_ref):
        u = u_ref[...].astype(BF16)
        for kb in range(kblocks):
            ub = u[:, kb * MXU_DIM:(kb + 1) * MXU_DIM]
            rows = slice(kb * MXU_DIM, (kb + 1) * MXU_DIM)
            cols = slice(kb * sw, (kb + 1) * sw)
            hre[:, cols] = jnp.dot(ub, bre_ref[rows, cols], preferred_element_type=F32)
            him[:, cols] = jnp.dot(ub, bim_ref[rows, cols], preferred_element_type=F32)

    @pl.when(i < nc)
    def _():
        bu(uc_ref)

    @pl.when(i >= nc)
    def _():
        bu(ul_ref)

    ng = tm // SUBLANES
    edge = 0 if reverse else SUBLANES - 1
    ncp = S5_CHUNKS_PER_LOOP
    for lc0 in range(0, n_state // LANES, ncp):
        cols = [slice((lc0 + c) * LANES, (lc0 + c + 1) * LANES) for c in range(ncp)]
        tabs = [[tab_ref[t, :, cs] for t in range(8)] for cs in cols]

        def body(t, car, cols=cols, tabs=tabs):
            g = (ng - 1 - t) if reverse else t
            r0 = pl.multiple_of(g * SUBLANES, SUBLANES)
            new = []
            for c in range(ncp):
                cs = cols[c]
                tb = tabs[c]
                r = hre[pl.ds(r0, SUBLANES), cs]
                m = him[pl.ds(r0, SUBLANES), cs]
                for si, k in enumerate((1, 2, 4)):
                    sh = SUBLANES - k if reverse else k
                    ar, ai = tb[2 * si], tb[2 * si + 1]
                    sr = pltpu.roll(r, sh, 0)
                    sm = pltpu.roll(m, sh, 0)
                    r, m = r + ar * sr - ai * sm, m + ar * sm + ai * sr
                cr, cm = car[2 * c], car[2 * c + 1]
                r, m = r + tb[6] * cr - tb[7] * cm, m + tb[6] * cm + tb[7] * cr
                hre[pl.ds(r0, SUBLANES), cs] = r
                him[pl.ds(r0, SUBLANES), cs] = m
                new.append(jnp.broadcast_to(r[edge:edge + 1, :], (SUBLANES, LANES)))
                new.append(jnp.broadcast_to(m[edge:edge + 1, :], (SUBLANES, LANES)))
            return tuple(new)

        car0 = []
        for cs in cols:
            car0 += [carry[0, :, cs], carry[1, :, cs]]
        carf = lax.fori_loop(0, ng, body, tuple(car0))
        for c, cs in enumerate(cols):
            carry[0, :, cs] = carf[2 * c]
            carry[1, :, cs] = carf[2 * c + 1]

    @pl.when(i >= nc)
    def _():
        n_out = cre_ref.shape[1]
        nblocks = n_out // MXU_DIM
        rw = n_state // nblocks
        for nb in range(nblocks):
            rows = slice(nb * rw, (nb + 1) * rw)
            cols = slice(nb * MXU_DIM, (nb + 1) * MXU_DIM)
            y_ref[:, cols] = (
                jnp.dot(hre[:, rows].astype(BF16), cre_ref[rows, cols], preferred_element_type=F32)
                - jnp.dot(him[:, rows].astype(BF16), cim_ref[rows, cols], preferred_element_type=F32))


def _s5_call(u_ctx, u_lat, b_re, b_im, c_re, c_im, tab, *, reverse, tm):
    b, lc, w = u_ctx.shape
    l = u_lat.shape[1]
    n_state = b_re.shape[1]
    assert lc % tm == 0 and l % tm == 0 and w % MXU_DIM == 0 and n_state % (LANES * S5_CHUNKS_PER_LOOP) == 0
    nc, nl = lc // tm, l // tm

    if reverse:
        ctx_idx = lambda bi, i: (bi, nc - 1 - jnp.minimum(i, nc - 1), 0)
        lat_idx = lambda bi, i: (bi, nl - 1 - jnp.maximum(i - nc, 0), 0)
    else:
        ctx_idx = lambda bi, i: (bi, jnp.minimum(i, nc - 1), 0)
        lat_idx = lambda bi, i: (bi, jnp.maximum(i - nc, 0), 0)
    return pl.pallas_call(
        functools.partial(_s5_kernel, nc=nc, tm=tm, reverse=reverse),
        grid=(b, nc + nl),
        in_specs=[pl.BlockSpec((None, tm, w), ctx_idx),
                  pl.BlockSpec((None, tm, w), lat_idx),
                  _const_spec(b_re.shape), _const_spec(b_im.shape),
                  _const_spec(c_re.shape), _const_spec(c_im.shape),
                  _const_spec(tab.shape)],
        out_specs=pl.BlockSpec((None, tm, w), lat_idx),
        out_shape=jax.ShapeDtypeStruct((b, l, w), F32),
        scratch_shapes=[pltpu.VMEM((tm, n_state), F32), pltpu.VMEM((tm, n_state), F32),
                        pltpu.VMEM((2, SUBLANES, n_state), F32)],
        name="s5_bwd" if reverse else "s5_fwd",
        compiler_params=_cparams("parallel", "arbitrary"),
    )(u_ctx, u_lat, b_re, b_im, c_re, c_im, tab)


def _attn_kernel(q_ref, kl_ref, vl_ref, kc_ref, vc_ref, lq_ref, lk_ref, sg_ref, o_ref,
                 qs, m_s, l_s, acc, *, tq, lam_init):
    ki = pl.program_id(3)
    half = q_ref.shape[1] // 2

    def step(k, v):
        s = lax.dot_general(qs[...], k, (((1,), (1,)), ((), ())), preferred_element_type=F32)
        m_prev = m_s[...]
        m_new = jnp.maximum(m_prev, jnp.max(s, axis=-1, keepdims=True))
        alpha = jnp.exp(m_prev - m_new)
        p = jnp.exp(s - m_new)
        l_s[...] = alpha * l_s[...] + jnp.sum(p, axis=-1, keepdims=True)
        acc[...] = alpha * acc[...] + jnp.dot(p.astype(BF16), v, preferred_element_type=F32)
        m_s[...] = m_new

    @pl.when(ki == 0)
    def _():
        q = q_ref[...]
        lane = lax.broadcasted_iota(jnp.int32, q.shape, 1)
        zero = jnp.zeros_like(q)
        qs[0:tq, :] = jnp.where(lane < half, q, zero)
        qs[tq:2 * tq, :] = jnp.where(lane >= half, q, zero)
        m_s[...] = jnp.full_like(m_s, -jnp.inf)
        l_s[...] = jnp.zeros_like(l_s)
        acc[...] = jnp.zeros_like(acc)
        step(kc_ref[...], vc_ref[...])

    step(kl_ref[...], vl_ref[...])

    @pl.when(ki == pl.num_programs(3) - 1)
    def _():
        lq = lq_ref[...]
        lk = lk_ref[...]
        lam = (jnp.exp(jnp.sum(lq[0:1, :] * lk[0:1, :], axis=-1, keepdims=True))
               - jnp.exp(jnp.sum(lq[1:2, :] * lk[1:2, :], axis=-1, keepdims=True)) + lam_init)
        o = acc[0:tq, :] / l_s[0:tq, :] - lam * (acc[tq:2 * tq, :] / l_s[tq:2 * tq, :])
        ms = jnp.mean(o * o, axis=-1, keepdims=True)
        o_ref[...] = ((o * lax.rsqrt(ms + EPS) * sg_ref[...]) * (1.0 - lam_init)).astype(o_ref.dtype)


def _attn_call(q, k_lat, v_lat, k_ctx, v_ctx, lam_q, lam_k, subln_g, *, lam_init, tq, tk):
    b, l, w = q.shape
    lc = k_ctx.shape[1]
    hw = w // DIFF_HEADS
    assert l % tq == 0 and l % tk == 0 and hw == LANES
    qmap = lambda bi, hi, qi, ki: (bi, qi, hi)
    kmap = lambda bi, hi, qi, ki: (bi, ki, hi)
    cmap = lambda bi, hi, qi, ki: (bi, 0, hi)
    return pl.pallas_call(
        functools.partial(_attn_kernel, tq=tq, lam_init=lam_init),
        grid=(b, DIFF_HEADS, l // tq, l // tk),
        in_specs=[pl.BlockSpec((None, tq, hw), qmap),
                  pl.BlockSpec((None, tk, hw), kmap), pl.BlockSpec((None, tk, hw), kmap),
                  pl.BlockSpec((None, lc, hw), cmap), pl.BlockSpec((None, lc, hw), cmap),
                  _const_spec(lam_q.shape), _const_spec(lam_k.shape), _const_spec(subln_g.shape)],
        out_specs=pl.BlockSpec((None, tq, hw), qmap),
        out_shape=jax.ShapeDtypeStruct((b, l, w), BF16),
        scratch_shapes=[pltpu.VMEM((2 * tq, hw), BF16), pltpu.VMEM((2 * tq, 1), F32),
                        pltpu.VMEM((2 * tq, 1), F32), pltpu.VMEM((2 * tq, hw), F32)],
        name="diff_attn",
        compiler_params=_cparams("parallel", "parallel", "parallel", "arbitrary"),
    )(q, k_lat, v_lat, k_ctx, v_ctx, lam_q, lam_k, subln_g)


def _outproj_kernel(x_ref, mod_ref, u_ref, yf_ref, yb_ref, a_ref, d_ref, wg_ref, bg_ref,
                    wos_ref, woa_ref, o_ref):
    y = d_ref[...] * u_ref[...] + yf_ref[...] + yb_ref[...]
    g = jax.nn.gelu(y)
    z = jnp.dot(g.astype(BF16), wg_ref[...], preferred_element_type=F32) + bg_ref[...]
    s = g * jax.nn.sigmoid(z)
    out = (jnp.dot(s.astype(BF16), wos_ref[...], preferred_element_type=F32)
           + jnp.dot(a_ref[...], woa_ref[...], preferred_element_type=F32))
    o_ref[...] = x_ref[...] + mod_ref[5:6, :] * out


def _outproj_call(x, mod, u, yf, yb, a, d_skip, w_glu, b_glu, w_out_s, w_out_a, *, tm):
    b, l, d = x.shape
    w = u.shape[2]
    assert l % tm == 0
    tok = lambda bi, ti: (bi, ti, 0)
    wide = pl.BlockSpec((None, tm, d), tok)
    narrow = pl.BlockSpec((None, tm, w), tok)
    return pl.pallas_call(
        _outproj_kernel,
        grid=(b, l // tm),
        in_specs=[wide, pl.BlockSpec((None, N_MOD, d), lambda bi, ti: (bi, 0, 0)),
                  narrow, narrow, narrow, narrow,
                  _const_spec(d_skip.shape), _const_spec(w_glu.shape), _const_spec(b_glu.shape),
                  _const_spec(w_out_s.shape), _const_spec(w_out_a.shape)],
        out_specs=wide,
        out_shape=jax.ShapeDtypeStruct((b, l, d), F32),
        name="outproj",
        compiler_params=_cparams("parallel", "parallel"),
    )(x, mod, u, yf, yb, a, d_skip, w_glu, b_glu, w_out_s, w_out_a)


def _block_diag(blocks):
    g, r, c = blocks.shape
    eye = jnp.eye(g, dtype=blocks.dtype)
    return (eye[:, None, :, None] * blocks[:, :, None, :]).reshape(g * r, g * c)


def kernel(x, c, ctx, c_ctx, w_mod, b_mod, norm_g, ffn_w_in, ffn_w_out, w_in, w_out, ssm_a_re, ssm_a_im, ssm_log_dt, ssm_b_re, ssm_b_im, ssm_c_re, ssm_c_im, ssm_d, w_glu, b_glu, lam_q, lam_k, subln_g, final_g):
    b, l, d = x.shape
    lc = ctx.shape[1]
    depth = w_mod.shape[0]
    assert depth == 1, "context-stream update between layers is not implemented"
    n_groups, n_state_g, n_in_g = ssm_b_re.shape[2:]
    ssm_w = n_groups * n_in_g
    n_state = n_groups * n_state_g
    head_dim = lam_q.shape[-1]
    ctx_row = b
    rows = SUBLANES * (-(-(b + 1) // SUBLANES))

    tm_lat = 512 if l % 512 == 0 else 256
    tm_ctx = 256
    tm_s5 = 256
    tq = 512 if l % 512 == 0 else 256
    tk = 1024 if l % 1024 == 0 else 256

    layer = 0
    lam_init = 0.8 - 0.6 * math.exp(-0.3 * layer)

    c_rows = jnp.zeros((rows, d), F32).at[:b].set(c).at[b].set(c_ctx)
    mod = _mod_call(c_rows, w_mod[layer], b_mod[layer]).reshape(rows, N_MOD, d)
    lat_row = lambda bi: bi
    ctx_mod_row = lambda bi: ctx_row

    g0, g1, g2 = (norm_g[layer, i].reshape(1, d) for i in range(3))
    w1_in, w1_out = ffn_w_in[layer, 0].astype(BF16), ffn_w_out[layer, 0].astype(BF16)
    w2_in, w2_out = ffn_w_in[layer, 1].astype(BF16), ffn_w_out[layer, 1].astype(BF16)
    w_mix = w_in[layer].astype(BF16)

    x1 = _ffn_call(x, mod, g0, w1_in, w1_out, mod_i=0, mod_row=lat_row, tm=tm_lat)
    ctx1 = _ffn_call(ctx, mod, g0, w1_in, w1_out, mod_i=0, mod_row=ctx_mod_row, tm=tm_ctx)

    tabs = _rope_tables(l, head_dim)
    u_lat, q_lat, k_lat, v_lat = _inproj_call(x1, mod, g1, w_mix, mod_row=lat_row, tm=tm_lat,
                                              rope_tabs=tabs, q_scale=head_dim ** -0.5)
    u_ctx, k_ctx, v_ctx = _inproj_call(ctx1, mod, g1, w_mix, mod_row=ctx_mod_row, tm=tm_ctx)

    flat = lambda p: p[layer].reshape(2, n_state)
    log_dt = jnp.repeat(ssm_log_dt[layer], n_state_g, axis=-1)
    b_t = lambda p: p[layer].transpose(0, 3, 1, 2).reshape(2, n_in_g, n_state)
    tab, bbar_re, bbar_im = _s5_prep_call(flat(ssm_a_re), flat(ssm_a_im), log_dt,
                                          b_t(ssm_b_re), b_t(ssm_b_im))
    ys = []
    for direction in range(2):
        to_blocks = lambda m: _block_diag(
            m[direction].reshape(n_in_g, n_groups, n_state_g).transpose(1, 0, 2)).astype(BF16)
        c_blocks = lambda p: _block_diag(p[layer, direction].transpose(0, 2, 1)).astype(BF16)
        ys.append(_s5_call(u_ctx, u_lat, to_blocks(bbar_re), to_blocks(bbar_im),
                           c_blocks(ssm_c_re), c_blocks(ssm_c_im), tab[direction],
                           reverse=direction == 1, tm=tm_s5))

    a_lat = _attn_call(q_lat, k_lat, v_lat, k_ctx, v_ctx, lam_q[layer], lam_k[layer],
                       subln_g[layer].reshape(1, -1), lam_init=lam_init, tq=tq, tk=tk)

    w_o = w_out[layer].astype(BF16)
    x2 = _outproj_call(x1, mod, u_lat, ys[0], ys[1], a_lat, ssm_d[layer].reshape(1, ssm_w),
                       w_glu[layer].astype(BF16), b_glu[layer].reshape(1, ssm_w),
                       w_o[:ssm_w], w_o[ssm_w:], tm=tm_lat)

    return _ffn_call(x2, mod, g2, w2_in, w2_out, mod_i=2, mod_row=lat_row, tm=tm_lat,
                     final_g=final_g.reshape(1, d))
```

```python
import functools
import math

import jax
import jax.numpy as jnp
from jax import lax
from jax.experimental import pallas as pl
from jax.experimental.pallas import tpu as pltpu

F32 = jnp.float32
BF16 = jnp.bfloat16

EPS = 1e-6
N_MOD = 9
GRID_W = 64
ROPE_BASE = 10000.0
DIFF_HEADS = 4
SUBLANES = 8
LANES = 128
MXU_DIM = 256
VMEM_LIMIT = 56 * 1024 * 1024


def _cparams(*sem):
    return pltpu.CompilerParams(dimension_semantics=sem, vmem_limit_bytes=VMEM_LIMIT)


def _const_spec(shape):
    nd = len(shape)
    return pl.BlockSpec(shape, lambda *_: (0,) * nd, pipeline_mode=pl.Buffered(1))


def _rms_adaln(x, g, shift, scale):
    ms = jnp.mean(x * x, axis=-1, keepdims=True)
    return (x * lax.rsqrt(ms + EPS) * g) * (1.0 + scale) + shift


def _mod_kernel(c_ref, w_ref, b_ref, o_ref):
    c = c_ref[...]
    sc = c * jax.nn.sigmoid(c)
    o_ref[...] = jnp.dot(sc, w_ref[...], precision=lax.Precision.HIGHEST,
                         preferred_element_type=F32) + b_ref[...]


def _mod_call(c_rows, w_mod, b_mod):
    rows, d = c_rows.shape
    n = w_mod.shape[1]
    bn = d
    return pl.pallas_call(
        _mod_kernel,
        grid=(n // bn,),
        in_specs=[pl.BlockSpec((rows, d), lambda j: (0, 0)),
                  pl.BlockSpec((d, bn), lambda j: (0, j)),
                  pl.BlockSpec((1, bn), lambda j: (0, j))],
        out_specs=pl.BlockSpec((rows, bn), lambda j: (0, j)),
        out_shape=jax.ShapeDtypeStruct((rows, n), F32),
        name="mod",
        compiler_params=_cparams("arbitrary"),
    )(c_rows, w_mod, b_mod.reshape(1, n))


def _ffn_kernel(x_ref, mod_ref, g_ref, win_ref, wout_ref, *rest, mod_i, d_ff, fc, final):
    if final:
        fg_ref, o_ref, acc_ref = rest
    else:
        o_ref, acc_ref = rest
    x = x_ref[...]
    shift = mod_ref[3 * mod_i:3 * mod_i + 1, :]
    scale = mod_ref[3 * mod_i + 1:3 * mod_i + 2, :]
    gate = mod_ref[3 * mod_i + 2:3 * mod_i + 3, :]
    h = _rms_adaln(x, g_ref[...], shift, scale).astype(BF16)
    for j in range(d_ff // fc):
        gt = jnp.dot(h, win_ref[:, j * fc:(j + 1) * fc], preferred_element_type=F32)
        up = jnp.dot(h, win_ref[:, d_ff + j * fc:d_ff + (j + 1) * fc], preferred_element_type=F32)
        act = (gt * jax.nn.sigmoid(gt) * up).astype(BF16)
        part = jnp.dot(act, wout_ref[j * fc:(j + 1) * fc, :], preferred_element_type=F32)
        if j == 0:
            acc_ref[...] = part
        else:
            acc_ref[...] += part
    y = x + (0.5 * gate) * acc_ref[...]
    if final:
        ms = jnp.mean(y * y, axis=-1, keepdims=True)
        y = y * lax.rsqrt(ms + EPS) * fg_ref[...]
    o_ref[...] = y


def _ffn_call(x, mod, g, w_in, w_out, *, mod_i, mod_row, tm, final_g=None):
    b, t, d = x.shape
    d_ff = w_out.shape[0]
    fc = MXU_DIM
    assert t % tm == 0 and d_ff % fc == 0
    final = final_g is not None
    in_specs = [pl.BlockSpec((None, tm, d), lambda bi, ti: (bi, ti, 0)),
                pl.BlockSpec((None, N_MOD, d), lambda bi, ti: (mod_row(bi), 0, 0)),
                _const_spec((1, d)), _const_spec(w_in.shape), _const_spec(w_out.shape)]
    args = [x, mod, g, w_in, w_out]
    if final:
        in_specs.append(_const_spec((1, d)))
        args.append(final_g)
    return pl.pallas_call(
        functools.partial(_ffn_kernel, mod_i=mod_i, d_ff=d_ff, fc=fc, final=final),
        grid=(b, t // tm),
        in_specs=in_specs,
        out_specs=pl.BlockSpec((None, tm, d), lambda bi, ti: (bi, ti, 0)),
        out_shape=jax.ShapeDtypeStruct((b, t, d), F32),
        scratch_shapes=[pltpu.VMEM((tm, d), F32)],
        name="ffn_final" if final else "ffn",
        compiler_params=_cparams("parallel", "parallel"),
    )(*args)


def _rope(t, cos, sin_signed, first_half):
    outs = []
    for j in range(t.shape[1] // LANES):
        tj = t[:, j * LANES:(j + 1) * LANES]
        partner = jnp.where(first_half, pltpu.roll(tj, LANES - 16, 1), pltpu.roll(tj, 16, 1))
        outs.append(tj * cos + partner * sin_signed)
    return jnp.concatenate(outs, axis=1)


def _inproj_kernel(x_ref, mod_ref, g_ref, w_ref, *rest, rope, width, q_scale):
    if rope:
        cos_ref, sin_ref, u_ref, q_ref, k_ref, v_ref = rest
    else:
        u_ref, k_ref, v_ref = rest
    h = _rms_adaln(x_ref[...], g_ref[...], mod_ref[3:4, :], mod_ref[4:5, :]).astype(BF16)
    u_ref[...] = jnp.dot(h, w_ref[:, 0:width], preferred_element_type=F32)
    k = jnp.dot(h, w_ref[:, 2 * width:3 * width], preferred_element_type=F32)
    if rope:
        q = jnp.dot(h, w_ref[:, width:2 * width], preferred_element_type=F32)
        cos = cos_ref[...]
        sin_signed = sin_ref[...]
        lane = lax.broadcasted_iota(jnp.int32, (1, LANES), 1)
        first_half = (lane % 32) < 16
        q_ref[...] = (_rope(q, cos, sin_signed, first_half) * q_scale).astype(BF16)
        k = _rope(k, cos, sin_signed, first_half)
    k_ref[...] = k.astype(BF16)
    v = jnp.dot(h, w_ref[:, 3 * width:4 * width], preferred_element_type=F32).astype(BF16)
    ones = jnp.ones((v.shape[0], LANES), BF16)
    for hd in range(width // LANES):
        v_ref[:, 2 * hd * LANES:(2 * hd + 1) * LANES] = v[:, hd * LANES:(hd + 1) * LANES]
        v_ref[:, (2 * hd + 1) * LANES:(2 * hd + 2) * LANES] = ones


def _inproj_call(x, mod, g, w, *, mod_row, tm, rope_tabs=None, q_scale=1.0):
    b, t, d = x.shape
    width = w.shape[1] // 4
    assert t % tm == 0
    rope = rope_tabs is not None
    tok = lambda bi, ti: (bi, ti, 0)
    in_specs = [pl.BlockSpec((None, tm, d), tok),
                pl.BlockSpec((None, N_MOD, d), lambda bi, ti: (mod_row(bi), 0, 0)),
                _const_spec((1, d)), _const_spec(w.shape)]
    args = [x, mod, g, w]
    out_specs = [pl.BlockSpec((tm, width), lambda bi, ti: (ti, bi))]
    out_shape = [jax.ShapeDtypeStruct((t, b * width), F32)]
    n_qk = 1
    if rope:
        in_specs += [pl.BlockSpec((tm, LANES), lambda bi, ti: (ti, 0))] * 2
        args += list(rope_tabs)
        n_qk = 2
    out_specs += [pl.BlockSpec((None, tm, width), tok)] * n_qk
    out_shape += [jax.ShapeDtypeStruct((b, t, width), BF16)] * n_qk
    out_specs.append(pl.BlockSpec((None, tm, 2 * width), tok))
    out_shape.append(jax.ShapeDtypeStruct((b, t, 2 * width), BF16))
    return pl.pallas_call(
        functools.partial(_inproj_kernel, rope=rope, width=width, q_scale=q_scale),
        grid=(b, t // tm),
        in_specs=in_specs, out_specs=out_specs, out_shape=out_shape,
        name="inproj_rope" if rope else "inproj",
        compiler_params=_cparams("parallel", "parallel"),
    )(*args)


def _rope_tables(length, head_dim):
    n_freq = head_dim // 4
    pos = jnp.arange(length, dtype=jnp.int32)
    row = (pos // GRID_W).astype(F32)
    col = (pos % GRID_W).astype(F32)
    inv_freq = ROPE_BASE ** (-jnp.arange(n_freq, dtype=F32) / n_freq)
    ang = jnp.stack([row[:, None] * inv_freq, col[:, None] * inv_freq], axis=1)
    cos = jnp.broadcast_to(jnp.cos(ang)[:, :, None, :], (length, 2, 2, n_freq))
    sign = jnp.array([-1.0, 1.0], F32)[None, None, :, None]
    sin = jnp.sin(ang)[:, :, None, :] * sign
    reps = LANES // head_dim
    cos = jnp.tile(cos.reshape(length, head_dim), (1, reps))
    sin = jnp.tile(jnp.broadcast_to(sin, (length, 2, 2, n_freq)).reshape(length, head_dim), (1, reps))
    return cos, sin


def _s5_prep_kernel(are_ref, aim_ref, ldt_ref, bre_ref, bim_ref, tab_ref, obre_ref, obim_ref):
    n_dir = are_ref.shape[0]
    n_state = are_ref.shape[1]
    for d in range(n_dir):
        a_re = are_ref[d:d + 1, :]
        a_im = aim_ref[d:d + 1, :]
        dt = jnp.exp(ldt_ref[d:d + 1, :])
        mag = jnp.exp(dt * a_re)
        abar_re = mag * jnp.cos(dt * a_im)
        abar_im = mag * jnp.sin(dt * a_im)
        zr = abar_re - 1.0
        zi = abar_im
        den = a_re * a_re + a_im * a_im
        coef_re = (zr * a_re + zi * a_im) / den
        coef_im = (zi * a_re - zr * a_im) / den
        b_re = bre_ref[d]
        b_im = bim_ref[d]
        obre_ref[d] = coef_re * b_re - coef_im * b_im
        obim_ref[d] = coef_re * b_im + coef_im * b_re
        tab_ref[d, 0] = jnp.broadcast_to(abar_re, (SUBLANES, n_state))
        tab_ref[d, 1] = jnp.broadcast_to(abar_im, (SUBLANES, n_state))


def _s5_prep_call(a_re, a_im, log_dt, b_re_t, b_im_t):
    n_dir, n_state = a_re.shape
    h = b_re_t.shape[1]
    return pl.pallas_call(
        _s5_prep_kernel,
        out_shape=[jax.ShapeDtypeStruct((n_dir, 2, SUBLANES, n_state), F32),
                   jax.ShapeDtypeStruct((n_dir, h, n_state), F32),
                   jax.ShapeDtypeStruct((n_dir, h, n_state), F32)],
        name="s5_prep",
    )(a_re, a_im, log_dt, b_re_t, b_im_t)


S5_CHUNKS_PER_PASS = 8
S5_SCAN_UNROLL = 4


def _s5_kernel(uc_ref, ul_ref, bre_ref, bim_ref, cre_ref, cim_ref, tab_ref, y_ref,
               bure, buim, hre, him, carry, *, nc, tt, reverse):
    i = pl.program_id(0)
    n_in = bre_ref.shape[0]
    n_state = bre_ref.shape[1]
    nb = uc_ref.shape[1]
    rows_t = tt * nb
    kblocks = n_in // MXU_DIM
    sw = n_state // kblocks

    @pl.when(i == 0)
    def _():
        carry[...] = jnp.zeros_like(carry)

    def bu(u_ref):
        u = u_ref[...].reshape(rows_t, n_in).astype(BF16)
        for kb in range(kblocks):
            ub = u[:, kb * MXU_DIM:(kb + 1) * MXU_DIM]
            rows = slice(kb * MXU_DIM, (kb + 1) * MXU_DIM)
            cols = slice(kb * sw, (kb + 1) * sw)
            bure[:, cols] = jnp.dot(ub, bre_ref[rows, cols], preferred_element_type=F32)
            buim[:, cols] = jnp.dot(ub, bim_ref[rows, cols], preferred_element_type=F32)

    @pl.when(i < nc)
    def _():
        bu(uc_ref)

    @pl.when(i >= nc)
    def _():
        bu(ul_ref)

    ncp = S5_CHUNKS_PER_PASS
    for lc0 in range(0, n_state // LANES, ncp):
        cols = [slice((lc0 + c) * LANES, (lc0 + c + 1) * LANES) for c in range(ncp)]
        a_re = [tab_ref[0, :, cs] for cs in cols]
        a_im = [tab_ref[1, :, cs] for cs in cols]

        def body(t, car, cols=cols, a_re=a_re, a_im=a_im):
            tok = (tt - 1 - t) if reverse else t
            r0 = pl.multiple_of(tok * nb, nb)
            new = []
            for c in range(ncp):
                cs = cols[c]
                hr, hm = car[2 * c], car[2 * c + 1]
                nr = a_re[c] * hr - a_im[c] * hm + bure[pl.ds(r0, nb), cs]
                nm = a_re[c] * hm + a_im[c] * hr + buim[pl.ds(r0, nb), cs]
                hre[pl.ds(r0, nb), cs] = nr
                him[pl.ds(r0, nb), cs] = nm
                new += [nr, nm]
            return tuple(new)

        car0 = []
        for cs in cols:
            car0 += [carry[0, :, cs], carry[1, :, cs]]
        carf = lax.fori_loop(0, tt, body, tuple(car0), unroll=S5_SCAN_UNROLL)
        for c, cs in enumerate(cols):
            carry[0, :, cs] = carf[2 * c]
            carry[1, :, cs] = carf[2 * c + 1]

    @pl.when(i >= nc)
    def _():
        n_out = cre_ref.shape[1]
        nblocks = n_out // MXU_DIM
        rw = n_state // nblocks
        for ob in range(nblocks):
            rows = slice(ob * rw, (ob + 1) * rw)
            cols = slice(ob * MXU_DIM, (ob + 1) * MXU_DIM)
            y = (jnp.dot(hre[:, rows].astype(BF16), cre_ref[rows, cols], preferred_element_type=F32)
                 - jnp.dot(him[:, rows].astype(BF16), cim_ref[rows, cols], preferred_element_type=F32))
            y_ref[:, :, cols] = y.reshape(tt, nb, MXU_DIM)


def _s5_call(u_ctx, u_lat, b_re, b_im, c_re, c_im, tab, *, reverse, tt):
    lc, b, w = u_ctx.shape
    l = u_lat.shape[0]
    n_state = b_re.shape[1]
    assert b == SUBLANES, "the scan keeps one batch row per sublane"
    assert lc % tt == 0 and l % tt == 0 and w % MXU_DIM == 0
    assert n_state % (LANES * S5_CHUNKS_PER_PASS) == 0 and tt % S5_SCAN_UNROLL == 0
    nc, nl = lc // tt, l // tt

    if reverse:
        ctx_idx = lambda i: (nc - 1 - jnp.minimum(i, nc - 1), 0, 0)
        lat_idx = lambda i: (nl - 1 - jnp.maximum(i - nc, 0), 0, 0)
    else:
        ctx_idx = lambda i: (jnp.minimum(i, nc - 1), 0, 0)
        lat_idx = lambda i: (jnp.maximum(i - nc, 0), 0, 0)
    state_tile = pltpu.VMEM((tt * b, n_state), F32)
    return pl.pallas_call(
        functools.partial(_s5_kernel, nc=nc, tt=tt, reverse=reverse),
        grid=(nc + nl,),
        in_specs=[pl.BlockSpec((tt, b, w), ctx_idx),
                  pl.BlockSpec((tt, b, w), lat_idx),
                  _const_spec(b_re.shape), _const_spec(b_im.shape),
                  _const_spec(c_re.shape), _const_spec(c_im.shape),
                  _const_spec(tab.shape)],
        out_specs=pl.BlockSpec((tt, b, w), lat_idx),
        out_shape=jax.ShapeDtypeStruct((l, b, w), F32),
        scratch_shapes=[state_tile, state_tile, state_tile, state_tile,
                        pltpu.VMEM((2, SUBLANES, n_state), F32)],
        name="s5_bwd" if reverse else "s5_fwd",
        compiler_params=_cparams("arbitrary"),
    )(u_ctx, u_lat, b_re, b_im, c_re, c_im, tab)


_NT = (((1,), (1,)), ((), ()))


def _attn_kernel(q_ref, kl_ref, vl_ref, kc_ref, vc_ref, lq_ref, lk_ref, sg_ref, o_ref,
                 qs, m_s, acc, *, tq, tk, rc, lam_init):
    hw = q_ref.shape[1]
    half = hw // 2
    q = q_ref[...]
    lane = lax.broadcasted_iota(jnp.int32, q.shape, 1)
    zero = jnp.zeros_like(q)
    qs[0:tq, :] = jnp.where(lane < half, q, zero)
    qs[tq:2 * tq, :] = jnp.where(lane >= half, q, zero)
    chunks = [slice(c * rc, (c + 1) * rc) for c in range(2 * tq // rc)]

    for rows in chunks:
        s = lax.dot_general(qs[rows, :], kc_ref[...], _NT, preferred_element_type=F32)
        m = jnp.max(s, axis=-1, keepdims=True)
        p = jnp.exp2(s - m)
        acc[rows, :] = jnp.dot(p.astype(BF16), vc_ref[...], preferred_element_type=F32)
        m_s[rows, :] = m

    def kv_block(j, carry):
        off = pl.multiple_of(j * tk, tk)
        for rows in chunks:
            s = lax.dot_general(qs[rows, :], kl_ref[pl.ds(off, tk), :], _NT,
                                preferred_element_type=F32)
            m_prev = m_s[rows, :]
            m_new = jnp.maximum(m_prev, jnp.max(s, axis=-1, keepdims=True))
            alpha = jnp.exp2(m_prev - m_new)
            p = jnp.exp2(s - m_new)
            acc[rows, :] = alpha * acc[rows, :] + jnp.dot(
                p.astype(BF16), vl_ref[pl.ds(off, tk), :], preferred_element_type=F32)
            m_s[rows, :] = m_new
        return carry

    lax.fori_loop(0, kl_ref.shape[0] // tk, kv_block, 0)

    lq = lq_ref[...]
    lk = lk_ref[...]
    lam = (jnp.exp(jnp.sum(lq[0:1, :] * lk[0:1, :], axis=-1, keepdims=True))
           - jnp.exp(jnp.sum(lq[1:2, :] * lk[1:2, :], axis=-1, keepdims=True)) + lam_init)
    o = (acc[0:tq, 0:hw] / acc[0:tq, hw:2 * hw]
         - lam * (acc[tq:2 * tq, 0:hw] / acc[tq:2 * tq, hw:2 * hw]))
    ms = jnp.mean(o * o, axis=-1, keepdims=True)
    o_ref[...] = ((o * lax.rsqrt(ms + EPS) * sg_ref[...]) * (1.0 - lam_init)).astype(o_ref.dtype)


def _attn_call(q, k_lat, v_lat, k_ctx, v_ctx, lam_q, lam_k, subln_g, *, lam_init, tq, tk, rc):
    b, l, w = q.shape
    lc = k_ctx.shape[1]
    hw = w // DIFF_HEADS
    assert l % tq == 0 and l % tk == 0 and (2 * tq) % rc == 0 and hw == LANES
    qmap = lambda bi, hi, qi: (bi, qi, hi)
    kvmap = lambda bi, hi, qi: (bi, 0, hi)
    return pl.pallas_call(
        functools.partial(_attn_kernel, tq=tq, tk=tk, rc=rc, lam_init=lam_init),
        grid=(b, DIFF_HEADS, l // tq),
        in_specs=[pl.BlockSpec((None, tq, hw), qmap),
                  pl.BlockSpec((None, l, hw), kvmap), pl.BlockSpec((None, l, 2 * hw), kvmap),
                  pl.BlockSpec((None, lc, hw), kvmap), pl.BlockSpec((None, lc, 2 * hw), kvmap),
                  _const_spec(lam_q.shape), _const_spec(lam_k.shape), _const_spec(subln_g.shape)],
        out_specs=pl.BlockSpec((None, tq, hw), qmap),
        out_shape=jax.ShapeDtypeStruct((b, l, w), BF16),
        scratch_shapes=[pltpu.VMEM((2 * tq, hw), BF16), pltpu.VMEM((2 * tq, 1), F32),
                        pltpu.VMEM((2 * tq, 2 * hw), F32)],
        name="diff_attn",
        compiler_params=_cparams("parallel", "parallel", "parallel"),
    )(q, k_lat, v_lat, k_ctx, v_ctx, lam_q, lam_k, subln_g)


def _outproj_kernel(x_ref, mod_ref, u_ref, yf_ref, yb_ref, a_ref, d_ref, wg_ref, bg_ref,
                    wos_ref, woa_ref, o_ref):
    y = d_ref[...] * u_ref[...] + yf_ref[...] + yb_ref[...]
    g = jax.nn.gelu(y)
    z = jnp.dot(g.astype(BF16), wg_ref[...], preferred_element_type=F32) + bg_ref[...]
    s = g * jax.nn.sigmoid(z)
    out = (jnp.dot(s.astype(BF16), wos_ref[...], preferred_element_type=F32)
           + jnp.dot(a_ref[...], woa_ref[...], preferred_element_type=F32))
    o_ref[...] = x_ref[...] + mod_ref[5:6, :] * out


def _outproj_call(x, mod, u, yf, yb, a, d_skip, w_glu, b_glu, w_out_s, w_out_a, *, tm):
    b, l, d = x.shape
    w = a.shape[2]
    assert l % tm == 0
    tok = lambda bi, ti: (bi, ti, 0)
    wide = pl.BlockSpec((None, tm, d), tok)
    narrow = pl.BlockSpec((None, tm, w), tok)
    tmajor = pl.BlockSpec((tm, w), lambda bi, ti: (ti, bi))
    return pl.pallas_call(
        _outproj_kernel,
        grid=(b, l // tm),
        in_specs=[wide, pl.BlockSpec((None, N_MOD, d), lambda bi, ti: (bi, 0, 0)),
                  tmajor, tmajor, tmajor, narrow,
                  _const_spec(d_skip.shape), _const_spec(w_glu.shape), _const_spec(b_glu.shape),
                  _const_spec(w_out_s.shape), _const_spec(w_out_a.shape)],
        out_specs=wide,
        out_shape=jax.ShapeDtypeStruct((b, l, d), F32),
        name="outproj",
        compiler_params=_cparams("parallel", "parallel"),
    )(x, mod, u, yf, yb, a, d_skip, w_glu, b_glu, w_out_s, w_out_a)


def _block_diag(blocks):
    g, r, c = blocks.shape
    eye = jnp.eye(g, dtype=blocks.dtype)
    return (eye[:, None, :, None] * blocks[:, :, None, :]).reshape(g * r, g * c)


def kernel(x, c, ctx, c_ctx, w_mod, b_mod, norm_g, ffn_w_in, ffn_w_out, w_in, w_out, ssm_a_re, ssm_a_im, ssm_log_dt, ssm_b_re, ssm_b_im, ssm_c_re, ssm_c_im, ssm_d, w_glu, b_glu, lam_q, lam_k, subln_g, final_g):
    b, l, d = x.shape
    lc = ctx.shape[1]
    depth = w_mod.shape[0]
    assert depth == 1, "context-stream update between layers is not implemented"
    n_groups, n_state_g, n_in_g = ssm_b_re.shape[2:]
    ssm_w = n_groups * n_in_g
    n_state = n_groups * n_state_g
    head_dim = lam_q.shape[-1]
    ctx_row = b
    rows = SUBLANES * (-(-(b + 1) // SUBLANES))

    tm_lat = 512 if l % 512 == 0 else 256
    tm_ctx = 256
    tt_s5 = 64
    tq = 512 if l % 512 == 0 else 256
    tk = 1024 if l % 1024 == 0 else 256
    rc = 256

    layer = 0
    lam_init = 0.8 - 0.6 * math.exp(-0.3 * layer)

    c_rows = jnp.zeros((rows, d), F32).at[:b].set(c).at[b].set(c_ctx)
    mod = _mod_call(c_rows, w_mod[layer], b_mod[layer]).reshape(rows, N_MOD, d)
    lat_row = lambda bi: bi
    ctx_mod_row = lambda bi: ctx_row

    g0, g1, g2 = (norm_g[layer, i].reshape(1, d) for i in range(3))
    w1_in, w1_out = ffn_w_in[layer, 0].astype(BF16), ffn_w_out[layer, 0].astype(BF16)
    w2_in, w2_out = ffn_w_in[layer, 1].astype(BF16), ffn_w_out[layer, 1].astype(BF16)
    w_mix = w_in[layer].astype(BF16)

    x1 = _ffn_call(x, mod, g0, w1_in, w1_out, mod_i=0, mod_row=lat_row, tm=tm_lat)
    ctx1 = _ffn_call(ctx, mod, g0, w1_in, w1_out, mod_i=0, mod_row=ctx_mod_row, tm=tm_ctx)

    tabs = _rope_tables(l, head_dim)
    u_lat, q_lat, k_lat, v_lat = _inproj_call(x1, mod, g1, w_mix, mod_row=lat_row, tm=tm_lat,
                                              rope_tabs=tabs,
                                              q_scale=head_dim ** -0.5 * math.log2(math.e))
    u_ctx, k_ctx, v_ctx = _inproj_call(ctx1, mod, g1, w_mix, mod_row=ctx_mod_row, tm=tm_ctx)

    flat = lambda p: p[layer].reshape(2, n_state)
    log_dt = jnp.repeat(ssm_log_dt[layer], n_state_g, axis=-1)
    b_t = lambda p: p[layer].transpose(0, 3, 1, 2).reshape(2, n_in_g, n_state)
    tab, bbar_re, bbar_im = _s5_prep_call(flat(ssm_a_re), flat(ssm_a_im), log_dt,
                                          b_t(ssm_b_re), b_t(ssm_b_im))
    ys = []
    for direction in range(2):
        to_blocks = lambda m: _block_diag(
            m[direction].reshape(n_in_g, n_groups, n_state_g).transpose(1, 0, 2)).astype(BF16)
        c_blocks = lambda p: _block_diag(p[layer, direction].transpose(0, 2, 1)).astype(BF16)
        y_dir = _s5_call(u_ctx.reshape(lc, b, ssm_w), u_lat.reshape(l, b, ssm_w),
                         to_blocks(bbar_re), to_blocks(bbar_im),
                         c_blocks(ssm_c_re), c_blocks(ssm_c_im), tab[direction],
                         reverse=direction == 1, tt=tt_s5)
        ys.append(y_dir.reshape(l, b * ssm_w))

    a_lat = _attn_call(q_lat, k_lat, v_lat, k_ctx, v_ctx, lam_q[layer], lam_k[layer],
                       subln_g[layer].reshape(1, -1), lam_init=lam_init, tq=tq, tk=tk, rc=rc)

    w_o = w_out[layer].astype(BF16)
    x2 = _outproj_call(x1, mod, u_lat, ys[0], ys[1], a_lat, ssm_d[layer].reshape(1, ssm_w),
                       w_glu[layer].astype(BF16), b_glu[layer].reshape(1, ssm_w),
                       w_o[:ssm_w], w_o[ssm_w:], tm=tm_lat)

    return _ffn_call(x2, mod, g2, w2_in, w2_out, mod_i=2, mod_row=lat_row, tm=tm_lat,
                     final_g=final_g.reshape(1, d))
```

```python
import functools
import math

import jax
import jax.numpy as jnp
from jax import lax
from jax.experimental import pallas as pl
from jax.experimental.pallas import tpu as pltpu

F32 = jnp.float32
BF16 = jnp.bfloat16

EPS = 1e-6
N_MOD = 9
GRID_W = 64
ROPE_BASE = 10000.0
DIFF_HEADS = 4
SUBLANES = 8
LANES = 128
MXU_DIM = 256
VMEM_LIMIT = 56 * 1024 * 1024


def _cparams(*sem):
    return pltpu.CompilerParams(dimension_semantics=sem, vmem_limit_bytes=VMEM_LIMIT)


def _const_spec(shape):
    nd = len(shape)
    return pl.BlockSpec(shape, lambda *_: (0,) * nd, pipeline_mode=pl.Buffered(1))


def _rms_adaln(x, g, shift, scale):
    ms = jnp.mean(x * x, axis=-1, keepdims=True)
    return (x * lax.rsqrt(ms + EPS) * g) * (1.0 + scale) + shift


def _mod_kernel(c_ref, w_ref, b_ref, o_ref):
    c = c_ref[...]
    sc = c * jax.nn.sigmoid(c)
    o_ref[...] = jnp.dot(sc, w_ref[...], precision=lax.Precision.HIGHEST,
                         preferred_element_type=F32) + b_ref[...]


def _mod_call(c_rows, w_mod, b_mod):
    rows, d = c_rows.shape
    n = w_mod.shape[1]
    bn = d
    return pl.pallas_call(
        _mod_kernel,
        grid=(n // bn,),
        in_specs=[pl.BlockSpec((rows, d), lambda j: (0, 0)),
                  pl.BlockSpec((d, bn), lambda j: (0, j)),
                  pl.BlockSpec((1, bn), lambda j: (0, j))],
        out_specs=pl.BlockSpec((rows, bn), lambda j: (0, j)),
        out_shape=jax.ShapeDtypeStruct((rows, n), F32),
        name="mod",
        compiler_params=_cparams("arbitrary"),
    )(c_rows, w_mod, b_mod.reshape(1, n))


def _ffn_kernel(x_ref, mod_ref, g_ref, win_ref, wout_ref, *rest, mod_i, d_ff, fc, final):
    if final:
        fg_ref, o_ref, acc_ref = rest
    else:
        o_ref, acc_ref = rest
    x = x_ref[...]
    shift = mod_ref[3 * mod_i:3 * mod_i + 1, :]
    scale = mod_ref[3 * mod_i + 1:3 * mod_i + 2, :]
    gate = mod_ref[3 * mod_i + 2:3 * mod_i + 3, :]
    h = _rms_adaln(x, g_ref[...], shift, scale).astype(BF16)
    for j in range(d_ff // fc):
        gt = jnp.dot(h, win_ref[:, j * fc:(j + 1) * fc], preferred_element_type=F32)
        up = jnp.dot(h, win_ref[:, d_ff + j * fc:d_ff + (j + 1) * fc], preferred_element_type=F32)
        act = (gt * jax.nn.sigmoid(gt) * up).astype(BF16)
        part = jnp.dot(act, wout_ref[j * fc:(j + 1) * fc, :], preferred_element_type=F32)
        if j == 0:
            acc_ref[...] = part
        else:
            acc_ref[...] += part
    y = x + (0.5 * gate) * acc_ref[...]
    if final:
        ms = jnp.mean(y * y, axis=-1, keepdims=True)
        y = y * lax.rsqrt(ms + EPS) * fg_ref[...]
    o_ref[...] = y


def _ffn_call(x, mod, g, w_in, w_out, *, mod_i, mod_row, tm, final_g=None):
    b, t, d = x.shape
    d_ff = w_out.shape[0]
    fc = MXU_DIM
    assert t % tm == 0 and d_ff % fc == 0
    final = final_g is not None
    in_specs = [pl.BlockSpec((None, tm, d), lambda bi, ti: (bi, ti, 0)),
                pl.BlockSpec((None, N_MOD, d), lambda bi, ti: (mod_row(bi), 0, 0)),
                _const_spec((1, d)), _const_spec(w_in.shape), _const_spec(w_out.shape)]
    args = [x, mod, g, w_in, w_out]
    if final:
        in_specs.append(_const_spec((1, d)))
        args.append(final_g)
    return pl.pallas_call(
        functools.partial(_ffn_kernel, mod_i=mod_i, d_ff=d_ff, fc=fc, final=final),
        grid=(b, t // tm),
        in_specs=in_specs,
        out_specs=pl.BlockSpec((None, tm, d), lambda bi, ti: (bi, ti, 0)),
        out_shape=jax.ShapeDtypeStruct((b, t, d), F32),
        scratch_shapes=[pltpu.VMEM((tm, d), F32)],
        name="ffn_final" if final else "ffn",
        compiler_params=_cparams("parallel", "parallel"),
    )(*args)


def _rope(t, cos, sin_signed, first_half):
    outs = []
    for j in range(t.shape[1] // LANES):
        tj = t[:, j * LANES:(j + 1) * LANES]
        partner = jnp.where(first_half, pltpu.roll(tj, LANES - 16, 1), pltpu.roll(tj, 16, 1))
        outs.append(tj * cos + partner * sin_signed)
    return jnp.concatenate(outs, axis=1)


def _inproj_kernel(x_ref, mod_ref, g_ref, w_ref, *rest, rope, width, q_scale):
    if rope:
        cos_ref, sin_ref, u_ref, q_ref, k_ref, v_ref = rest
    else:
        u_ref, k_ref, v_ref = rest
    h = _rms_adaln(x_ref[...], g_ref[...], mod_ref[3:4, :], mod_ref[4:5, :]).astype(BF16)
    u_ref[...] = jnp.dot(h, w_ref[:, 0:width], preferred_element_type=F32)
    k = jnp.dot(h, w_ref[:, 2 * width:3 * width], preferred_element_type=F32)
    if rope:
        q = jnp.dot(h, w_ref[:, width:2 * width], preferred_element_type=F32)
        cos = cos_ref[...]
        sin_signed = sin_ref[...]
        lane = lax.broadcasted_iota(jnp.int32, (1, LANES), 1)
        first_half = (lane % 32) < 16
        q_ref[...] = (_rope(q, cos, sin_signed, first_half) * q_scale).astype(BF16)
        k = _rope(k, cos, sin_signed, first_half)
    k_ref[...] = k.astype(BF16)
    v = jnp.dot(h, w_ref[:, 3 * width:4 * width], preferred_element_type=F32).astype(BF16)
    ones = jnp.ones((v.shape[0], LANES), BF16)
    for hd in range(width // LANES):
        v_ref[:, 2 * hd * LANES:(2 * hd + 1) * LANES] = v[:, hd * LANES:(hd + 1) * LANES]
        v_ref[:, (2 * hd + 1) * LANES:(2 * hd + 2) * LANES] = ones


def _inproj_call(x, mod, g, w, *, mod_row, tm, rope_tabs=None, q_scale=1.0):
    b, t, d = x.shape
    width = w.shape[1] // 4
    assert t % tm == 0
    rope = rope_tabs is not None
    tok = lambda bi, ti: (bi, ti, 0)
    in_specs = [pl.BlockSpec((None, tm, d), tok),
                pl.BlockSpec((None, N_MOD, d), lambda bi, ti: (mod_row(bi), 0, 0)),
                _const_spec((1, d)), _const_spec(w.shape)]
    args = [x, mod, g, w]
    out_specs = [pl.BlockSpec((tm, width), lambda bi, ti: (ti, bi))]
    out_shape = [jax.ShapeDtypeStruct((t, b * width), F32)]
    n_qk = 1
    if rope:
        in_specs += [pl.BlockSpec((tm, LANES), lambda bi, ti: (ti, 0))] * 2
        args += list(rope_tabs)
        n_qk = 2
    out_specs += [pl.BlockSpec((None, tm, width), tok)] * n_qk
    out_shape += [jax.ShapeDtypeStruct((b, t, width), BF16)] * n_qk
    out_specs.append(pl.BlockSpec((None, tm, 2 * width), tok))
    out_shape.append(jax.ShapeDtypeStruct((b, t, 2 * width), BF16))
    return pl.pallas_call(
        functools.partial(_inproj_kernel, rope=rope, width=width, q_scale=q_scale),
        grid=(b, t // tm),
        in_specs=in_specs, out_specs=out_specs, out_shape=out_shape,
        name="inproj_rope" if rope else "inproj",
        compiler_params=_cparams("parallel", "parallel"),
    )(*args)


def _rope_tables(length, head_dim):
    n_freq = head_dim // 4
    pos = jnp.arange(length, dtype=jnp.int32)
    row = (pos // GRID_W).astype(F32)
    col = (pos % GRID_W).astype(F32)
    inv_freq = ROPE_BASE ** (-jnp.arange(n_freq, dtype=F32) / n_freq)
    ang = jnp.stack([row[:, None] * inv_freq, col[:, None] * inv_freq], axis=1)
    cos = jnp.broadcast_to(jnp.cos(ang)[:, :, None, :], (length, 2, 2, n_freq))
    sign = jnp.array([-1.0, 1.0], F32)[None, None, :, None]
    sin = jnp.sin(ang)[:, :, None, :] * sign
    reps = LANES // head_dim
    cos = jnp.tile(cos.reshape(length, head_dim), (1, reps))
    sin = jnp.tile(jnp.broadcast_to(sin, (length, 2, 2, n_freq)).reshape(length, head_dim), (1, reps))
    return cos, sin


def _s5_prep_kernel(are_ref, aim_ref, ldt_ref, bre_ref, bim_ref, tab_ref, obre_ref, obim_ref):
    n_dir = are_ref.shape[0]
    n_state = are_ref.shape[1]
    for d in range(n_dir):
        a_re = are_ref[d:d + 1, :]
        a_im = aim_ref[d:d + 1, :]
        dt = jnp.exp(ldt_ref[d:d + 1, :])
        mag = jnp.exp(dt * a_re)
        abar_re = mag * jnp.cos(dt * a_im)
        abar_im = mag * jnp.sin(dt * a_im)
        zr = abar_re - 1.0
        zi = abar_im
        den = a_re * a_re + a_im * a_im
        coef_re = (zr * a_re + zi * a_im) / den
        coef_im = (zi * a_re - zr * a_im) / den
        b_re = bre_ref[d]
        b_im = bim_ref[d]
        obre_ref[d] = coef_re * b_re - coef_im * b_im
        obim_ref[d] = coef_re * b_im + coef_im * b_re
        tab_ref[d, 0] = jnp.broadcast_to(abar_re, (SUBLANES, n_state))
        tab_ref[d, 1] = jnp.broadcast_to(abar_im, (SUBLANES, n_state))


def _s5_prep_call(a_re, a_im, log_dt, b_re_t, b_im_t):
    n_dir, n_state = a_re.shape
    h = b_re_t.shape[1]
    return pl.pallas_call(
        _s5_prep_kernel,
        out_shape=[jax.ShapeDtypeStruct((n_dir, 2, SUBLANES, n_state), F32),
                   jax.ShapeDtypeStruct((n_dir, h, n_state), F32),
                   jax.ShapeDtypeStruct((n_dir, h, n_state), F32)],
        name="s5_prep",
    )(a_re, a_im, log_dt, b_re_t, b_im_t)


S5_CHUNKS_PER_PASS = 8
S5_SCAN_UNROLL = 4


def _s5_kernel(uc_ref, ul_ref, bre_ref, bim_ref, cre_ref, cim_ref, tab_ref, y_ref,
               bure, buim, hre, him, carry, *, nc, tt, reverse):
    i = pl.program_id(0)
    n_in = bre_ref.shape[0]
    n_state = bre_ref.shape[1]
    nb = uc_ref.shape[1]
    rows_t = tt * nb
    kblocks = n_in // MXU_DIM
    sw = n_state // kblocks

    @pl.when(i == 0)
    def _():
        carry[...] = jnp.zeros_like(carry)

    def bu(u_ref):
        u = u_ref[...].reshape(rows_t, n_in).astype(BF16)
        for kb in range(kblocks):
            ub = u[:, kb * MXU_DIM:(kb + 1) * MXU_DIM]
            rows = slice(kb * MXU_DIM, (kb + 1) * MXU_DIM)
            cols = slice(kb * sw, (kb + 1) * sw)
            bure[:, cols] = jnp.dot(ub, bre_ref[rows, cols], preferred_element_type=F32)
            buim[:, cols] = jnp.dot(ub, bim_ref[rows, cols], preferred_element_type=F32)

    @pl.when(i < nc)
    def _():
        bu(uc_ref)

    @pl.when(i >= nc)
    def _():
        bu(ul_ref)

    ncp = S5_CHUNKS_PER_PASS
    for lc0 in range(0, n_state // LANES, ncp):
        cols = [slice((lc0 + c) * LANES, (lc0 + c + 1) * LANES) for c in range(ncp)]
        a_re = [tab_ref[0, :, cs] for cs in cols]
        a_im = [tab_ref[1, :, cs] for cs in cols]

        def body(t, car, cols=cols, a_re=a_re, a_im=a_im):
            tok = (tt - 1 - t) if reverse else t
            r0 = pl.multiple_of(tok * nb, nb)
            new = []
            for c in range(ncp):
                cs = cols[c]
                hr, hm = car[2 * c], car[2 * c + 1]
                nr = a_re[c] * hr - a_im[c] * hm + bure[pl.ds(r0, nb), cs]
                nm = a_re[c] * hm + a_im[c] * hr + buim[pl.ds(r0, nb), cs]
                hre[pl.ds(r0, nb), cs] = nr
                him[pl.ds(r0, nb), cs] = nm
                new += [nr, nm]
            return tuple(new)

        car0 = []
        for cs in cols:
            car0 += [carry[0, :, cs], carry[1, :, cs]]
        carf = lax.fori_loop(0, tt, body, tuple(car0), unroll=S5_SCAN_UNROLL)
        for c, cs in enumerate(cols):
            carry[0, :, cs] = carf[2 * c]
            carry[1, :, cs] = carf[2 * c + 1]

    @pl.when(i >= nc)
    def _():
        n_out = cre_ref.shape[1]
        nblocks = n_out // MXU_DIM
        rw = n_state // nblocks
        for ob in range(nblocks):
            rows = slice(ob * rw, (ob + 1) * rw)
            cols = slice(ob * MXU_DIM, (ob + 1) * MXU_DIM)
            y = (jnp.dot(hre[:, rows].astype(BF16), cre_ref[rows, cols], preferred_element_type=F32)
                 - jnp.dot(him[:, rows].astype(BF16), cim_ref[rows, cols], preferred_element_type=F32))
            y_ref[:, :, cols] = y.reshape(tt, nb, MXU_DIM)


def _s5_call(u_ctx, u_lat, b_re, b_im, c_re, c_im, tab, *, reverse, tt):
    lc, b, w = u_ctx.shape
    l = u_lat.shape[0]
    n_state = b_re.shape[1]
    assert b == SUBLANES, "the scan keeps one batch row per sublane"
    assert lc % tt == 0 and l % tt == 0 and w % MXU_DIM == 0
    assert n_state % (LANES * S5_CHUNKS_PER_PASS) == 0 and tt % S5_SCAN_UNROLL == 0
    nc, nl = lc // tt, l // tt

    if reverse:
        ctx_idx = lambda i: (nc - 1 - jnp.minimum(i, nc - 1), 0, 0)
        lat_idx = lambda i: (nl - 1 - jnp.maximum(i - nc, 0), 0, 0)
    else:
        ctx_idx = lambda i: (jnp.minimum(i, nc - 1), 0, 0)
        lat_idx = lambda i: (jnp.maximum(i - nc, 0), 0, 0)
    state_tile = pltpu.VMEM((tt * b, n_state), F32)
    return pl.pallas_call(
        functools.partial(_s5_kernel, nc=nc, tt=tt, reverse=reverse),
        grid=(nc + nl,),
        in_specs=[pl.BlockSpec((tt, b, w), ctx_idx),
                  pl.BlockSpec((tt, b, w), lat_idx),
                  _const_spec(b_re.shape), _const_spec(b_im.shape),
                  _const_spec(c_re.shape), _const_spec(c_im.shape),
                  _const_spec(tab.shape)],
        out_specs=pl.BlockSpec((tt, b, w), lat_idx),
        out_shape=jax.ShapeDtypeStruct((l, b, w), F32),
        scratch_shapes=[state_tile, state_tile, state_tile, state_tile,
                        pltpu.VMEM((2, SUBLANES, n_state), F32)],
        name="s5_bwd" if reverse else "s5_fwd",
        compiler_params=_cparams("arbitrary"),
    )(u_ctx, u_lat, b_re, b_im, c_re, c_im, tab)


_NT = (((1,), (1,)), ((), ()))
ATTN_LOOKAHEAD = 2


def _attn_kernel(q_ref, kl_ref, vl_ref, kc_ref, vc_ref, lq_ref, lk_ref, sg_ref, o_ref,
                 qs, m_s, acc, *, tq, tk, rc, lam_init):
    hw = q_ref.shape[1]
    half = hw // 2
    q = q_ref[...]
    lane = lax.broadcasted_iota(jnp.int32, q.shape, 1)
    zero = jnp.zeros_like(q)
    qs[0:tq, :] = jnp.where(lane < half, q, zero)
    qs[tq:2 * tq, :] = jnp.where(lane >= half, q, zero)
    chunks = [slice(c * rc, (c + 1) * rc) for c in range(2 * tq // rc)]

    for rows in chunks:
        s = lax.dot_general(qs[rows, :], kc_ref[...], _NT, preferred_element_type=F32)
        m = jnp.max(s, axis=-1, keepdims=True)
        p = jnp.exp2(s - m)
        acc[rows, :] = jnp.dot(p.astype(BF16), vc_ref[...], preferred_element_type=F32)
        m_s[rows, :] = m

    steps = [(j * tk, rows) for j in range(kl_ref.shape[0] // tk) for rows in chunks]

    def scores(off, rows):
        return lax.dot_general(qs[rows, :], kl_ref[off:off + tk, :], _NT,
                               preferred_element_type=F32)

    pending = [scores(*st) for st in steps[:ATTN_LOOKAHEAD]]
    for i, (off, rows) in enumerate(steps):
        s = pending.pop(0)
        if i + ATTN_LOOKAHEAD < len(steps):
            pending.append(scores(*steps[i + ATTN_LOOKAHEAD]))
        m_prev = m_s[rows, :]
        m_new = jnp.maximum(m_prev, jnp.max(s, axis=-1, keepdims=True))
        alpha = jnp.exp2(m_prev - m_new)
        p = jnp.exp2(s - m_new)
        acc[rows, :] = alpha * acc[rows, :] + jnp.dot(
            p.astype(BF16), vl_ref[off:off + tk, :], preferred_element_type=F32)
        m_s[rows, :] = m_new

    lq = lq_ref[...]
    lk = lk_ref[...]
    lam = (jnp.exp(jnp.sum(lq[0:1, :] * lk[0:1, :], axis=-1, keepdims=True))
           - jnp.exp(jnp.sum(lq[1:2, :] * lk[1:2, :], axis=-1, keepdims=True)) + lam_init)
    o = (acc[0:tq, 0:hw] / acc[0:tq, hw:2 * hw]
         - lam * (acc[tq:2 * tq, 0:hw] / acc[tq:2 * tq, hw:2 * hw]))
    ms = jnp.mean(o * o, axis=-1, keepdims=True)
    o_ref[...] = ((o * lax.rsqrt(ms + EPS) * sg_ref[...]) * (1.0 - lam_init)).astype(o_ref.dtype)


def _attn_call(q, k_lat, v_lat, k_ctx, v_ctx, lam_q, lam_k, subln_g, *, lam_init, tq, tk, rc):
    b, l, w = q.shape
    lc = k_ctx.shape[1]
    hw = w // DIFF_HEADS
    assert l % tq == 0 and l % tk == 0 and (2 * tq) % rc == 0 and hw == LANES
    qmap = lambda bi, hi, qi: (bi, qi, hi)
    kvmap = lambda bi, hi, qi: (bi, 0, hi)
    return pl.pallas_call(
        functools.partial(_attn_kernel, tq=tq, tk=tk, rc=rc, lam_init=lam_init),
        grid=(b, DIFF_HEADS, l // tq),
        in_specs=[pl.BlockSpec((None, tq, hw), qmap),
                  pl.BlockSpec((None, l, hw), kvmap), pl.BlockSpec((None, l, 2 * hw), kvmap),
                  pl.BlockSpec((None, lc, hw), kvmap), pl.BlockSpec((None, lc, 2 * hw), kvmap),
                  _const_spec(lam_q.shape), _const_spec(lam_k.shape), _const_spec(subln_g.shape)],
        out_specs=pl.BlockSpec((None, tq, hw), qmap),
        out_shape=jax.ShapeDtypeStruct((b, l, w), BF16),
        scratch_shapes=[pltpu.VMEM((2 * tq, hw), BF16), pltpu.VMEM((2 * tq, 1), F32),
                        pltpu.VMEM((2 * tq, 2 * hw), F32)],
        name="diff_attn",
        compiler_params=_cparams("parallel", "parallel", "parallel"),
    )(q, k_lat, v_lat, k_ctx, v_ctx, lam_q, lam_k, subln_g)


def _outproj_kernel(x_ref, mod_ref, u_ref, yf_ref, yb_ref, a_ref, d_ref, wg_ref, bg_ref,
                    wos_ref, woa_ref, o_ref):
    y = d_ref[...] * u_ref[...] + yf_ref[...] + yb_ref[...]
    g = jax.nn.gelu(y)
    z = jnp.dot(g.astype(BF16), wg_ref[...], preferred_element_type=F32) + bg_ref[...]
    s = g * jax.nn.sigmoid(z)
    out = (jnp.dot(s.astype(BF16), wos_ref[...], preferred_element_type=F32)
           + jnp.dot(a_ref[...], woa_ref[...], preferred_element_type=F32))
    o_ref[...] = x_ref[...] + mod_ref[5:6, :] * out


def _outproj_call(x, mod, u, yf, yb, a, d_skip, w_glu, b_glu, w_out_s, w_out_a, *, tm):
    b, l, d = x.shape
    w = a.shape[2]
    assert l % tm == 0
    tok = lambda bi, ti: (bi, ti, 0)
    wide = pl.BlockSpec((None, tm, d), tok)
    narrow = pl.BlockSpec((None, tm, w), tok)
    tmajor = pl.BlockSpec((tm, w), lambda bi, ti: (ti, bi))
    return pl.pallas_call(
        _outproj_kernel,
        grid=(b, l // tm),
        in_specs=[wide, pl.BlockSpec((None, N_MOD, d), lambda bi, ti: (bi, 0, 0)),
                  tmajor, tmajor, tmajor, narrow,
                  _const_spec(d_skip.shape), _const_spec(w_glu.shape), _const_spec(b_glu.shape),
                  _const_spec(w_out_s.shape), _const_spec(w_out_a.shape)],
        out_specs=wide,
        out_shape=jax.ShapeDtypeStruct((b, l, d), F32),
        name="outproj",
        compiler_params=_cparams("parallel", "parallel"),
    )(x, mod, u, yf, yb, a, d_skip, w_glu, b_glu, w_out_s, w_out_a)


def _block_diag(blocks):
    g, r, c = blocks.shape
    eye = jnp.eye(g, dtype=blocks.dtype)
    return (eye[:, None, :, None] * blocks[:, :, None, :]).reshape(g * r, g * c)


def kernel(x, c, ctx, c_ctx, w_mod, b_mod, norm_g, ffn_w_in, ffn_w_out, w_in, w_out, ssm_a_re, ssm_a_im, ssm_log_dt, ssm_b_re, ssm_b_im, ssm_c_re, ssm_c_im, ssm_d, w_glu, b_glu, lam_q, lam_k, subln_g, final_g):
    b, l, d = x.shape
    lc = ctx.shape[1]
    depth = w_mod.shape[0]
    assert depth == 1, "context-stream update between layers is not implemented"
    n_groups, n_state_g, n_in_g = ssm_b_re.shape[2:]
    ssm_w = n_groups * n_in_g
    n_state = n_groups * n_state_g
    head_dim = lam_q.shape[-1]
    ctx_row = b
    rows = SUBLANES * (-(-(b + 1) // SUBLANES))

    tm_lat = 512 if l % 512 == 0 else 256
    tm_ctx = 256
    tt_s5 = 64
    tq = 512 if l % 512 == 0 else 256
    tk = 1024 if l % 1024 == 0 else 256
    rc = 256

    layer = 0
    lam_init = 0.8 - 0.6 * math.exp(-0.3 * layer)

    c_rows = jnp.zeros((rows, d), F32).at[:b].set(c).at[b].set(c_ctx)
    mod = _mod_call(c_rows, w_mod[layer], b_mod[layer]).reshape(rows, N_MOD, d)
    lat_row = lambda bi: bi
    ctx_mod_row = lambda bi: ctx_row

    g0, g1, g2 = (norm_g[layer, i].reshape(1, d) for i in range(3))
    w1_in, w1_out = ffn_w_in[layer, 0].astype(BF16), ffn_w_out[layer, 0].astype(BF16)
    w2_in, w2_out = ffn_w_in[layer, 1].astype(BF16), ffn_w_out[layer, 1].astype(BF16)
    w_mix = w_in[layer].astype(BF16)

    x1 = _ffn_call(x, mod, g0, w1_in, w1_out, mod_i=0, mod_row=lat_row, tm=tm_lat)
    ctx1 = _ffn_call(ctx, mod, g0, w1_in, w1_out, mod_i=0, mod_row=ctx_mod_row, tm=tm_ctx)

    tabs = _rope_tables(l, head_dim)
    u_lat, q_lat, k_lat, v_lat = _inproj_call(x1, mod, g1, w_mix, mod_row=lat_row, tm=tm_lat,
                                              rope_tabs=tabs,
                                              q_scale=head_dim ** -0.5 * math.log2(math.e))
    u_ctx, k_ctx, v_ctx = _inproj_call(ctx1, mod, g1, w_mix, mod_row=ctx_mod_row, tm=tm_ctx)

    flat = lambda p: p[layer].reshape(2, n_state)
    log_dt = jnp.repeat(ssm_log_dt[layer], n_state_g, axis=-1)
    b_t = lambda p: p[layer].transpose(0, 3, 1, 2).reshape(2, n_in_g, n_state)
    tab, bbar_re, bbar_im = _s5_prep_call(flat(ssm_a_re), flat(ssm_a_im), log_dt,
                                          b_t(ssm_b_re), b_t(ssm_b_im))
    ys = []
    for direction in range(2):
        to_blocks = lambda m: _block_diag(
            m[direction].reshape(n_in_g, n_groups, n_state_g).transpose(1, 0, 2)).astype(BF16)
        c_blocks = lambda p: _block_diag(p[layer, direction].transpose(0, 2, 1)).astype(BF16)
        y_dir = _s5_call(u_ctx.reshape(lc, b, ssm_w), u_lat.reshape(l, b, ssm_w),
                         to_blocks(bbar_re), to_blocks(bbar_im),
                         c_blocks(ssm_c_re), c_blocks(ssm_c_im), tab[direction],
                         reverse=direction == 1, tt=tt_s5)
        ys.append(y_dir.reshape(l, b * ssm_w))

    a_lat = _attn_call(q_lat, k_lat, v_lat, k_ctx, v_ctx, lam_q[layer], lam_k[layer],
                       subln_g[layer].reshape(1, -1), lam_init=lam_init, tq=tq, tk=tk, rc=rc)

    w_o = w_out[layer].astype(BF16)
    x2 = _outproj_call(x1, mod, u_lat, ys[0], ys[1], a_lat, ssm_d[layer].reshape(1, ssm_w),
                       w_glu[layer].astype(BF16), b_glu[layer].reshape(1, ssm_w),
                       w_o[:ssm_w], w_o[ssm_w:], tm=tm_lat)

    return _ffn_call(x2, mod, g2, w2_in, w2_out, mod_i=2, mod_row=lat_row, tm=tm_lat,
                     final_g=final_g.reshape(1, d))
```

```python
import functools
import math

import jax
import jax.numpy as jnp
from jax import lax
from jax.experimental import pallas as pl
from jax.experimental.pallas import tpu as pltpu

F32 = jnp.float32
BF16 = jnp.bfloat16

EPS = 1e-6
N_MOD = 9
GRID_W = 64
ROPE_BASE = 10000.0
DIFF_HEADS = 4
SUBLANES = 8
LANES = 128
MXU_DIM = 256
VMEM_LIMIT = 56 * 1024 * 1024


def _cparams(*sem):
    return pltpu.CompilerParams(dimension_semantics=sem, vmem_limit_bytes=VMEM_LIMIT)


def _const_spec(shape):
    nd = len(shape)
    return pl.BlockSpec(shape, lambda *_: (0,) * nd, pipeline_mode=pl.Buffered(1))


def _rms_adaln(x, g, shift, scale):
    ms = jnp.mean(x * x, axis=-1, keepdims=True)
    return (x * lax.rsqrt(ms + EPS) * g) * (1.0 + scale) + shift


def _mod_kernel(c_ref, w_ref, b_ref, o_ref):
    c = c_ref[...]
    sc = c * jax.nn.sigmoid(c)
    o_ref[...] = jnp.dot(sc, w_ref[...], precision=lax.Precision.HIGHEST,
                         preferred_element_type=F32) + b_ref[...]


def _mod_call(c_rows, w_mod, b_mod):
    rows, d = c_rows.shape
    n = w_mod.shape[1]
    bn = d
    return pl.pallas_call(
        _mod_kernel,
        grid=(n // bn,),
        in_specs=[pl.BlockSpec((rows, d), lambda j: (0, 0)),
                  pl.BlockSpec((d, bn), lambda j: (0, j)),
                  pl.BlockSpec((1, bn), lambda j: (0, j))],
        out_specs=pl.BlockSpec((rows, bn), lambda j: (0, j)),
        out_shape=jax.ShapeDtypeStruct((rows, n), F32),
        name="mod",
        compiler_params=_cparams("arbitrary"),
    )(c_rows, w_mod, b_mod.reshape(1, n))


def _ffn_kernel(x_ref, mod_ref, g_ref, win_ref, wout_ref, *rest, mod_i, d_ff, fc, final):
    if final:
        fg_ref, o_ref, acc_ref = rest
    else:
        o_ref, acc_ref = rest
    x = x_ref[...]
    shift = mod_ref[3 * mod_i:3 * mod_i + 1, :]
    scale = mod_ref[3 * mod_i + 1:3 * mod_i + 2, :]
    gate = mod_ref[3 * mod_i + 2:3 * mod_i + 3, :]
    h = _rms_adaln(x, g_ref[...], shift, scale).astype(BF16)
    for j in range(d_ff // fc):
        gt = jnp.dot(h, win_ref[:, j * fc:(j + 1) * fc], preferred_element_type=F32)
        up = jnp.dot(h, win_ref[:, d_ff + j * fc:d_ff + (j + 1) * fc], preferred_element_type=F32)
        act = (gt * jax.nn.sigmoid(gt) * up).astype(BF16)
        part = jnp.dot(act, wout_ref[j * fc:(j + 1) * fc, :], preferred_element_type=F32)
        if j == 0:
            acc_ref[...] = part
        else:
            acc_ref[...] += part
    y = x + (0.5 * gate) * acc_ref[...]
    if final:
        ms = jnp.mean(y * y, axis=-1, keepdims=True)
        y = y * lax.rsqrt(ms + EPS) * fg_ref[...]
    o_ref[...] = y


def _ffn_call(x, mod, g, w_in, w_out, *, mod_i, mod_row, tm, final_g=None):
    b, t, d = x.shape
    d_ff = w_out.shape[0]
    fc = MXU_DIM
    assert t % tm == 0 and d_ff % fc == 0
    final = final_g is not None
    in_specs = [pl.BlockSpec((None, tm, d), lambda bi, ti: (bi, ti, 0)),
                pl.BlockSpec((None, N_MOD, d), lambda bi, ti: (mod_row(bi), 0, 0)),
                _const_spec((1, d)), _const_spec(w_in.shape), _const_spec(w_out.shape)]
    args = [x, mod, g, w_in, w_out]
    if final:
        in_specs.append(_const_spec((1, d)))
        args.append(final_g)
    return pl.pallas_call(
        functools.partial(_ffn_kernel, mod_i=mod_i, d_ff=d_ff, fc=fc, final=final),
        grid=(b, t // tm),
        in_specs=in_specs,
        out_specs=pl.BlockSpec((None, tm, d), lambda bi, ti: (bi, ti, 0)),
        out_shape=jax.ShapeDtypeStruct((b, t, d), F32),
        scratch_shapes=[pltpu.VMEM((tm, d), F32)],
        name="ffn_final" if final else "ffn",
        compiler_params=_cparams("parallel", "parallel"),
    )(*args)


def _rope(t, cos, sin_signed, first_half):
    outs = []
    for j in range(t.shape[1] // LANES):
        tj = t[:, j * LANES:(j + 1) * LANES]
        partner = jnp.where(first_half, pltpu.roll(tj, LANES - 16, 1), pltpu.roll(tj, 16, 1))
        outs.append(tj * cos + partner * sin_signed)
    return jnp.concatenate(outs, axis=1)


def _inproj_kernel(x_ref, mod_ref, g_ref, w_ref, *rest, rope, width, q_scale):
    if rope:
        cos_ref, sin_ref, u_ref, q_ref, k_ref, v_ref = rest
    else:
        u_ref, k_ref, v_ref = rest
    h = _rms_adaln(x_ref[...], g_ref[...], mod_ref[3:4, :], mod_ref[4:5, :]).astype(BF16)
    u_ref[...] = jnp.dot(h, w_ref[:, 0:width], preferred_element_type=F32)
    k = jnp.dot(h, w_ref[:, 2 * width:3 * width], preferred_element_type=F32)
    if rope:
        q = jnp.dot(h, w_ref[:, width:2 * width], preferred_element_type=F32)
        cos = cos_ref[...]
        sin_signed = sin_ref[...]
        lane = lax.broadcasted_iota(jnp.int32, (1, LANES), 1)
        first_half = (lane % 32) < 16
        q_ref[...] = (_rope(q, cos, sin_signed, first_half) * q_scale).astype(BF16)
        k = _rope(k, cos, sin_signed, first_half)
    k_ref[...] = k.astype(BF16)
    v = jnp.dot(h, w_ref[:, 3 * width:4 * width], preferred_element_type=F32).astype(BF16)
    ones = jnp.ones((v.shape[0], LANES), BF16)
    for hd in range(width // LANES):
        v_ref[:, 2 * hd * LANES:(2 * hd + 1) * LANES] = v[:, hd * LANES:(hd + 1) * LANES]
        v_ref[:, (2 * hd + 1) * LANES:(2 * hd + 2) * LANES] = ones


def _inproj_call(x, mod, g, w, *, mod_row, tm, rope_tabs=None, q_scale=1.0):
    b, t, d = x.shape
    width = w.shape[1] // 4
    assert t % tm == 0
    rope = rope_tabs is not None
    tok = lambda bi, ti: (bi, ti, 0)
    in_specs = [pl.BlockSpec((None, tm, d), tok),
                pl.BlockSpec((None, N_MOD, d), lambda bi, ti: (mod_row(bi), 0, 0)),
                _const_spec((1, d)), _const_spec(w.shape)]
    args = [x, mod, g, w]
    out_specs = [pl.BlockSpec((tm, width), lambda bi, ti: (ti, bi))]
    out_shape = [jax.ShapeDtypeStruct((t, b * width), F32)]
    n_qk = 1
    if rope:
        in_specs += [pl.BlockSpec((tm, LANES), lambda bi, ti: (ti, 0))] * 2
        args += list(rope_tabs)
        n_qk = 2
    out_specs += [pl.BlockSpec((None, tm, width), tok)] * n_qk
    out_shape += [jax.ShapeDtypeStruct((b, t, width), BF16)] * n_qk
    out_specs.append(pl.BlockSpec((None, tm, 2 * width), tok))
    out_shape.append(jax.ShapeDtypeStruct((b, t, 2 * width), BF16))
    return pl.pallas_call(
        functools.partial(_inproj_kernel, rope=rope, width=width, q_scale=q_scale),
        grid=(b, t // tm),
        in_specs=in_specs, out_specs=out_specs, out_shape=out_shape,
        name="inproj_rope" if rope else "inproj",
        compiler_params=_cparams("parallel", "parallel"),
    )(*args)


def _rope_tables(length, head_dim):
    n_freq = head_dim // 4
    pos = jnp.arange(length, dtype=jnp.int32)
    row = (pos // GRID_W).astype(F32)
    col = (pos % GRID_W).astype(F32)
    inv_freq = ROPE_BASE ** (-jnp.arange(n_freq, dtype=F32) / n_freq)
    ang = jnp.stack([row[:, None] * inv_freq, col[:, None] * inv_freq], axis=1)
    cos = jnp.broadcast_to(jnp.cos(ang)[:, :, None, :], (length, 2, 2, n_freq))
    sign = jnp.array([-1.0, 1.0], F32)[None, None, :, None]
    sin = jnp.sin(ang)[:, :, None, :] * sign
    reps = LANES // head_dim
    cos = jnp.tile(cos.reshape(length, head_dim), (1, reps))
    sin = jnp.tile(jnp.broadcast_to(sin, (length, 2, 2, n_freq)).reshape(length, head_dim), (1, reps))
    return cos, sin


def _s5_prep_kernel(are_ref, aim_ref, ldt_ref, bre_ref, bim_ref, tab_ref, obre_ref, obim_ref):
    n_dir = are_ref.shape[0]
    n_state = are_ref.shape[1]
    for d in range(n_dir):
        a_re = are_ref[d:d + 1, :]
        a_im = aim_ref[d:d + 1, :]
        dt = jnp.exp(ldt_ref[d:d + 1, :])
        mag = jnp.exp(dt * a_re)
        abar_re = mag * jnp.cos(dt * a_im)
        abar_im = mag * jnp.sin(dt * a_im)
        zr = abar_re - 1.0
        zi = abar_im
        den = a_re * a_re + a_im * a_im
        coef_re = (zr * a_re + zi * a_im) / den
        coef_im = (zi * a_re - zr * a_im) / den
        b_re = bre_ref[d]
        b_im = bim_ref[d]
        obre_ref[d] = coef_re * b_re - coef_im * b_im
        obim_ref[d] = coef_re * b_im + coef_im * b_re
        tab_ref[d, 0] = jnp.broadcast_to(abar_re, (SUBLANES, n_state))
        tab_ref[d, 1] = jnp.broadcast_to(abar_im, (SUBLANES, n_state))


def _s5_prep_call(a_re, a_im, log_dt, b_re_t, b_im_t):
    n_dir, n_state = a_re.shape
    h = b_re_t.shape[1]
    return pl.pallas_call(
        _s5_prep_kernel,
        out_shape=[jax.ShapeDtypeStruct((n_dir, 2, SUBLANES, n_state), F32),
                   jax.ShapeDtypeStruct((n_dir, h, n_state), F32),
                   jax.ShapeDtypeStruct((n_dir, h, n_state), F32)],
        name="s5_prep",
    )(a_re, a_im, log_dt, b_re_t, b_im_t)


S5_CHUNKS_PER_PASS = 8
S5_SCAN_UNROLL = 4


def _s5_kernel(uc_ref, ul_ref, bre_ref, bim_ref, cre_ref, cim_ref, tab_ref, y_ref,
               bure, buim, hre, him, carry, *, nc, tt, reverse):
    i = pl.program_id(0)
    n_in = bre_ref.shape[0]
    n_state = bre_ref.shape[1]
    nb = uc_ref.shape[1]
    rows_t = tt * nb
    kblocks = n_in // MXU_DIM
    sw = n_state // kblocks

    @pl.when(i == 0)
    def _():
        carry[...] = jnp.zeros_like(carry)

    def bu(u_ref):
        u = u_ref[...].reshape(rows_t, n_in).astype(BF16)
        for kb in range(kblocks):
            ub = u[:, kb * MXU_DIM:(kb + 1) * MXU_DIM]
            rows = slice(kb * MXU_DIM, (kb + 1) * MXU_DIM)
            cols = slice(kb * sw, (kb + 1) * sw)
            bure[:, cols] = jnp.dot(ub, bre_ref[rows, cols], preferred_element_type=F32)
            buim[:, cols] = jnp.dot(ub, bim_ref[rows, cols], preferred_element_type=F32)

    @pl.when(i < nc)
    def _():
        bu(uc_ref)

    @pl.when(i >= nc)
    def _():
        bu(ul_ref)

    ncp = S5_CHUNKS_PER_PASS
    for lc0 in range(0, n_state // LANES, ncp):
        cols = [slice((lc0 + c) * LANES, (lc0 + c + 1) * LANES) for c in range(ncp)]
        a_re = [tab_ref[0, :, cs] for cs in cols]
        a_im = [tab_ref[1, :, cs] for cs in cols]

        def body(t, car, cols=cols, a_re=a_re, a_im=a_im):
            tok = (tt - 1 - t) if reverse else t
            r0 = pl.multiple_of(tok * nb, nb)
            new = []
            for c in range(ncp):
                cs = cols[c]
                hr, hm = car[2 * c], car[2 * c + 1]
                nr = a_re[c] * hr - a_im[c] * hm + bure[pl.ds(r0, nb), cs]
                nm = a_re[c] * hm + a_im[c] * hr + buim[pl.ds(r0, nb), cs]
                hre[pl.ds(r0, nb), cs] = nr
                him[pl.ds(r0, nb), cs] = nm
                new += [nr, nm]
            return tuple(new)

        car0 = []
        for cs in cols:
            car0 += [carry[0, :, cs], carry[1, :, cs]]
        carf = lax.fori_loop(0, tt, body, tuple(car0), unroll=S5_SCAN_UNROLL)
        for c, cs in enumerate(cols):
            carry[0, :, cs] = carf[2 * c]
            carry[1, :, cs] = carf[2 * c + 1]

    @pl.when(i >= nc)
    def _():
        n_out = cre_ref.shape[1]
        nblocks = n_out // MXU_DIM
        rw = n_state // nblocks
        for ob in range(nblocks):
            rows = slice(ob * rw, (ob + 1) * rw)
            cols = slice(ob * MXU_DIM, (ob + 1) * MXU_DIM)
            y = (jnp.dot(hre[:, rows].astype(BF16), cre_ref[rows, cols], preferred_element_type=F32)
                 - jnp.dot(him[:, rows].astype(BF16), cim_ref[rows, cols], preferred_element_type=F32))
            y_ref[:, :, cols] = y.reshape(tt, nb, MXU_DIM)


def _s5_call(u_ctx, u_lat, b_re, b_im, c_re, c_im, tab, *, reverse, tt):
    lc, b, w = u_ctx.shape
    l = u_lat.shape[0]
    n_state = b_re.shape[1]
    assert b == SUBLANES, "the scan keeps one batch row per sublane"
    assert lc % tt == 0 and l % tt == 0 and w % MXU_DIM == 0
    assert n_state % (LANES * S5_CHUNKS_PER_PASS) == 0 and tt % S5_SCAN_UNROLL == 0
    nc, nl = lc // tt, l // tt

    if reverse:
        ctx_idx = lambda i: (nc - 1 - jnp.minimum(i, nc - 1), 0, 0)
        lat_idx = lambda i: (nl - 1 - jnp.maximum(i - nc, 0), 0, 0)
    else:
        ctx_idx = lambda i: (jnp.minimum(i, nc - 1), 0, 0)
        lat_idx = lambda i: (jnp.maximum(i - nc, 0), 0, 0)
    state_tile = pltpu.VMEM((tt * b, n_state), F32)
    return pl.pallas_call(
        functools.partial(_s5_kernel, nc=nc, tt=tt, reverse=reverse),
        grid=(nc + nl,),
        in_specs=[pl.BlockSpec((tt, b, w), ctx_idx),
                  pl.BlockSpec((tt, b, w), lat_idx),
                  _const_spec(b_re.shape), _const_spec(b_im.shape),
                  _const_spec(c_re.shape), _const_spec(c_im.shape),
                  _const_spec(tab.shape)],
        out_specs=pl.BlockSpec((tt, b, w), lat_idx),
        out_shape=jax.ShapeDtypeStruct((l, b, w), F32),
        scratch_shapes=[state_tile, state_tile, state_tile, state_tile,
                        pltpu.VMEM((2, SUBLANES, n_state), F32)],
        name="s5_bwd" if reverse else "s5_fwd",
        compiler_params=_cparams("arbitrary"),
    )(u_ctx, u_lat, b_re, b_im, c_re, c_im, tab)


_NT = (((1,), (1,)), ((), ()))
ATTN_LOOKAHEAD = 2


def _attn_kernel(q_ref, kl_ref, vl_ref, kc_ref, vc_ref, lq_ref, lk_ref, sg_ref, o_ref,
                 qs, m_s, acc, *, tq, tk, rc, lam_init):
    hw = q_ref.shape[1]
    half = hw // 2
    q = q_ref[...]
    lane = lax.broadcasted_iota(jnp.int32, q.shape, 1)
    zero = jnp.zeros_like(q)
    qs[0:tq, :] = jnp.where(lane < half, q, zero)
    qs[tq:2 * tq, :] = jnp.where(lane >= half, q, zero)
    chunks = [slice(c * rc, (c + 1) * rc) for c in range(2 * tq // rc)]

    blocks = [(kc_ref, vc_ref, 0, kc_ref.shape[0])]
    blocks += [(kl_ref, vl_ref, j * tk, tk) for j in range(kl_ref.shape[0] // tk)]
    steps = [(bi, ci) for bi in range(len(blocks)) for ci in range(len(chunks))]

    def scores(i):
        bi, ci = steps[i]
        k_ref, _, off, width = blocks[bi]
        return lax.dot_general(qs[chunks[ci], :], k_ref[off:off + width, :], _NT,
                               preferred_element_type=F32)

    pending = [scores(i) for i in range(ATTN_LOOKAHEAD)]
    for i, (bi, ci) in enumerate(steps):
        s = pending.pop(0)
        if i + ATTN_LOOKAHEAD < len(steps):
            pending.append(scores(i + ATTN_LOOKAHEAD))
        _, v_ref, off, width = blocks[bi]
        rows = chunks[ci]
        m_new = jnp.max(s, axis=-1, keepdims=True)
        if bi > 0:
            m_prev = m_s[rows, :]
            m_new = jnp.maximum(m_prev, m_new)
            alpha = jnp.exp2(m_prev - m_new)
        p = jnp.exp2((s - m_new).astype(BF16))
        pv = jnp.dot(p, v_ref[off:off + width, :], preferred_element_type=F32)
        acc[rows, :] = pv if bi == 0 else alpha * acc[rows, :] + pv
        m_s[rows, :] = m_new

    lq = lq_ref[...]
    lk = lk_ref[...]
    lam = (jnp.exp(jnp.sum(lq[0:1, :] * lk[0:1, :], axis=-1, keepdims=True))
           - jnp.exp(jnp.sum(lq[1:2, :] * lk[1:2, :], axis=-1, keepdims=True)) + lam_init)
    o = (acc[0:tq, 0:hw] / acc[0:tq, hw:2 * hw]
         - lam * (acc[tq:2 * tq, 0:hw] / acc[tq:2 * tq, hw:2 * hw]))
    ms = jnp.mean(o * o, axis=-1, keepdims=True)
    o_ref[...] = ((o * lax.rsqrt(ms + EPS) * sg_ref[...]) * (1.0 - lam_init)).astype(o_ref.dtype)


def _attn_call(q, k_lat, v_lat, k_ctx, v_ctx, lam_q, lam_k, subln_g, *, lam_init, tq, tk, rc):
    b, l, w = q.shape
    lc = k_ctx.shape[1]
    hw = w // DIFF_HEADS
    assert l % tq == 0 and l % tk == 0 and (2 * tq) % rc == 0 and hw == LANES
    qmap = lambda bi, hi, qi: (bi, qi, hi)
    kvmap = lambda bi, hi, qi: (bi, 0, hi)
    return pl.pallas_call(
        functools.partial(_attn_kernel, tq=tq, tk=tk, rc=rc, lam_init=lam_init),
        grid=(b, DIFF_HEADS, l // tq),
        in_specs=[pl.BlockSpec((None, tq, hw), qmap),
                  pl.BlockSpec((None, l, hw), kvmap), pl.BlockSpec((None, l, 2 * hw), kvmap),
                  pl.BlockSpec((None, lc, hw), kvmap), pl.BlockSpec((None, lc, 2 * hw), kvmap),
                  _const_spec(lam_q.shape), _const_spec(lam_k.shape), _const_spec(subln_g.shape)],
        out_specs=pl.BlockSpec((None, tq, hw), qmap),
        out_shape=jax.ShapeDtypeStruct((b, l, w), BF16),
        scratch_shapes=[pltpu.VMEM((2 * tq, hw), BF16), pltpu.VMEM((2 * tq, 1), F32),
                        pltpu.VMEM((2 * tq, 2 * hw), F32)],
        name="diff_attn",
        compiler_params=_cparams("parallel", "parallel", "parallel"),
    )(q, k_lat, v_lat, k_ctx, v_ctx, lam_q, lam_k, subln_g)


def _outproj_kernel(x_ref, mod_ref, u_ref, yf_ref, yb_ref, a_ref, d_ref, wg_ref, bg_ref,
                    wos_ref, woa_ref, o_ref):
    y = d_ref[...] * u_ref[...] + yf_ref[...] + yb_ref[...]
    g = jax.nn.gelu(y)
    z = jnp.dot(g.astype(BF16), wg_ref[...], preferred_element_type=F32) + bg_ref[...]
    s = g * jax.nn.sigmoid(z)
    out = (jnp.dot(s.astype(BF16), wos_ref[...], preferred_element_type=F32)
           + jnp.dot(a_ref[...], woa_ref[...], preferred_element_type=F32))
    o_ref[...] = x_ref[...] + mod_ref[5:6, :] * out


def _outproj_call(x, mod, u, yf, yb, a, d_skip, w_glu, b_glu, w_out_s, w_out_a, *, tm):
    b, l, d = x.shape
    w = a.shape[2]
    assert l % tm == 0
    tok = lambda bi, ti: (bi, ti, 0)
    wide = pl.BlockSpec((None, tm, d), tok)
    narrow = pl.BlockSpec((None, tm, w), tok)
    tmajor = pl.BlockSpec((tm, w), lambda bi, ti: (ti, bi))
    return pl.pallas_call(
        _outproj_kernel,
        grid=(b, l // tm),
        in_specs=[wide, pl.BlockSpec((None, N_MOD, d), lambda bi, ti: (bi, 0, 0)),
                  tmajor, tmajor, tmajor, narrow,
                  _const_spec(d_skip.shape), _const_spec(w_glu.shape), _const_spec(b_glu.shape),
                  _const_spec(w_out_s.shape), _const_spec(w_out_a.shape)],
        out_specs=wide,
        out_shape=jax.ShapeDtypeStruct((b, l, d), F32),
        name="outproj",
        compiler_params=_cparams("parallel", "parallel"),
    )(x, mod, u, yf, yb, a, d_skip, w_glu, b_glu, w_out_s, w_out_a)


def _block_diag(blocks):
    g, r, c = blocks.shape
    eye = jnp.eye(g, dtype=blocks.dtype)
    return (eye[:, None, :, None] * blocks[:, :, None, :]).reshape(g * r, g * c)


def kernel(x, c, ctx, c_ctx, w_mod, b_mod, norm_g, ffn_w_in, ffn_w_out, w_in, w_out, ssm_a_re, ssm_a_im, ssm_log_dt, ssm_b_re, ssm_b_im, ssm_c_re, ssm_c_im, ssm_d, w_glu, b_glu, lam_q, lam_k, subln_g, final_g):
    b, l, d = x.shape
    lc = ctx.shape[1]
    depth = w_mod.shape[0]
    assert depth == 1, "context-stream update between layers is not implemented"
    n_groups, n_state_g, n_in_g = ssm_b_re.shape[2:]
    ssm_w = n_groups * n_in_g
    n_state = n_groups * n_state_g
    head_dim = lam_q.shape[-1]
    ctx_row = b
    rows = SUBLANES * (-(-(b + 1) // SUBLANES))

    tm_lat = 512 if l % 512 == 0 else 256
    tm_ctx = 256
    tt_s5 = 64
    tq = 512 if l % 512 == 0 else 256
    tk = 1024 if l % 1024 == 0 else 256
    rc = 256

    layer = 0
    lam_init = 0.8 - 0.6 * math.exp(-0.3 * layer)

    c_rows = jnp.zeros((rows, d), F32).at[:b].set(c).at[b].set(c_ctx)
    mod = _mod_call(c_rows, w_mod[layer], b_mod[layer]).reshape(rows, N_MOD, d)
    lat_row = lambda bi: bi
    ctx_mod_row = lambda bi: ctx_row

    g0, g1, g2 = (norm_g[layer, i].reshape(1, d) for i in range(3))
    w1_in, w1_out = ffn_w_in[layer, 0].astype(BF16), ffn_w_out[layer, 0].astype(BF16)
    w2_in, w2_out = ffn_w_in[layer, 1].astype(BF16), ffn_w_out[layer, 1].astype(BF16)
    w_mix = w_in[layer].astype(BF16)

    x1 = _ffn_call(x, mod, g0, w1_in, w1_out, mod_i=0, mod_row=lat_row, tm=tm_lat)
    ctx1 = _ffn_call(ctx, mod, g0, w1_in, w1_out, mod_i=0, mod_row=ctx_mod_row, tm=tm_ctx)

    tabs = _rope_tables(l, head_dim)
    u_lat, q_lat, k_lat, v_lat = _inproj_call(x1, mod, g1, w_mix, mod_row=lat_row, tm=tm_lat,
                                              rope_tabs=tabs,
                                              q_scale=head_dim ** -0.5 * math.log2(math.e))
    u_ctx, k_ctx, v_ctx = _inproj_call(ctx1, mod, g1, w_mix, mod_row=ctx_mod_row, tm=tm_ctx)

    flat = lambda p: p[layer].reshape(2, n_state)
    log_dt = jnp.repeat(ssm_log_dt[layer], n_state_g, axis=-1)
    b_t = lambda p: p[layer].transpose(0, 3, 1, 2).reshape(2, n_in_g, n_state)
    tab, bbar_re, bbar_im = _s5_prep_call(flat(ssm_a_re), flat(ssm_a_im), log_dt,
                                          b_t(ssm_b_re), b_t(ssm_b_im))
    ys = []
    for direction in range(2):
        to_blocks = lambda m: _block_diag(
            m[direction].reshape(n_in_g, n_groups, n_state_g).transpose(1, 0, 2)).astype(BF16)
        c_blocks = lambda p: _block_diag(p[layer, direction].transpose(0, 2, 1)).astype(BF16)
        y_dir = _s5_call(u_ctx.reshape(lc, b, ssm_w), u_lat.reshape(l, b, ssm_w),
                         to_blocks(bbar_re), to_blocks(bbar_im),
                         c_blocks(ssm_c_re), c_blocks(ssm_c_im), tab[direction],
                         reverse=direction == 1, tt=tt_s5)
        ys.append(y_dir.reshape(l, b * ssm_w))

    a_lat = _attn_call(q_lat, k_lat, v_lat, k_ctx, v_ctx, lam_q[layer], lam_k[layer],
                       subln_g[layer].reshape(1, -1), lam_init=lam_init, tq=tq, tk=tk, rc=rc)

    w_o = w_out[layer].astype(BF16)
    x2 = _outproj_call(x1, mod, u_lat, ys[0], ys[1], a_lat, ssm_d[layer].reshape(1, ssm_w),
                       w_glu[layer].astype(BF16), b_glu[layer].reshape(1, ssm_w),
                       w_o[:ssm_w], w_o[ssm_w:], tm=tm_lat)

    return _ffn_call(x2, mod, g2, w2_in, w2_out, mod_i=2, mod_row=lat_row, tm=tm_lat,
                     final_g=final_g.reshape(1, d))
```

```python
import functools
import math

import jax
import jax.numpy as jnp
from jax import lax
from jax.experimental import pallas as pl
from jax.experimental.pallas import tpu as pltpu

F32 = jnp.float32
BF16 = jnp.bfloat16

EPS = 1e-6
N_MOD = 9
GRID_W = 64
ROPE_BASE = 10000.0
DIFF_HEADS = 4
SUBLANES = 8
LANES = 128
MXU_DIM = 256
VMEM_LIMIT = 56 * 1024 * 1024


def _cparams(*sem):
    return pltpu.CompilerParams(dimension_semantics=sem, vmem_limit_bytes=VMEM_LIMIT)


def _const_spec(shape):
    nd = len(shape)
    return pl.BlockSpec(shape, lambda *_: (0,) * nd, pipeline_mode=pl.Buffered(1))


def _rms_adaln(x, g, shift, scale):
    ms = jnp.mean(x * x, axis=-1, keepdims=True)
    return (x * lax.rsqrt(ms + EPS) * g) * (1.0 + scale) + shift


def _mod_kernel(c_ref, w_ref, b_ref, o_ref):
    c = c_ref[...]
    sc = c * jax.nn.sigmoid(c)
    o_ref[...] = jnp.dot(sc, w_ref[...], precision=lax.Precision.HIGHEST,
                         preferred_element_type=F32) + b_ref[...]


def _mod_call(c_rows, w_mod, b_mod):
    rows, d = c_rows.shape
    n = w_mod.shape[1]
    bn = d
    return pl.pallas_call(
        _mod_kernel,
        grid=(n // bn,),
        in_specs=[pl.BlockSpec((rows, d), lambda j: (0, 0)),
                  pl.BlockSpec((d, bn), lambda j: (0, j)),
                  pl.BlockSpec((1, bn), lambda j: (0, j))],
        out_specs=pl.BlockSpec((rows, bn), lambda j: (0, j)),
        out_shape=jax.ShapeDtypeStruct((rows, n), F32),
        name="mod",
        compiler_params=_cparams("arbitrary"),
    )(c_rows, w_mod, b_mod.reshape(1, n))


def _ffn_kernel(x_ref, mod_ref, g_ref, win_ref, wout_ref, *rest, mod_i, d_ff, fc, final):
    if final:
        fg_ref, o_ref, acc_ref = rest
    else:
        o_ref, acc_ref = rest
    x = x_ref[...]
    shift = mod_ref[3 * mod_i:3 * mod_i + 1, :]
    scale = mod_ref[3 * mod_i + 1:3 * mod_i + 2, :]
    gate = mod_ref[3 * mod_i + 2:3 * mod_i + 3, :]
    h = _rms_adaln(x, g_ref[...], shift, scale).astype(BF16)
    for j in range(d_ff // fc):
        gt = jnp.dot(h, win_ref[:, j * fc:(j + 1) * fc], preferred_element_type=F32)
        up = jnp.dot(h, win_ref[:, d_ff + j * fc:d_ff + (j + 1) * fc], preferred_element_type=F32)
        act = (gt * jax.nn.sigmoid(gt) * up).astype(BF16)
        part = jnp.dot(act, wout_ref[j * fc:(j + 1) * fc, :], preferred_element_type=F32)
        if j == 0:
            acc_ref[...] = part
        else:
            acc_ref[...] += part
    y = x + (0.5 * gate) * acc_ref[...]
    if final:
        ms = jnp.mean(y * y, axis=-1, keepdims=True)
        y = y * lax.rsqrt(ms + EPS) * fg_ref[...]
    o_ref[...] = y


def _ffn_call(x, mod, g, w_in, w_out, *, mod_i, mod_row, tm, final_g=None):
    b, t, d = x.shape
    d_ff = w_out.shape[0]
    fc = MXU_DIM
    assert t % tm == 0 and d_ff % fc == 0
    final = final_g is not None
    in_specs = [pl.BlockSpec((None, tm, d), lambda bi, ti: (bi, ti, 0)),
                pl.BlockSpec((None, N_MOD, d), lambda bi, ti: (mod_row(bi), 0, 0)),
                _const_spec((1, d)), _const_spec(w_in.shape), _const_spec(w_out.shape)]
    args = [x, mod, g, w_in, w_out]
    if final:
        in_specs.append(_const_spec((1, d)))
        args.append(final_g)
    return pl.pallas_call(
        functools.partial(_ffn_kernel, mod_i=mod_i, d_ff=d_ff, fc=fc, final=final),
        grid=(b, t // tm),
        in_specs=in_specs,
        out_specs=pl.BlockSpec((None, tm, d), lambda bi, ti: (bi, ti, 0)),
        out_shape=jax.ShapeDtypeStruct((b, t, d), F32),
        scratch_shapes=[pltpu.VMEM((tm, d), F32)],
        name="ffn_final" if final else "ffn",
        compiler_params=_cparams("parallel", "parallel"),
    )(*args)


def _rope(t, cos, sin_signed, first_half):
    outs = []
    for j in range(t.shape[1] // LANES):
        tj = t[:, j * LANES:(j + 1) * LANES]
        partner = jnp.where(first_half, pltpu.roll(tj, LANES - 16, 1), pltpu.roll(tj, 16, 1))
        outs.append(tj * cos + partner * sin_signed)
    return jnp.concatenate(outs, axis=1)


def _tile_max_norm2(xb, head_dim, out_shape):
    width = xb.shape[1]
    xf = xb.astype(F32)
    row = lax.broadcasted_iota(jnp.int32, (width, LANES), 0)
    col = lax.broadcasted_iota(jnp.int32, (width, LANES), 1)
    indicator = jnp.where(row // head_dim == col, 1.0, 0.0).astype(BF16)
    norm2 = jnp.dot((xf * xf).astype(BF16), indicator, preferred_element_type=F32)
    return jnp.broadcast_to(jnp.max(norm2, axis=0, keepdims=True), out_shape)


def _inproj_kernel(x_ref, mod_ref, g_ref, w_ref, *rest, rope, width, q_scale, head_dim):
    if rope:
        cos_ref, sin_ref, u_ref, q_ref, k_ref, v_ref, kn_ref, qn_ref = rest
    else:
        u_ref, k_ref, v_ref, kn_ref = rest
    h = _rms_adaln(x_ref[...], g_ref[...], mod_ref[3:4, :], mod_ref[4:5, :]).astype(BF16)
    u_ref[...] = jnp.dot(h, w_ref[:, 0:width], preferred_element_type=F32)
    k = jnp.dot(h, w_ref[:, 2 * width:3 * width], preferred_element_type=F32)
    if rope:
        q = jnp.dot(h, w_ref[:, width:2 * width], preferred_element_type=F32)
        cos = cos_ref[...]
        sin_signed = sin_ref[...]
        lane = lax.broadcasted_iota(jnp.int32, (1, LANES), 1)
        first_half = (lane % 32) < 16
        qb = (_rope(q, cos, sin_signed, first_half) * q_scale).astype(BF16)
        q_ref[...] = qb
        qn_ref[...] = _tile_max_norm2(qb, head_dim, qn_ref.shape)
        k = _rope(k, cos, sin_signed, first_half)
    kb = k.astype(BF16)
    k_ref[...] = kb
    kn_ref[...] = _tile_max_norm2(kb, head_dim, kn_ref.shape)
    v = jnp.dot(h, w_ref[:, 3 * width:4 * width], preferred_element_type=F32).astype(BF16)
    ones = jnp.ones((v.shape[0], LANES), BF16)
    for hd in range(width // LANES):
        v_ref[:, 2 * hd * LANES:(2 * hd + 1) * LANES] = v[:, hd * LANES:(hd + 1) * LANES]
        v_ref[:, (2 * hd + 1) * LANES:(2 * hd + 2) * LANES] = ones


def _inproj_call(x, mod, g, w, *, mod_row, tm, head_dim, rope_tabs=None, q_scale=1.0):
    b, t, d = x.shape
    width = w.shape[1] // 4
    assert t % tm == 0
    rope = rope_tabs is not None
    tok = lambda bi, ti: (bi, ti, 0)
    in_specs = [pl.BlockSpec((None, tm, d), tok),
                pl.BlockSpec((None, N_MOD, d), lambda bi, ti: (mod_row(bi), 0, 0)),
                _const_spec((1, d)), _const_spec(w.shape)]
    args = [x, mod, g, w]
    out_specs = [pl.BlockSpec((tm, width), lambda bi, ti: (ti, bi))]
    out_shape = [jax.ShapeDtypeStruct((t, b * width), F32)]
    n_qk = 1
    if rope:
        in_specs += [pl.BlockSpec((tm, LANES), lambda bi, ti: (ti, 0))] * 2
        args += list(rope_tabs)
        n_qk = 2
    out_specs += [pl.BlockSpec((None, tm, width), tok)] * n_qk
    out_shape += [jax.ShapeDtypeStruct((b, t, width), BF16)] * n_qk
    out_specs.append(pl.BlockSpec((None, tm, 2 * width), tok))
    out_shape.append(jax.ShapeDtypeStruct((b, t, 2 * width), BF16))
    for _ in range(n_qk):
        out_specs.append(pl.BlockSpec((None, None, SUBLANES, LANES), lambda bi, ti: (bi, ti, 0, 0)))
        out_shape.append(jax.ShapeDtypeStruct((b, t // tm, SUBLANES, LANES), F32))
    return pl.pallas_call(
        functools.partial(_inproj_kernel, rope=rope, width=width, q_scale=q_scale,
                          head_dim=head_dim),
        grid=(b, t // tm),
        in_specs=in_specs, out_specs=out_specs, out_shape=out_shape,
        name="inproj_rope" if rope else "inproj",
        compiler_params=_cparams("parallel", "parallel"),
    )(*args)


def _rope_tables(length, head_dim):
    n_freq = head_dim // 4
    pos = jnp.arange(length, dtype=jnp.int32)
    row = (pos // GRID_W).astype(F32)
    col = (pos % GRID_W).astype(F32)
    inv_freq = ROPE_BASE ** (-jnp.arange(n_freq, dtype=F32) / n_freq)
    ang = jnp.stack([row[:, None] * inv_freq, col[:, None] * inv_freq], axis=1)
    cos = jnp.broadcast_to(jnp.cos(ang)[:, :, None, :], (length, 2, 2, n_freq))
    sign = jnp.array([-1.0, 1.0], F32)[None, None, :, None]
    sin = jnp.sin(ang)[:, :, None, :] * sign
    reps = LANES // head_dim
    cos = jnp.tile(cos.reshape(length, head_dim), (1, reps))
    sin = jnp.tile(jnp.broadcast_to(sin, (length, 2, 2, n_freq)).reshape(length, head_dim), (1, reps))
    return cos, sin


def _s5_prep_kernel(are_ref, aim_ref, ldt_ref, bre_ref, bim_ref, tab_ref, obre_ref, obim_ref):
    n_dir = are_ref.shape[0]
    n_state = are_ref.shape[1]
    for d in range(n_dir):
        a_re = are_ref[d:d + 1, :]
        a_im = aim_ref[d:d + 1, :]
        dt = jnp.exp(ldt_ref[d:d + 1, :])
        mag = jnp.exp(dt * a_re)
        abar_re = mag * jnp.cos(dt * a_im)
        abar_im = mag * jnp.sin(dt * a_im)
        zr = abar_re - 1.0
        zi = abar_im
        den = a_re * a_re + a_im * a_im
        coef_re = (zr * a_re + zi * a_im) / den
        coef_im = (zi * a_re - zr * a_im) / den
        b_re = bre_ref[d]
        b_im = bim_ref[d]
        obre_ref[d] = coef_re * b_re - coef_im * b_im
        obim_ref[d] = coef_re * b_im + coef_im * b_re
        tab_ref[d, 0] = jnp.broadcast_to(abar_re, (SUBLANES, n_state))
        tab_ref[d, 1] = jnp.broadcast_to(abar_im, (SUBLANES, n_state))


def _s5_prep_call(a_re, a_im, log_dt, b_re_t, b_im_t):
    n_dir, n_state = a_re.shape
    h = b_re_t.shape[1]
    return pl.pallas_call(
        _s5_prep_kernel,
        out_shape=[jax.ShapeDtypeStruct((n_dir, 2, SUBLANES, n_state), F32),
                   jax.ShapeDtypeStruct((n_dir, h, n_state), F32),
                   jax.ShapeDtypeStruct((n_dir, h, n_state), F32)],
        name="s5_prep",
    )(a_re, a_im, log_dt, b_re_t, b_im_t)


S5_CHUNKS_PER_PASS = 8
S5_SCAN_UNROLL = 4


def _s5_kernel(uc_ref, ul_ref, bre_ref, bim_ref, cre_ref, cim_ref, tab_ref, y_ref,
               bure, buim, hre, him, carry, *, nc, tt, reverse):
    i = pl.program_id(0)
    n_in = bre_ref.shape[0]
    n_state = bre_ref.shape[1]
    nb = uc_ref.shape[1]
    rows_t = tt * nb
    kblocks = n_in // MXU_DIM
    sw = n_state // kblocks

    @pl.when(i == 0)
    def _():
        carry[...] = jnp.zeros_like(carry)

    def bu(u_ref):
        u = u_ref[...].reshape(rows_t, n_in).astype(BF16)
        for kb in range(kblocks):
            ub = u[:, kb * MXU_DIM:(kb + 1) * MXU_DIM]
            rows = slice(kb * MXU_DIM, (kb + 1) * MXU_DIM)
            cols = slice(kb * sw, (kb + 1) * sw)
            bure[:, cols] = jnp.dot(ub, bre_ref[rows, cols], preferred_element_type=F32)
            buim[:, cols] = jnp.dot(ub, bim_ref[rows, cols], preferred_element_type=F32)

    @pl.when(i < nc)
    def _():
        bu(uc_ref)

    @pl.when(i >= nc)
    def _():
        bu(ul_ref)

    ncp = S5_CHUNKS_PER_PASS
    for lc0 in range(0, n_state // LANES, ncp):
        cols = [slice((lc0 + c) * LANES, (lc0 + c + 1) * LANES) for c in range(ncp)]
        a_re = [tab_ref[0, :, cs] for cs in cols]
        a_im = [tab_ref[1, :, cs] for cs in cols]

        def body(t, car, cols=cols, a_re=a_re, a_im=a_im):
            tok = (tt - 1 - t) if reverse else t
            r0 = pl.multiple_of(tok * nb, nb)
            new = []
            for c in range(ncp):
                cs = cols[c]
                hr, hm = car[2 * c], car[2 * c + 1]
                nr = a_re[c] * hr - a_im[c] * hm + bure[pl.ds(r0, nb), cs]
                nm = a_re[c] * hm + a_im[c] * hr + buim[pl.ds(r0, nb), cs]
                hre[pl.ds(r0, nb), cs] = nr
                him[pl.ds(r0, nb), cs] = nm
                new += [nr, nm]
            return tuple(new)

        car0 = []
        for cs in cols:
            car0 += [carry[0, :, cs], carry[1, :, cs]]
        carf = lax.fori_loop(0, tt, body, tuple(car0), unroll=S5_SCAN_UNROLL)
        for c, cs in enumerate(cols):
            carry[0, :, cs] = carf[2 * c]
            carry[1, :, cs] = carf[2 * c + 1]

    @pl.when(i >= nc)
    def _():
        n_out = cre_ref.shape[1]
        nblocks = n_out // MXU_DIM
        rw = n_state // nblocks
        for ob in range(nblocks):
            rows = slice(ob * rw, (ob + 1) * rw)
            cols = slice(ob * MXU_DIM, (ob + 1) * MXU_DIM)
            y = (jnp.dot(hre[:, rows].astype(BF16), cre_ref[rows, cols], preferred_element_type=F32)
                 - jnp.dot(him[:, rows].astype(BF16), cim_ref[rows, cols], preferred_element_type=F32))
            y_ref[:, :, cols] = y.reshape(tt, nb, MXU_DIM)


def _s5_call(u_ctx, u_lat, b_re, b_im, c_re, c_im, tab, *, reverse, tt):
    lc, b, w = u_ctx.shape
    l = u_lat.shape[0]
    n_state = b_re.shape[1]
    assert b == SUBLANES, "the scan keeps one batch row per sublane"
    assert lc % tt == 0 and l % tt == 0 and w % MXU_DIM == 0
    assert n_state % (LANES * S5_CHUNKS_PER_PASS) == 0 and tt % S5_SCAN_UNROLL == 0
    nc, nl = lc // tt, l // tt

    if reverse:
        ctx_idx = lambda i: (nc - 1 - jnp.minimum(i, nc - 1), 0, 0)
        lat_idx = lambda i: (nl - 1 - jnp.maximum(i - nc, 0), 0, 0)
    else:
        ctx_idx = lambda i: (jnp.minimum(i, nc - 1), 0, 0)
        lat_idx = lambda i: (jnp.maximum(i - nc, 0), 0, 0)
    state_tile = pltpu.VMEM((tt * b, n_state), F32)
    return pl.pallas_call(
        functools.partial(_s5_kernel, nc=nc, tt=tt, reverse=reverse),
        grid=(nc + nl,),
        in_specs=[pl.BlockSpec((tt, b, w), ctx_idx),
                  pl.BlockSpec((tt, b, w), lat_idx),
                  _const_spec(b_re.shape), _const_spec(b_im.shape),
                  _const_spec(c_re.shape), _const_spec(c_im.shape),
                  _const_spec(tab.shape)],
        out_specs=pl.BlockSpec((tt, b, w), lat_idx),
        out_shape=jax.ShapeDtypeStruct((l, b, w), F32),
        scratch_shapes=[state_tile, state_tile, state_tile, state_tile,
                        pltpu.VMEM((2, SUBLANES, n_state), F32)],
        name="s5_bwd" if reverse else "s5_fwd",
        compiler_params=_cparams("arbitrary"),
    )(u_ctx, u_lat, b_re, b_im, c_re, c_im, tab)


_NT = (((1,), (1,)), ((), ()))
ATTN_BOUND_LIMIT = 48.0
ATTN_BOUND_MARGIN = 1.02


def _attn_kernel(bound_ref, q_ref, kl_ref, vl_ref, kc_ref, vc_ref, lq_ref, lk_ref, sg_ref, o_ref,
                 qs, m_s, acc, *, tq, tk, rc, lam_init):
    hw = q_ref.shape[1]
    half = hw // 2
    q = q_ref[...]
    lane = lax.broadcasted_iota(jnp.int32, q.shape, 1)
    zero = jnp.zeros_like(q)
    qs[0:tq, :] = jnp.where(lane < half, q, zero)
    qs[tq:2 * tq, :] = jnp.where(lane >= half, q, zero)
    chunks = [slice(c * rc, (c + 1) * rc) for c in range(2 * tq // rc)]
    blocks = [(kc_ref, vc_ref, 0, kc_ref.shape[0])]
    blocks += [(kl_ref, vl_ref, j * tk, tk) for j in range(kl_ref.shape[0] // tk)]

    base = ((pl.program_id(0) * pl.num_programs(1) + pl.program_id(1)) * pl.num_programs(2)
            + pl.program_id(2)) * 2
    bounds = [bound_ref[base], bound_ref[base + 1]]
    bound_ok = jnp.maximum(bounds[0], bounds[1]) <= ATTN_BOUND_LIMIT

    @pl.when(bound_ok)
    def _():
        for rows in chunks:
            m = bounds[rows.start // tq]
            total = None
            for k_ref, v_ref, off, width in blocks:
                s = lax.dot_general(qs[rows, :], k_ref[off:off + width, :], _NT,
                                    preferred_element_type=F32)
                p = jnp.exp2(s - m).astype(BF16)
                pv = jnp.dot(p, v_ref[off:off + width, :], preferred_element_type=F32)
                total = pv if total is None else total + pv
            acc[rows, :] = total

    @pl.when(jnp.logical_not(bound_ok))
    def _():
        for rows in chunks:
            s = lax.dot_general(qs[rows, :], kc_ref[...], _NT, preferred_element_type=F32)
            m = jnp.max(s, axis=-1, keepdims=True)
            p = jnp.exp2((s - m).astype(BF16))
            acc[rows, :] = jnp.dot(p, vc_ref[...], preferred_element_type=F32)
            m_s[rows, :] = m

        def kv_block(j, carry):
            off = pl.multiple_of(j * tk, tk)
            for rows in chunks:
                s = lax.dot_general(qs[rows, :], kl_ref[pl.ds(off, tk), :], _NT,
                                    preferred_element_type=F32)
                m_prev = m_s[rows, :]
                m_new = jnp.maximum(m_prev, jnp.max(s, axis=-1, keepdims=True))
                alpha = jnp.exp2(m_prev - m_new)
                p = jnp.exp2((s - m_new).astype(BF16))
                acc[rows, :] = alpha * acc[rows, :] + jnp.dot(
                    p, vl_ref[pl.ds(off, tk), :], preferred_element_type=F32)
                m_s[rows, :] = m_new
            return carry

        lax.fori_loop(0, kl_ref.shape[0] // tk, kv_block, 0)

    lq = lq_ref[...]
    lk = lk_ref[...]
    lam = (jnp.exp(jnp.sum(lq[0:1, :] * lk[0:1, :], axis=-1, keepdims=True))
           - jnp.exp(jnp.sum(lq[1:2, :] * lk[1:2, :], axis=-1, keepdims=True)) + lam_init)
    o = (acc[0:tq, 0:hw] / acc[0:tq, hw:2 * hw]
         - lam * (acc[tq:2 * tq, 0:hw] / acc[tq:2 * tq, hw:2 * hw]))
    ms = jnp.mean(o * o, axis=-1, keepdims=True)
    o_ref[...] = ((o * lax.rsqrt(ms + EPS) * sg_ref[...]) * (1.0 - lam_init)).astype(o_ref.dtype)


def _attn_call(bounds, q, k_lat, v_lat, k_ctx, v_ctx, lam_q, lam_k, subln_g, *,
               lam_init, tq, tk, rc):
    b, l, w = q.shape
    lc = k_ctx.shape[1]
    hw = w // DIFF_HEADS
    assert l % tq == 0 and l % tk == 0 and tq % rc == 0 and hw == LANES
    assert bounds.shape == (b * DIFF_HEADS * (l // tq) * 2,)
    qmap = lambda bi, hi, qi, bnd: (bi, qi, hi)
    kvmap = lambda bi, hi, qi, bnd: (bi, 0, hi)
    const = lambda shape: pl.BlockSpec(shape, lambda bi, hi, qi, bnd: (0,) * len(shape),
                                       pipeline_mode=pl.Buffered(1))
    return pl.pallas_call(
        functools.partial(_attn_kernel, tq=tq, tk=tk, rc=rc, lam_init=lam_init),
        grid_spec=pltpu.PrefetchScalarGridSpec(
            num_scalar_prefetch=1,
            grid=(b, DIFF_HEADS, l // tq),
            in_specs=[pl.BlockSpec((None, tq, hw), qmap),
                      pl.BlockSpec((None, l, hw), kvmap), pl.BlockSpec((None, l, 2 * hw), kvmap),
                      pl.BlockSpec((None, lc, hw), kvmap), pl.BlockSpec((None, lc, 2 * hw), kvmap),
                      const(lam_q.shape), const(lam_k.shape), const(subln_g.shape)],
            out_specs=pl.BlockSpec((None, tq, hw), qmap),
            scratch_shapes=[pltpu.VMEM((2 * tq, hw), BF16), pltpu.VMEM((2 * tq, 1), F32),
                            pltpu.VMEM((2 * tq, 2 * hw), F32)]),
        out_shape=jax.ShapeDtypeStruct((b, l, w), BF16),
        name="diff_attn",
        compiler_params=_cparams("parallel", "parallel", "parallel"),
    )(bounds, q, k_lat, v_lat, k_ctx, v_ctx, lam_q, lam_k, subln_g)


def _outproj_kernel(x_ref, mod_ref, u_ref, yf_ref, yb_ref, a_ref, d_ref, wg_ref, bg_ref,
                    wos_ref, woa_ref, o_ref):
    y = d_ref[...] * u_ref[...] + yf_ref[...] + yb_ref[...]
    g = jax.nn.gelu(y)
    z = jnp.dot(g.astype(BF16), wg_ref[...], preferred_element_type=F32) + bg_ref[...]
    s = g * jax.nn.sigmoid(z)
    out = (jnp.dot(s.astype(BF16), wos_ref[...], preferred_element_type=F32)
           + jnp.dot(a_ref[...], woa_ref[...], preferred_element_type=F32))
    o_ref[...] = x_ref[...] + mod_ref[5:6, :] * out


def _outproj_call(x, mod, u, yf, yb, a, d_skip, w_glu, b_glu, w_out_s, w_out_a, *, tm):
    b, l, d = x.shape
    w = a.shape[2]
    assert l % tm == 0
    tok = lambda bi, ti: (bi, ti, 0)
    wide = pl.BlockSpec((None, tm, d), tok)
    narrow = pl.BlockSpec((None, tm, w), tok)
    tmajor = pl.BlockSpec((tm, w), lambda bi, ti: (ti, bi))
    return pl.pallas_call(
        _outproj_kernel,
        grid=(b, l // tm),
        in_specs=[wide, pl.BlockSpec((None, N_MOD, d), lambda bi, ti: (bi, 0, 0)),
                  tmajor, tmajor, tmajor, narrow,
                  _const_spec(d_skip.shape), _const_spec(w_glu.shape), _const_spec(b_glu.shape),
                  _const_spec(w_out_s.shape), _const_spec(w_out_a.shape)],
        out_specs=wide,
        out_shape=jax.ShapeDtypeStruct((b, l, d), F32),
        name="outproj",
        compiler_params=_cparams("parallel", "parallel"),
    )(x, mod, u, yf, yb, a, d_skip, w_glu, b_glu, w_out_s, w_out_a)


def _block_diag(blocks):
    g, r, c = blocks.shape
    eye = jnp.eye(g, dtype=blocks.dtype)
    return (eye[:, None, :, None] * blocks[:, :, None, :]).reshape(g * r, g * c)


def kernel(x, c, ctx, c_ctx, w_mod, b_mod, norm_g, ffn_w_in, ffn_w_out, w_in, w_out, ssm_a_re, ssm_a_im, ssm_log_dt, ssm_b_re, ssm_b_im, ssm_c_re, ssm_c_im, ssm_d, w_glu, b_glu, lam_q, lam_k, subln_g, final_g):
    b, l, d = x.shape
    lc = ctx.shape[1]
    depth = w_mod.shape[0]
    assert depth == 1, "context-stream update between layers is not implemented"
    n_groups, n_state_g, n_in_g = ssm_b_re.shape[2:]
    ssm_w = n_groups * n_in_g
    n_state = n_groups * n_state_g
    head_dim = lam_q.shape[-1]
    ctx_row = b
    rows = SUBLANES * (-(-(b + 1) // SUBLANES))

    tm_lat = 512 if l % 512 == 0 else 256
    tm_ctx = 256
    tt_s5 = 64
    tq = 512 if l % 512 == 0 else 256
    tk = 1024 if l % 1024 == 0 else 256
    rc = 256

    layer = 0
    lam_init = 0.8 - 0.6 * math.exp(-0.3 * layer)

    c_rows = jnp.zeros((rows, d), F32).at[:b].set(c).at[b].set(c_ctx)
    mod = _mod_call(c_rows, w_mod[layer], b_mod[layer]).reshape(rows, N_MOD, d)
    lat_row = lambda bi: bi
    ctx_mod_row = lambda bi: ctx_row

    g0, g1, g2 = (norm_g[layer, i].reshape(1, d) for i in range(3))
    w1_in, w1_out = ffn_w_in[layer, 0].astype(BF16), ffn_w_out[layer, 0].astype(BF16)
    w2_in, w2_out = ffn_w_in[layer, 1].astype(BF16), ffn_w_out[layer, 1].astype(BF16)
    w_mix = w_in[layer].astype(BF16)

    x1 = _ffn_call(x, mod, g0, w1_in, w1_out, mod_i=0, mod_row=lat_row, tm=tm_lat)
    ctx1 = _ffn_call(ctx, mod, g0, w1_in, w1_out, mod_i=0, mod_row=ctx_mod_row, tm=tm_ctx)

    tabs = _rope_tables(l, head_dim)
    u_lat, q_lat, k_lat, v_lat, kn_lat, qn_lat = _inproj_call(
        x1, mod, g1, w_mix, mod_row=lat_row, tm=tm_lat, head_dim=head_dim, rope_tabs=tabs,
        q_scale=head_dim ** -0.5 * math.log2(math.e))
    u_ctx, k_ctx, v_ctx, kn_ctx = _inproj_call(ctx1, mod, g1, w_mix, mod_row=ctx_mod_row,
                                               tm=tm_ctx, head_dim=head_dim)

    flat = lambda p: p[layer].reshape(2, n_state)
    log_dt = jnp.repeat(ssm_log_dt[layer], n_state_g, axis=-1)
    b_t = lambda p: p[layer].transpose(0, 3, 1, 2).reshape(2, n_in_g, n_state)
    tab, bbar_re, bbar_im = _s5_prep_call(flat(ssm_a_re), flat(ssm_a_im), log_dt,
                                          b_t(ssm_b_re), b_t(ssm_b_im))
    ys = []
    for direction in range(2):
        to_blocks = lambda m: _block_diag(
            m[direction].reshape(n_in_g, n_groups, n_state_g).transpose(1, 0, 2)).astype(BF16)
        c_blocks = lambda p: _block_diag(p[layer, direction].transpose(0, 2, 1)).astype(BF16)
        y_dir = _s5_call(u_ctx.reshape(lc, b, ssm_w), u_lat.reshape(l, b, ssm_w),
                         to_blocks(bbar_re), to_blocks(bbar_im),
                         c_blocks(ssm_c_re), c_blocks(ssm_c_im), tab[direction],
                         reverse=direction == 1, tt=tt_s5)
        ys.append(y_dir.reshape(l, b * ssm_w))

    n_hc = 2 * DIFF_HEADS
    kn2 = jnp.maximum(kn_lat.max(axis=(1, 2)), kn_ctx.max(axis=(1, 2)))[:, None, :n_hc]
    qn2 = qn_lat[:, :, 0, :n_hc]
    if tq >= tm_lat:
        qn2 = qn2.reshape(b, l // tq, tq // tm_lat, n_hc).max(axis=2)
    else:
        qn2 = jnp.repeat(qn2, tm_lat // tq, axis=1)
    bounds = jnp.sqrt(qn2 * kn2) * ATTN_BOUND_MARGIN
    bounds = bounds.reshape(b, l // tq, DIFF_HEADS, 2).transpose(0, 2, 1, 3).reshape(-1)
    a_lat = _attn_call(bounds, q_lat, k_lat, v_lat, k_ctx, v_ctx, lam_q[layer], lam_k[layer],
                       subln_g[layer].reshape(1, -1), lam_init=lam_init, tq=tq, tk=tk, rc=rc)

    w_o = w_out[layer].astype(BF16)
    x2 = _outproj_call(x1, mod, u_lat, ys[0], ys[1], a_lat, ssm_d[layer].reshape(1, ssm_w),
                       w_glu[layer].astype(BF16), b_glu[layer].reshape(1, ssm_w),
                       w_o[:ssm_w], w_o[ssm_w:], tm=tm_lat)

    return _ffn_call(x2, mod, g2, w2_in, w2_out, mod_i=2, mod_row=lat_row, tm=tm_lat,
                     final_g=final_g.reshape(1, d))
```

```python
import functools
import math

import jax
import jax.numpy as jnp
from jax import lax
from jax.experimental import pallas as pl
from jax.experimental.pallas import tpu as pltpu

F32 = jnp.float32
BF16 = jnp.bfloat16

EPS = 1e-6
N_MOD = 9
GRID_W = 64
ROPE_BASE = 10000.0
DIFF_HEADS = 4
SUBLANES = 8
LANES = 128
MXU_DIM = 256
VMEM_LIMIT = 56 * 1024 * 1024


def _cparams(*sem):
    return pltpu.CompilerParams(dimension_semantics=sem, vmem_limit_bytes=VMEM_LIMIT)


def _const_spec(shape):
    nd = len(shape)
    return pl.BlockSpec(shape, lambda *_: (0,) * nd, pipeline_mode=pl.Buffered(1))


def _rms_adaln(x, g, shift, scale):
    ms = jnp.mean(x * x, axis=-1, keepdims=True)
    return (x * lax.rsqrt(ms + EPS) * g) * (1.0 + scale) + shift


def _mod_kernel(c_ref, w_ref, b_ref, o_ref):
    c = c_ref[...]
    sc = c * jax.nn.sigmoid(c)
    o_ref[...] = jnp.dot(sc, w_ref[...], precision=lax.Precision.HIGHEST,
                         preferred_element_type=F32) + b_ref[...]


def _mod_call(c_rows, w_mod, b_mod):
    rows, d = c_rows.shape
    n = w_mod.shape[1]
    bn = d
    return pl.pallas_call(
        _mod_kernel,
        grid=(n // bn,),
        in_specs=[pl.BlockSpec((rows, d), lambda j: (0, 0)),
                  pl.BlockSpec((d, bn), lambda j: (0, j)),
                  pl.BlockSpec((1, bn), lambda j: (0, j))],
        out_specs=pl.BlockSpec((rows, bn), lambda j: (0, j)),
        out_shape=jax.ShapeDtypeStruct((rows, n), F32),
        name="mod",
        compiler_params=_cparams("arbitrary"),
    )(c_rows, w_mod, b_mod.reshape(1, n))


def _ffn_kernel(x_ref, mod_ref, g_ref, win_ref, wout_ref, *rest, mod_i, d_ff, fc, final):
    if final:
        fg_ref, o_ref, acc_ref = rest
    else:
        o_ref, acc_ref = rest
    x = x_ref[...]
    shift = mod_ref[3 * mod_i:3 * mod_i + 1, :]
    scale = mod_ref[3 * mod_i + 1:3 * mod_i + 2, :]
    gate = mod_ref[3 * mod_i + 2:3 * mod_i + 3, :]
    h = _rms_adaln(x, g_ref[...], shift, scale).astype(BF16)
    for j in range(d_ff // fc):
        gt = jnp.dot(h, win_ref[:, j * fc:(j + 1) * fc], preferred_element_type=F32)
        up = jnp.dot(h, win_ref[:, d_ff + j * fc:d_ff + (j + 1) * fc], preferred_element_type=F32)
        act = (gt * jax.nn.sigmoid(gt) * up).astype(BF16)
        part = jnp.dot(act, wout_ref[j * fc:(j + 1) * fc, :], preferred_element_type=F32)
        if j == 0:
            acc_ref[...] = part
        else:
            acc_ref[...] += part
    y = x + (0.5 * gate) * acc_ref[...]
    if final:
        ms = jnp.mean(y * y, axis=-1, keepdims=True)
        y = y * lax.rsqrt(ms + EPS) * fg_ref[...]
    o_ref[...] = y


def _ffn_call(x, mod, g, w_in, w_out, *, mod_i, mod_row, tm, final_g=None):
    b, t, d = x.shape
    d_ff = w_out.shape[0]
    fc = MXU_DIM
    assert t % tm == 0 and d_ff % fc == 0
    final = final_g is not None
    in_specs = [pl.BlockSpec((None, tm, d), lambda bi, ti: (bi, ti, 0)),
                pl.BlockSpec((None, N_MOD, d), lambda bi, ti: (mod_row(bi), 0, 0)),
                _const_spec((1, d)), _const_spec(w_in.shape), _const_spec(w_out.shape)]
    args = [x, mod, g, w_in, w_out]
    if final:
        in_specs.append(_const_spec((1, d)))
        args.append(final_g)
    return pl.pallas_call(
        functools.partial(_ffn_kernel, mod_i=mod_i, d_ff=d_ff, fc=fc, final=final),
        grid=(b, t // tm),
        in_specs=in_specs,
        out_specs=pl.BlockSpec((None, tm, d), lambda bi, ti: (bi, ti, 0)),
        out_shape=jax.ShapeDtypeStruct((b, t, d), F32),
        scratch_shapes=[pltpu.VMEM((tm, d), F32)],
        name="ffn_final" if final else "ffn",
        compiler_params=_cparams("parallel", "parallel"),
    )(*args)


def _rope(t, cos, sin_signed, first_half):
    outs = []
    for j in range(t.shape[1] // LANES):
        tj = t[:, j * LANES:(j + 1) * LANES]
        partner = jnp.where(first_half, pltpu.roll(tj, LANES - 16, 1), pltpu.roll(tj, 16, 1))
        outs.append(tj * cos + partner * sin_signed)
    return jnp.concatenate(outs, axis=1)


def _tile_max_norm2(xb, indicator, out_shape):
    xf = xb.astype(F32)
    norm2 = jnp.dot((xf * xf).astype(BF16), indicator, preferred_element_type=F32)
    return jnp.broadcast_to(jnp.max(norm2, axis=0, keepdims=True), out_shape)


def _inproj_kernel(x_ref, mod_ref, g_ref, w_ref, ind_ref, *rest, rope, width, q_scale):
    if rope:
        cos_ref, sin_ref, u_ref, q_ref, k_ref, v_ref, kn_ref, qn_ref = rest
    else:
        u_ref, k_ref, v_ref, kn_ref = rest
    h = _rms_adaln(x_ref[...], g_ref[...], mod_ref[3:4, :], mod_ref[4:5, :]).astype(BF16)
    u_ref[...] = jnp.dot(h, w_ref[:, 0:width], preferred_element_type=F32)
    k = jnp.dot(h, w_ref[:, 2 * width:3 * width], preferred_element_type=F32)
    if rope:
        q = jnp.dot(h, w_ref[:, width:2 * width], preferred_element_type=F32)
        cos = cos_ref[...]
        sin_signed = sin_ref[...]
        lane = lax.broadcasted_iota(jnp.int32, (1, LANES), 1)
        first_half = (lane % 32) < 16
        qb = (_rope(q, cos, sin_signed, first_half) * q_scale).astype(BF16)
        q_ref[...] = qb
        qn_ref[...] = _tile_max_norm2(qb, ind_ref[...], qn_ref.shape)
        k = _rope(k, cos, sin_signed, first_half)
    kb = k.astype(BF16)
    k_ref[...] = kb
    kn_ref[...] = _tile_max_norm2(kb, ind_ref[...], kn_ref.shape)
    v = jnp.dot(h, w_ref[:, 3 * width:4 * width], preferred_element_type=F32).astype(BF16)
    ones = jnp.ones((v.shape[0], LANES), BF16)
    for hd in range(width // LANES):
        v_ref[:, 2 * hd * LANES:(2 * hd + 1) * LANES] = v[:, hd * LANES:(hd + 1) * LANES]
        v_ref[:, (2 * hd + 1) * LANES:(2 * hd + 2) * LANES] = ones


def _inproj_call(x, mod, g, w, *, mod_row, tm, head_dim, rope_tabs=None, q_scale=1.0):
    b, t, d = x.shape
    width = w.shape[1] // 4
    assert t % tm == 0
    rope = rope_tabs is not None
    tok = lambda bi, ti: (bi, ti, 0)
    segment = jnp.arange(width, dtype=jnp.int32)[:, None] // head_dim
    indicator = (segment == jnp.arange(LANES, dtype=jnp.int32)[None, :]).astype(BF16)
    in_specs = [pl.BlockSpec((None, tm, d), tok),
                pl.BlockSpec((None, N_MOD, d), lambda bi, ti: (mod_row(bi), 0, 0)),
                _const_spec((1, d)), _const_spec(w.shape), _const_spec(indicator.shape)]
    args = [x, mod, g, w, indicator]
    out_specs = [pl.BlockSpec((tm, width), lambda bi, ti: (ti, bi))]
    out_shape = [jax.ShapeDtypeStruct((t, b * width), F32)]
    n_qk = 1
    if rope:
        in_specs += [pl.BlockSpec((tm, LANES), lambda bi, ti: (ti, 0))] * 2
        args += list(rope_tabs)
        n_qk = 2
    out_specs += [pl.BlockSpec((None, tm, width), tok)] * n_qk
    out_shape += [jax.ShapeDtypeStruct((b, t, width), BF16)] * n_qk
    out_specs.append(pl.BlockSpec((None, tm, 2 * width), tok))
    out_shape.append(jax.ShapeDtypeStruct((b, t, 2 * width), BF16))
    for _ in range(n_qk):
        out_specs.append(pl.BlockSpec((None, None, SUBLANES, LANES), lambda bi, ti: (bi, ti, 0, 0)))
        out_shape.append(jax.ShapeDtypeStruct((b, t // tm, SUBLANES, LANES), F32))
    return pl.pallas_call(
        functools.partial(_inproj_kernel, rope=rope, width=width, q_scale=q_scale),
        grid=(b, t // tm),
        in_specs=in_specs, out_specs=out_specs, out_shape=out_shape,
        name="inproj_rope" if rope else "inproj",
        compiler_params=_cparams("parallel", "parallel"),
    )(*args)


def _rope_tables(length, head_dim):
    n_freq = head_dim // 4
    pos = jnp.arange(length, dtype=jnp.int32)
    row = (pos // GRID_W).astype(F32)
    col = (pos % GRID_W).astype(F32)
    inv_freq = ROPE_BASE ** (-jnp.arange(n_freq, dtype=F32) / n_freq)
    ang = jnp.stack([row[:, None] * inv_freq, col[:, None] * inv_freq], axis=1)
    cos = jnp.broadcast_to(jnp.cos(ang)[:, :, None, :], (length, 2, 2, n_freq))
    sign = jnp.array([-1.0, 1.0], F32)[None, None, :, None]
    sin = jnp.sin(ang)[:, :, None, :] * sign
    reps = LANES // head_dim
    cos = jnp.tile(cos.reshape(length, head_dim), (1, reps))
    sin = jnp.tile(jnp.broadcast_to(sin, (length, 2, 2, n_freq)).reshape(length, head_dim), (1, reps))
    return cos, sin


def _s5_prep_kernel(are_ref, aim_ref, ldt_ref, bre_ref, bim_ref, tab_ref, obre_ref, obim_ref):
    n_dir = are_ref.shape[0]
    n_state = are_ref.shape[1]
    for d in range(n_dir):
        a_re = are_ref[d:d + 1, :]
        a_im = aim_ref[d:d + 1, :]
        dt = jnp.exp(ldt_ref[d:d + 1, :])
        mag = jnp.exp(dt * a_re)
        abar_re = mag * jnp.cos(dt * a_im)
        abar_im = mag * jnp.sin(dt * a_im)
        zr = abar_re - 1.0
        zi = abar_im
        den = a_re * a_re + a_im * a_im
        coef_re = (zr * a_re + zi * a_im) / den
        coef_im = (zi * a_re - zr * a_im) / den
        b_re = bre_ref[d]
        b_im = bim_ref[d]
        obre_ref[d] = coef_re * b_re - coef_im * b_im
        obim_ref[d] = coef_re * b_im + coef_im * b_re
        tab_ref[d, 0] = jnp.broadcast_to(abar_re, (SUBLANES, n_state))
        tab_ref[d, 1] = jnp.broadcast_to(abar_im, (SUBLANES, n_state))


def _s5_prep_call(a_re, a_im, log_dt, b_re_t, b_im_t):
    n_dir, n_state = a_re.shape
    h = b_re_t.shape[1]
    return pl.pallas_call(
        _s5_prep_kernel,
        out_shape=[jax.ShapeDtypeStruct((n_dir, 2, SUBLANES, n_state), F32),
                   jax.ShapeDtypeStruct((n_dir, h, n_state), F32),
                   jax.ShapeDtypeStruct((n_dir, h, n_state), F32)],
        name="s5_prep",
    )(a_re, a_im, log_dt, b_re_t, b_im_t)


S5_CHUNKS_PER_PASS = 8


def _s5_kernel(uc_ref, ul_ref, bre_ref, bim_ref, cre_ref, cim_ref, tab_ref, y_ref,
               bu0re, bu0im, bu1re, bu1im, h0re, h0im, h1re, h1im, carry, *, nc, tt, reverse):
    s = pl.program_id(0)
    n_in = bre_ref.shape[0]
    n_state = bre_ref.shape[1]
    nb = uc_ref.shape[1]
    rows_t = tt * nb
    kblocks = n_in // MXU_DIM
    sw = n_state // kblocks

    @pl.when(s == 0)
    def _():
        carry[...] = jnp.zeros_like(carry)
        for buf in (bu1re, bu1im, h0re, h0im):
            buf[...] = jnp.zeros_like(buf)

    def stage_bu(dst_re, dst_im):
        u = jnp.where(s < nc, uc_ref[...], ul_ref[...]).reshape(rows_t, n_in).astype(BF16)
        for kb in range(kblocks):
            ub = u[:, kb * MXU_DIM:(kb + 1) * MXU_DIM]
            rows = slice(kb * MXU_DIM, (kb + 1) * MXU_DIM)
            cols = slice(kb * sw, (kb + 1) * sw)
            dst_re[:, cols] = jnp.dot(ub, bre_ref[rows, cols], preferred_element_type=F32)
            dst_im[:, cols] = jnp.dot(ub, bim_ref[rows, cols], preferred_element_type=F32)

    def stage_scan(src_re, src_im, dst_re, dst_im):
        ncp = S5_CHUNKS_PER_PASS
        for lc0 in range(0, n_state // LANES, ncp):
            cols = [slice((lc0 + c) * LANES, (lc0 + c + 1) * LANES) for c in range(ncp)]
            a_re = [tab_ref[0, :, cs] for cs in cols]
            a_im = [tab_ref[1, :, cs] for cs in cols]
            state = [(carry[0, :, cs], carry[1, :, cs]) for cs in cols]
            for t0 in range(0, tt, 2):
                toks = (tt - 1 - t0, tt - 2 - t0) if reverse else (t0, t0 + 1)
                pair = slice(min(toks) * nb, (min(toks) + 2) * nb)
                for c, cs in enumerate(cols):
                    hr, hm = state[c]
                    out = {}
                    for tok in toks:
                        rows = slice(tok * nb, (tok + 1) * nb)
                        hr, hm = (a_re[c] * hr - a_im[c] * hm + src_re[rows, cs],
                                  a_re[c] * hm + a_im[c] * hr + src_im[rows, cs])
                        out[tok] = (hr, hm)
                    lo, hi = out[min(toks)], out[max(toks)]
                    dst_re[pair, cs] = jnp.concatenate([lo[0], hi[0]], axis=0).astype(BF16)
                    dst_im[pair, cs] = jnp.concatenate([lo[1], hi[1]], axis=0).astype(BF16)
                    state[c] = (hr, hm)
            for c, cs in enumerate(cols):
                carry[0, :, cs] = state[c][0]
                carry[1, :, cs] = state[c][1]

    def stage_readout(src_re, src_im):
        n_out = cre_ref.shape[1]
        nblocks = n_out // MXU_DIM
        rw = n_state // nblocks
        for ob in range(nblocks):
            rows = slice(ob * rw, (ob + 1) * rw)
            cols = slice(ob * MXU_DIM, (ob + 1) * MXU_DIM)
            y = (jnp.dot(src_re[:, rows], cre_ref[rows, cols], preferred_element_type=F32)
                 - jnp.dot(src_im[:, rows], cim_ref[rows, cols], preferred_element_type=F32))
            y_ref[:, :, cols] = y.reshape(tt, nb, MXU_DIM)

    @pl.when(s % 2 == 0)
    def _():
        stage_bu(bu0re, bu0im)
        stage_scan(bu1re, bu1im, h1re, h1im)
        stage_readout(h0re, h0im)

    @pl.when(s % 2 == 1)
    def _():
        stage_bu(bu1re, bu1im)
        stage_scan(bu0re, bu0im, h0re, h0im)
        stage_readout(h1re, h1im)


def _s5_call(u_ctx, u_lat, b_re, b_im, c_re, c_im, tab, *, reverse, tt):
    lc, b, w = u_ctx.shape
    l = u_lat.shape[0]
    n_state = b_re.shape[1]
    assert b == SUBLANES, "the scan keeps one batch row per sublane"
    assert lc % tt == 0 and l % tt == 0 and w % MXU_DIM == 0
    assert n_state % (LANES * S5_CHUNKS_PER_PASS) == 0
    nc, nl = lc // tt, l // tt

    def ctx_tile(s):
        j = jnp.minimum(s, nc - 1)
        return nc - 1 - j if reverse else j

    def lat_tile(s):
        j = jnp.clip(s - nc, 0, nl - 1)
        return nl - 1 - j if reverse else j

    bu_tile = pltpu.VMEM((tt * b, n_state), F32)
    h_tile = pltpu.VMEM((tt * b, n_state), BF16)
    return pl.pallas_call(
        functools.partial(_s5_kernel, nc=nc, tt=tt, reverse=reverse),
        grid=(nc + nl + 2,),
        in_specs=[pl.BlockSpec((tt, b, w), lambda s: (ctx_tile(s), 0, 0)),
                  pl.BlockSpec((tt, b, w), lambda s: (lat_tile(s), 0, 0)),
                  _const_spec(b_re.shape), _const_spec(b_im.shape),
                  _const_spec(c_re.shape), _const_spec(c_im.shape),
                  _const_spec(tab.shape)],
        out_specs=pl.BlockSpec((tt, b, w), lambda s: (lat_tile(s - 2), 0, 0)),
        out_shape=jax.ShapeDtypeStruct((l, b, w), F32),
        scratch_shapes=[bu_tile] * 4 + [h_tile] * 4 + [pltpu.VMEM((2, SUBLANES, n_state), F32)],
        name="s5_bwd" if reverse else "s5_fwd",
        compiler_params=_cparams("arbitrary"),
    )(u_ctx, u_lat, b_re, b_im, c_re, c_im, tab)


_NT = (((1,), (1,)), ((), ()))
ATTN_BOUND_LIMIT = 48.0
ATTN_BOUND_MARGIN = 1.02


def _attn_kernel(bound_ref, q_ref, kl_ref, vl_ref, kc_ref, vc_ref, lq_ref, lk_ref, sg_ref, o_ref,
                 qs, m_s, acc, *, tq, tk, rc, lam_init):
    hw = q_ref.shape[1]
    half = hw // 2
    q = q_ref[...]
    lane = lax.broadcasted_iota(jnp.int32, q.shape, 1)
    zero = jnp.zeros_like(q)
    qs[0:tq, :] = jnp.where(lane < half, q, zero)
    qs[tq:2 * tq, :] = jnp.where(lane >= half, q, zero)
    chunks = [slice(c * rc, (c + 1) * rc) for c in range(2 * tq // rc)]
    blocks = [(kc_ref, vc_ref, 0, kc_ref.shape[0])]
    blocks += [(kl_ref, vl_ref, j * tk, tk) for j in range(kl_ref.shape[0] // tk)]

    base = ((pl.program_id(0) * pl.num_programs(1) + pl.program_id(1)) * pl.num_programs(2)
            + pl.program_id(2)) * 2
    bounds = [bound_ref[base], bound_ref[base + 1]]
    bound_ok = jnp.maximum(bounds[0], bounds[1]) <= ATTN_BOUND_LIMIT

    @pl.when(bound_ok)
    def _():
        for rows in chunks:
            m = bounds[rows.start // tq]
            total = None
            for k_ref, v_ref, off, width in blocks:
                s = lax.dot_general(qs[rows, :], k_ref[off:off + width, :], _NT,
                                    preferred_element_type=F32)
                p = jnp.exp2(s - m).astype(BF16)
                pv = jnp.dot(p, v_ref[off:off + width, :], preferred_element_type=F32)
                total = pv if total is None else total + pv
            acc[rows, :] = total

    @pl.when(jnp.logical_not(bound_ok))
    def _():
        for rows in chunks:
            s = lax.dot_general(qs[rows, :], kc_ref[...], _NT, preferred_element_type=F32)
            m = jnp.max(s, axis=-1, keepdims=True)
            p = jnp.exp2((s - m).astype(BF16))
            acc[rows, :] = jnp.dot(p, vc_ref[...], preferred_element_type=F32)
            m_s[rows, :] = m

        def kv_block(j, carry):
            off = pl.multiple_of(j * tk, tk)
            for rows in chunks:
                s = lax.dot_general(qs[rows, :], kl_ref[pl.ds(off, tk), :], _NT,
                                    preferred_element_type=F32)
                m_prev = m_s[rows, :]
                m_new = jnp.maximum(m_prev, jnp.max(s, axis=-1, keepdims=True))
                alpha = jnp.exp2(m_prev - m_new)
                p = jnp.exp2((s - m_new).astype(BF16))
                acc[rows, :] = alpha * acc[rows, :] + jnp.dot(
                    p, vl_ref[pl.ds(off, tk), :], preferred_element_type=F32)
                m_s[rows, :] = m_new
            return carry

        lax.fori_loop(0, kl_ref.shape[0] // tk, kv_block, 0)

    lq = lq_ref[...]
    lk = lk_ref[...]
    lam = (jnp.exp(jnp.sum(lq[0:1, :] * lk[0:1, :], axis=-1, keepdims=True))
           - jnp.exp(jnp.sum(lq[1:2, :] * lk[1:2, :], axis=-1, keepdims=True)) + lam_init)
    o = (acc[0:tq, 0:hw] / acc[0:tq, hw:2 * hw]
         - lam * (acc[tq:2 * tq, 0:hw] / acc[tq:2 * tq, hw:2 * hw]))
    ms = jnp.mean(o * o, axis=-1, keepdims=True)
    o_ref[...] = ((o * lax.rsqrt(ms + EPS) * sg_ref[...]) * (1.0 - lam_init)).astype(o_ref.dtype)


def _attn_call(bounds, q, k_lat, v_lat, k_ctx, v_ctx, lam_q, lam_k, subln_g, *,
               lam_init, tq, tk, rc):
    b, l, w = q.shape
    lc = k_ctx.shape[1]
    hw = w // DIFF_HEADS
    assert l % tq == 0 and l % tk == 0 and tq % rc == 0 and hw == LANES
    assert bounds.shape == (b * DIFF_HEADS * (l // tq) * 2,)
    qmap = lambda bi, hi, qi, bnd: (bi, qi, hi)
    kvmap = lambda bi, hi, qi, bnd: (bi, 0, hi)
    const = lambda shape: pl.BlockSpec(shape, lambda bi, hi, qi, bnd: (0,) * len(shape),
                                       pipeline_mode=pl.Buffered(1))
    return pl.pallas_call(
        functools.partial(_attn_kernel, tq=tq, tk=tk, rc=rc, lam_init=lam_init),
        grid_spec=pltpu.PrefetchScalarGridSpec(
            num_scalar_prefetch=1,
            grid=(b, DIFF_HEADS, l // tq),
            in_specs=[pl.BlockSpec((None, tq, hw), qmap),
                      pl.BlockSpec((None, l, hw), kvmap), pl.BlockSpec((None, l, 2 * hw), kvmap),
                      pl.BlockSpec((None, lc, hw), kvmap), pl.BlockSpec((None, lc, 2 * hw), kvmap),
                      const(lam_q.shape), const(lam_k.shape), const(subln_g.shape)],
            out_specs=pl.BlockSpec((None, tq, hw), qmap),
            scratch_shapes=[pltpu.VMEM((2 * tq, hw), BF16), pltpu.VMEM((2 * tq, 1), F32),
                            pltpu.VMEM((2 * tq, 2 * hw), F32)]),
        out_shape=jax.ShapeDtypeStruct((b, l, w), BF16),
        name="diff_attn",
        compiler_params=_cparams("parallel", "parallel", "parallel"),
    )(bounds, q, k_lat, v_lat, k_ctx, v_ctx, lam_q, lam_k, subln_g)


def _outproj_kernel(x_ref, mod_ref, u_ref, yf_ref, yb_ref, a_ref, d_ref, wg_ref, bg_ref,
                    wos_ref, woa_ref, o_ref):
    y = d_ref[...] * u_ref[...] + yf_ref[...] + yb_ref[...]
    g = jax.nn.gelu(y)
    z = jnp.dot(g.astype(BF16), wg_ref[...], preferred_element_type=F32) + bg_ref[...]
    s = g * jax.nn.sigmoid(z)
    out = (jnp.dot(s.astype(BF16), wos_ref[...], preferred_element_type=F32)
           + jnp.dot(a_ref[...], woa_ref[...], preferred_element_type=F32))
    o_ref[...] = x_ref[...] + mod_ref[5:6, :] * out


def _outproj_call(x, mod, u, yf, yb, a, d_skip, w_glu, b_glu, w_out_s, w_out_a, *, tm):
    b, l, d = x.shape
    w = a.shape[2]
    assert l % tm == 0
    tok = lambda bi, ti: (bi, ti, 0)
    wide = pl.BlockSpec((None, tm, d), tok)
    narrow = pl.BlockSpec((None, tm, w), tok)
    tmajor = pl.BlockSpec((tm, w), lambda bi, ti: (ti, bi))
    return pl.pallas_call(
        _outproj_kernel,
        grid=(b, l // tm),
        in_specs=[wide, pl.BlockSpec((None, N_MOD, d), lambda bi, ti: (bi, 0, 0)),
                  tmajor, tmajor, tmajor, narrow,
                  _const_spec(d_skip.shape), _const_spec(w_glu.shape), _const_spec(b_glu.shape),
                  _const_spec(w_out_s.shape), _const_spec(w_out_a.shape)],
        out_specs=wide,
        out_shape=jax.ShapeDtypeStruct((b, l, d), F32),
        name="outproj",
        compiler_params=_cparams("parallel", "parallel"),
    )(x, mod, u, yf, yb, a, d_skip, w_glu, b_glu, w_out_s, w_out_a)


def _block_diag(blocks):
    g, r, c = blocks.shape
    eye = jnp.eye(g, dtype=blocks.dtype)
    return (eye[:, None, :, None] * blocks[:, :, None, :]).reshape(g * r, g * c)


def kernel(x, c, ctx, c_ctx, w_mod, b_mod, norm_g, ffn_w_in, ffn_w_out, w_in, w_out, ssm_a_re, ssm_a_im, ssm_log_dt, ssm_b_re, ssm_b_im, ssm_c_re, ssm_c_im, ssm_d, w_glu, b_glu, lam_q, lam_k, subln_g, final_g):
    b, l, d = x.shape
    lc = ctx.shape[1]
    depth = w_mod.shape[0]
    assert depth == 1, "context-stream update between layers is not implemented"
    n_groups, n_state_g, n_in_g = ssm_b_re.shape[2:]
    ssm_w = n_groups * n_in_g
    n_state = n_groups * n_state_g
    head_dim = lam_q.shape[-1]
    ctx_row = b
    rows = SUBLANES * (-(-(b + 1) // SUBLANES))

    tm_lat = 512 if l % 512 == 0 else 256
    tm_ctx = 256
    tt_s5 = 64
    tq = 512 if l % 512 == 0 else 256
    tk = 1024 if l % 1024 == 0 else 256
    rc = 256

    layer = 0
    lam_init = 0.8 - 0.6 * math.exp(-0.3 * layer)

    c_rows = jnp.zeros((rows, d), F32).at[:b].set(c).at[b].set(c_ctx)
    mod = _mod_call(c_rows, w_mod[layer], b_mod[layer]).reshape(rows, N_MOD, d)
    lat_row = lambda bi: bi
    ctx_mod_row = lambda bi: ctx_row

    g0, g1, g2 = (norm_g[layer, i].reshape(1, d) for i in range(3))
    w1_in, w1_out = ffn_w_in[layer, 0].astype(BF16), ffn_w_out[layer, 0].astype(BF16)
    w2_in, w2_out = ffn_w_in[layer, 1].astype(BF16), ffn_w_out[layer, 1].astype(BF16)
    w_mix = w_in[layer].astype(BF16)

    x1 = _ffn_call(x, mod, g0, w1_in, w1_out, mod_i=0, mod_row=lat_row, tm=tm_lat)
    ctx1 = _ffn_call(ctx, mod, g0, w1_in, w1_out, mod_i=0, mod_row=ctx_mod_row, tm=tm_ctx)

    tabs = _rope_tables(l, head_dim)
    u_lat, q_lat, k_lat, v_lat, kn_lat, qn_lat = _inproj_call(
        x1, mod, g1, w_mix, mod_row=lat_row, tm=tm_lat, head_dim=head_dim, rope_tabs=tabs,
        q_scale=head_dim ** -0.5 * math.log2(math.e))
    u_ctx, k_ctx, v_ctx, kn_ctx = _inproj_call(ctx1, mod, g1, w_mix, mod_row=ctx_mod_row,
                                               tm=tm_ctx, head_dim=head_dim)

    flat = lambda p: p[layer].reshape(2, n_state)
    log_dt = jnp.repeat(ssm_log_dt[layer], n_state_g, axis=-1)
    b_t = lambda p: p[layer].transpose(0, 3, 1, 2).reshape(2, n_in_g, n_state)
    tab, bbar_re, bbar_im = _s5_prep_call(flat(ssm_a_re), flat(ssm_a_im), log_dt,
                                          b_t(ssm_b_re), b_t(ssm_b_im))
    ys = []
    for direction in range(2):
        to_blocks = lambda m: _block_diag(
            m[direction].reshape(n_in_g, n_groups, n_state_g).transpose(1, 0, 2)).astype(BF16)
        c_blocks = lambda p: _block_diag(p[layer, direction].transpose(0, 2, 1)).astype(BF16)
        y_dir = _s5_call(u_ctx.reshape(lc, b, ssm_w), u_lat.reshape(l, b, ssm_w),
                         to_blocks(bbar_re), to_blocks(bbar_im),
                         c_blocks(ssm_c_re), c_blocks(ssm_c_im), tab[direction],
                         reverse=direction == 1, tt=tt_s5)
        ys.append(y_dir.reshape(l, b * ssm_w))

    n_hc = 2 * DIFF_HEADS
    kn2 = jnp.maximum(kn_lat.max(axis=(1, 2)), kn_ctx.max(axis=(1, 2)))[:, None, :n_hc]
    qn2 = qn_lat[:, :, 0, :n_hc]
    if tq >= tm_lat:
        qn2 = qn2.reshape(b, l // tq, tq // tm_lat, n_hc).max(axis=2)
    else:
        qn2 = jnp.repeat(qn2, tm_lat // tq, axis=1)
    bounds = jnp.sqrt(qn2 * kn2) * ATTN_BOUND_MARGIN
    bounds = bounds.reshape(b, l // tq, DIFF_HEADS, 2).transpose(0, 2, 1, 3).reshape(-1)
    a_lat = _attn_call(bounds, q_lat, k_lat, v_lat, k_ctx, v_ctx, lam_q[layer], lam_k[layer],
                       subln_g[layer].reshape(1, -1), lam_init=lam_init, tq=tq, tk=tk, rc=rc)

    w_o = w_out[layer].astype(BF16)
    x2 = _outproj_call(x1, mod, u_lat, ys[0], ys[1], a_lat, ssm_d[layer].reshape(1, ssm_w),
                       w_glu[layer].astype(BF16), b_glu[layer].reshape(1, ssm_w),
                       w_o[:ssm_w], w_o[ssm_w:], tm=tm_lat)

    return _ffn_call(x2, mod, g2, w2_in, w2_out, mod_i=2, mod_row=lat_row, tm=tm_lat,
                     final_g=final_g.reshape(1, d))
```

```python
import functools
import math

import jax
import jax.numpy as jnp
from jax import lax
from jax.experimental import pallas as pl
from jax.experimental.pallas import tpu as pltpu

F32 = jnp.float32
BF16 = jnp.bfloat16

EPS = 1e-6
N_MOD = 9
GRID_W = 64
ROPE_BASE = 10000.0
DIFF_HEADS = 4
SUBLANES = 8
LANES = 128
MXU_DIM = 256
VMEM_LIMIT = 56 * 1024 * 1024


def _cparams(*sem):
    return pltpu.CompilerParams(dimension_semantics=sem, vmem_limit_bytes=VMEM_LIMIT)


def _const_spec(shape):
    nd = len(shape)
    return pl.BlockSpec(shape, lambda *_: (0,) * nd, pipeline_mode=pl.Buffered(1))


def _rms_adaln(x, g, shift, scale):
    ms = jnp.mean(x * x, axis=-1, keepdims=True)
    return (x * lax.rsqrt(ms + EPS) * g) * (1.0 + scale) + shift


def _mod_kernel(c_ref, w_ref, b_ref, o_ref):
    c = c_ref[...]
    sc = c * jax.nn.sigmoid(c)
    o_ref[...] = jnp.dot(sc, w_ref[...], precision=lax.Precision.HIGHEST,
                         preferred_element_type=F32) + b_ref[...]


def _mod_call(c_rows, w_mod, b_mod):
    rows, d = c_rows.shape
    n = w_mod.shape[1]
    bn = d
    return pl.pallas_call(
        _mod_kernel,
        grid=(n // bn,),
        in_specs=[pl.BlockSpec((rows, d), lambda j: (0, 0)),
                  pl.BlockSpec((d, bn), lambda j: (0, j)),
                  pl.BlockSpec((1, bn), lambda j: (0, j))],
        out_specs=pl.BlockSpec((rows, bn), lambda j: (0, j)),
        out_shape=jax.ShapeDtypeStruct((rows, n), F32),
        name="mod",
        compiler_params=_cparams("arbitrary"),
    )(c_rows, w_mod, b_mod.reshape(1, n))


def _mixer_out(x, mod_ref, u_ref, yf_ref, yb_ref, a_ref, d_ref, wg_ref, bg_ref, wos_ref, woa_ref):
    y = d_ref[...] * u_ref[...] + yf_ref[...] + yb_ref[...]
    g = jax.nn.gelu(y)
    z = jnp.dot(g.astype(BF16), wg_ref[...], preferred_element_type=F32) + bg_ref[...]
    s = g * jax.nn.sigmoid(z)
    out = (jnp.dot(s.astype(BF16), wos_ref[...], preferred_element_type=F32)
           + jnp.dot(a_ref[...], woa_ref[...], preferred_element_type=F32))
    return x + mod_ref[5:6, :] * out


def _ffn_kernel(x_ref, mod_ref, *rest, mod_i, d_ff, fc, final, mixer):
    x = x_ref[...]
    if mixer:
        x = _mixer_out(x, mod_ref, *rest[:9])
        rest = rest[9:]
    g_ref, win_ref, wout_ref = rest[:3]
    rest = rest[3:]
    if final:
        fg_ref, o_ref, acc_ref = rest
    else:
        o_ref, acc_ref = rest
    shift = mod_ref[3 * mod_i:3 * mod_i + 1, :]
    scale = mod_ref[3 * mod_i + 1:3 * mod_i + 2, :]
    gate = mod_ref[3 * mod_i + 2:3 * mod_i + 3, :]
    h = _rms_adaln(x, g_ref[...], shift, scale).astype(BF16)
    for j in range(d_ff // fc):
        gt = jnp.dot(h, win_ref[:, j * fc:(j + 1) * fc], preferred_element_type=F32)
        up = jnp.dot(h, win_ref[:, d_ff + j * fc:d_ff + (j + 1) * fc], preferred_element_type=F32)
        act = (gt * jax.nn.sigmoid(gt) * up).astype(BF16)
        part = jnp.dot(act, wout_ref[j * fc:(j + 1) * fc, :], preferred_element_type=F32)
        if j == 0:
            acc_ref[...] = part
        else:
            acc_ref[...] += part
    y = x + (0.5 * gate) * acc_ref[...]
    if final:
        ms = jnp.mean(y * y, axis=-1, keepdims=True)
        y = y * lax.rsqrt(ms + EPS) * fg_ref[...]
    o_ref[...] = y


def _ffn_call(x, mod, g, w_in, w_out, *, mod_i, mod_row, tm, final_g=None, mixer=None):
    b, t, d = x.shape
    d_ff = w_out.shape[0]
    fc = MXU_DIM
    assert t % tm == 0 and d_ff % fc == 0
    final = final_g is not None
    in_specs = [pl.BlockSpec((None, tm, d), lambda bi, ti: (bi, ti, 0)),
                pl.BlockSpec((None, N_MOD, d), lambda bi, ti: (mod_row(bi), 0, 0))]
    args = [x, mod]
    if mixer is not None:
        u, yf, yb, a = mixer[:4]
        w = a.shape[2]
        tmajor = pl.BlockSpec((tm, w), lambda bi, ti: (ti, bi))
        in_specs += [tmajor, tmajor, tmajor, pl.BlockSpec((None, tm, w), lambda bi, ti: (bi, ti, 0))]
        in_specs += [_const_spec(p.shape) for p in mixer[4:]]
        args += list(mixer)
    in_specs += [_const_spec((1, d)), _const_spec(w_in.shape), _const_spec(w_out.shape)]
    args += [g, w_in, w_out]
    if final:
        in_specs.append(_const_spec((1, d)))
        args.append(final_g)
    return pl.pallas_call(
        functools.partial(_ffn_kernel, mod_i=mod_i, d_ff=d_ff, fc=fc, final=final,
                          mixer=mixer is not None),
        grid=(b, t // tm),
        in_specs=in_specs,
        out_specs=pl.BlockSpec((None, tm, d), lambda bi, ti: (bi, ti, 0)),
        out_shape=jax.ShapeDtypeStruct((b, t, d), F32),
        scratch_shapes=[pltpu.VMEM((tm, d), F32)],
        name="ffn_final" if final else "ffn",
        compiler_params=_cparams("parallel", "parallel"),
    )(*args)


def _rope(t, cos, sin_signed, first_half):
    outs = []
    for j in range(t.shape[1] // LANES):
        tj = t[:, j * LANES:(j + 1) * LANES]
        partner = jnp.where(first_half, pltpu.roll(tj, LANES - 16, 1), pltpu.roll(tj, 16, 1))
        outs.append(tj * cos + partner * sin_signed)
    return jnp.concatenate(outs, axis=1)


def _tile_max_norm2(xb, indicator, out_shape):
    xf = xb.astype(F32)
    norm2 = jnp.dot((xf * xf).astype(BF16), indicator, preferred_element_type=F32)
    return jnp.broadcast_to(jnp.max(norm2, axis=0, keepdims=True), out_shape)


def _inproj_kernel(x_ref, mod_ref, g_ref, w_ref, ind_ref, *rest, rope, width, q_scale):
    if rope:
        cos_ref, sin_ref, u_ref, q_ref, k_ref, v_ref, kn_ref, qn_ref = rest
    else:
        u_ref, k_ref, v_ref, kn_ref = rest
    h = _rms_adaln(x_ref[...], g_ref[...], mod_ref[3:4, :], mod_ref[4:5, :]).astype(BF16)
    u_ref[...] = jnp.dot(h, w_ref[:, 0:width], preferred_element_type=F32)
    k = jnp.dot(h, w_ref[:, 2 * width:3 * width], preferred_element_type=F32)
    if rope:
        q = jnp.dot(h, w_ref[:, width:2 * width], preferred_element_type=F32)
        cos = cos_ref[...]
        sin_signed = sin_ref[...]
        lane = lax.broadcasted_iota(jnp.int32, (1, LANES), 1)
        first_half = (lane % 32) < 16
        qb = (_rope(q, cos, sin_signed, first_half) * q_scale).astype(BF16)
        q_ref[...] = qb
        qn_ref[...] = _tile_max_norm2(qb, ind_ref[...], qn_ref.shape)
        k = _rope(k, cos, sin_signed, first_half)
    kb = k.astype(BF16)
    k_ref[...] = kb
    kn_ref[...] = _tile_max_norm2(kb, ind_ref[...], kn_ref.shape)
    v = jnp.dot(h, w_ref[:, 3 * width:4 * width], preferred_element_type=F32).astype(BF16)
    ones = jnp.ones((v.shape[0], LANES), BF16)
    for hd in range(width // LANES):
        v_ref[:, 2 * hd * LANES:(2 * hd + 1) * LANES] = v[:, hd * LANES:(hd + 1) * LANES]
        v_ref[:, (2 * hd + 1) * LANES:(2 * hd + 2) * LANES] = ones


def _inproj_call(x, mod, g, w, *, mod_row, tm, head_dim, rope_tabs=None, q_scale=1.0):
    b, t, d = x.shape
    width = w.shape[1] // 4
    assert t % tm == 0
    rope = rope_tabs is not None
    tok = lambda bi, ti: (bi, ti, 0)
    segment = jnp.arange(width, dtype=jnp.int32)[:, None] // head_dim
    indicator = (segment == jnp.arange(LANES, dtype=jnp.int32)[None, :]).astype(BF16)
    in_specs = [pl.BlockSpec((None, tm, d), tok),
                pl.BlockSpec((None, N_MOD, d), lambda bi, ti: (mod_row(bi), 0, 0)),
                _const_spec((1, d)), _const_spec(w.shape), _const_spec(indicator.shape)]
    args = [x, mod, g, w, indicator]
    out_specs = [pl.BlockSpec((tm, width), lambda bi, ti: (ti, bi))]
    out_shape = [jax.ShapeDtypeStruct((t, b * width), F32)]
    n_qk = 1
    if rope:
        in_specs += [pl.BlockSpec((tm, LANES), lambda bi, ti: (ti, 0))] * 2
        args += list(rope_tabs)
        n_qk = 2
    out_specs += [pl.BlockSpec((None, tm, width), tok)] * n_qk
    out_shape += [jax.ShapeDtypeStruct((b, t, width), BF16)] * n_qk
    out_specs.append(pl.BlockSpec((None, tm, 2 * width), tok))
    out_shape.append(jax.ShapeDtypeStruct((b, t, 2 * width), BF16))
    for _ in range(n_qk):
        out_specs.append(pl.BlockSpec((None, None, SUBLANES, LANES), lambda bi, ti: (bi, ti, 0, 0)))
        out_shape.append(jax.ShapeDtypeStruct((b, t // tm, SUBLANES, LANES), F32))
    return pl.pallas_call(
        functools.partial(_inproj_kernel, rope=rope, width=width, q_scale=q_scale),
        grid=(b, t // tm),
        in_specs=in_specs, out_specs=out_specs, out_shape=out_shape,
        name="inproj_rope" if rope else "inproj",
        compiler_params=_cparams("parallel", "parallel"),
    )(*args)


def _rope_tables(length, head_dim):
    n_freq = head_dim // 4
    pos = jnp.arange(length, dtype=jnp.int32)
    row = (pos // GRID_W).astype(F32)
    col = (pos % GRID_W).astype(F32)
    inv_freq = ROPE_BASE ** (-jnp.arange(n_freq, dtype=F32) / n_freq)
    ang = jnp.stack([row[:, None] * inv_freq, col[:, None] * inv_freq], axis=1)
    cos = jnp.broadcast_to(jnp.cos(ang)[:, :, None, :], (length, 2, 2, n_freq))
    sign = jnp.array([-1.0, 1.0], F32)[None, None, :, None]
    sin = jnp.sin(ang)[:, :, None, :] * sign
    reps = LANES // head_dim
    cos = jnp.tile(cos.reshape(length, head_dim), (1, reps))
    sin = jnp.tile(jnp.broadcast_to(sin, (length, 2, 2, n_freq)).reshape(length, head_dim), (1, reps))
    return cos, sin


def _s5_prep_kernel(are_ref, aim_ref, ldt_ref, bre_ref, bim_ref, tab_ref, obre_ref, obim_ref):
    n_dir = are_ref.shape[0]
    n_state = are_ref.shape[1]
    for d in range(n_dir):
        a_re = are_ref[d:d + 1, :]
        a_im = aim_ref[d:d + 1, :]
        dt = jnp.exp(ldt_ref[d:d + 1, :])
        mag = jnp.exp(dt * a_re)
        abar_re = mag * jnp.cos(dt * a_im)
        abar_im = mag * jnp.sin(dt * a_im)
        zr = abar_re - 1.0
        zi = abar_im
        den = a_re * a_re + a_im * a_im
        coef_re = (zr * a_re + zi * a_im) / den
        coef_im = (zi * a_re - zr * a_im) / den
        b_re = bre_ref[d]
        b_im = bim_ref[d]
        obre_ref[d] = coef_re * b_re - coef_im * b_im
        obim_ref[d] = coef_re * b_im + coef_im * b_re
        tab_ref[d, 0] = jnp.broadcast_to(abar_re, (SUBLANES, n_state))
        tab_ref[d, 1] = jnp.broadcast_to(abar_im, (SUBLANES, n_state))


def _s5_prep_call(a_re, a_im, log_dt, b_re_t, b_im_t):
    n_dir, n_state = a_re.shape
    h = b_re_t.shape[1]
    return pl.pallas_call(
        _s5_prep_kernel,
        out_shape=[jax.ShapeDtypeStruct((n_dir, 2, SUBLANES, n_state), F32),
                   jax.ShapeDtypeStruct((n_dir, h, n_state), F32),
                   jax.ShapeDtypeStruct((n_dir, h, n_state), F32)],
        name="s5_prep",
    )(a_re, a_im, log_dt, b_re_t, b_im_t)


S5_CHUNKS_PER_PASS = 8


def _s5_kernel(uc_ref, ul_ref, bre_ref, bim_ref, cre_ref, cim_ref, tab_ref, y_ref,
               bu0re, bu0im, bu1re, bu1im, h0re, h0im, h1re, h1im, carry, *, nc, tt, reverse):
    s = pl.program_id(0)
    n_in = bre_ref.shape[0]
    n_state = bre_ref.shape[1]
    nb = uc_ref.shape[1]
    rows_t = tt * nb
    kblocks = n_in // MXU_DIM
    sw = n_state // kblocks

    @pl.when(s == 0)
    def _():
        carry[...] = jnp.zeros_like(carry)
        for buf in (bu1re, bu1im, h0re, h0im):
            buf[...] = jnp.zeros_like(buf)

    def stage_bu(dst_re, dst_im):
        u = jnp.where(s < nc, uc_ref[...], ul_ref[...]).reshape(rows_t, n_in).astype(BF16)
        for kb in range(kblocks):
            ub = u[:, kb * MXU_DIM:(kb + 1) * MXU_DIM]
            rows = slice(kb * MXU_DIM, (kb + 1) * MXU_DIM)
            cols = slice(kb * sw, (kb + 1) * sw)
            dst_re[:, cols] = jnp.dot(ub, bre_ref[rows, cols], preferred_element_type=F32)
            dst_im[:, cols] = jnp.dot(ub, bim_ref[rows, cols], preferred_element_type=F32)

    def stage_scan(src_re, src_im, dst_re, dst_im):
        ncp = S5_CHUNKS_PER_PASS
        for lc0 in range(0, n_state // LANES, ncp):
            cols = [slice((lc0 + c) * LANES, (lc0 + c + 1) * LANES) for c in range(ncp)]
            a_re = [tab_ref[0, :, cs] for cs in cols]
            a_im = [tab_ref[1, :, cs] for cs in cols]
            state = [(carry[0, :, cs], carry[1, :, cs]) for cs in cols]
            for t0 in range(0, tt, 2):
                toks = (tt - 1 - t0, tt - 2 - t0) if reverse else (t0, t0 + 1)
                pair = slice(min(toks) * nb, (min(toks) + 2) * nb)
                for c, cs in enumerate(cols):
                    hr, hm = state[c]
                    out = {}
                    for tok in toks:
                        rows = slice(tok * nb, (tok + 1) * nb)
                        hr, hm = (a_re[c] * hr - a_im[c] * hm + src_re[rows, cs],
                                  a_re[c] * hm + a_im[c] * hr + src_im[rows, cs])
                        out[tok] = (hr, hm)
                    lo, hi = out[min(toks)], out[max(toks)]
                    dst_re[pair, cs] = jnp.concatenate([lo[0], hi[0]], axis=0).astype(BF16)
                    dst_im[pair, cs] = jnp.concatenate([lo[1], hi[1]], axis=0).astype(BF16)
                    state[c] = (hr, hm)
            for c, cs in enumerate(cols):
                carry[0, :, cs] = state[c][0]
                carry[1, :, cs] = state[c][1]

    def stage_readout(src_re, src_im):
        n_out = cre_ref.shape[1]
        nblocks = n_out // MXU_DIM
        rw = n_state // nblocks
        for ob in range(nblocks):
            rows = slice(ob * rw, (ob + 1) * rw)
            cols = slice(ob * MXU_DIM, (ob + 1) * MXU_DIM)
            y = (jnp.dot(src_re[:, rows], cre_ref[rows, cols], preferred_element_type=F32)
                 - jnp.dot(src_im[:, rows], cim_ref[rows, cols], preferred_element_type=F32))
            y_ref[:, :, cols] = y.reshape(tt, nb, MXU_DIM)

    @pl.when(s % 2 == 0)
    def _():
        stage_bu(bu0re, bu0im)
        stage_scan(bu1re, bu1im, h1re, h1im)
        stage_readout(h0re, h0im)

    @pl.when(s % 2 == 1)
    def _():
        stage_bu(bu1re, bu1im)
        stage_scan(bu0re, bu0im, h0re, h0im)
        stage_readout(h1re, h1im)


def _s5_call(u_ctx, u_lat, b_re, b_im, c_re, c_im, tab, *, reverse, tt):
    lc, b, w = u_ctx.shape
    l = u_lat.shape[0]
    n_state = b_re.shape[1]
    assert b == SUBLANES, "the scan keeps one batch row per sublane"
    assert lc % tt == 0 and l % tt == 0 and w % MXU_DIM == 0
    assert n_state % (LANES * S5_CHUNKS_PER_PASS) == 0
    nc, nl = lc // tt, l // tt

    def ctx_tile(s):
        j = jnp.minimum(s, nc - 1)
        return nc - 1 - j if reverse else j

    def lat_tile(s):
        j = jnp.clip(s - nc, 0, nl - 1)
        return nl - 1 - j if reverse else j

    bu_tile = pltpu.VMEM((tt * b, n_state), F32)
    h_tile = pltpu.VMEM((tt * b, n_state), BF16)
    return pl.pallas_call(
        functools.partial(_s5_kernel, nc=nc, tt=tt, reverse=reverse),
        grid=(nc + nl + 2,),
        in_specs=[pl.BlockSpec((tt, b, w), lambda s: (ctx_tile(s), 0, 0)),
                  pl.BlockSpec((tt, b, w), lambda s: (lat_tile(s), 0, 0)),
                  _const_spec(b_re.shape), _const_spec(b_im.shape),
                  _const_spec(c_re.shape), _const_spec(c_im.shape),
                  _const_spec(tab.shape)],
        out_specs=pl.BlockSpec((tt, b, w), lambda s: (lat_tile(s - 2), 0, 0)),
        out_shape=jax.ShapeDtypeStruct((l, b, w), F32),
        scratch_shapes=[bu_tile] * 4 + [h_tile] * 4 + [pltpu.VMEM((2, SUBLANES, n_state), F32)],
        name="s5_bwd" if reverse else "s5_fwd",
        compiler_params=_cparams("arbitrary"),
    )(u_ctx, u_lat, b_re, b_im, c_re, c_im, tab)


_NT = (((1,), (1,)), ((), ()))
ATTN_BOUND_LIMIT = 48.0
ATTN_BOUND_MARGIN = 1.02


def _attn_kernel(bound_ref, q_ref, kl_ref, vl_ref, kc_ref, vc_ref, lq_ref, lk_ref, sg_ref, o_ref,
                 qs, m_s, acc, *, tq, tk, rc, lam_init):
    hw = q_ref.shape[1]
    half = hw // 2
    q = q_ref[...]
    lane = lax.broadcasted_iota(jnp.int32, q.shape, 1)
    zero = jnp.zeros_like(q)
    qs[0:tq, :] = jnp.where(lane < half, q, zero)
    qs[tq:2 * tq, :] = jnp.where(lane >= half, q, zero)
    chunks = [slice(c * rc, (c + 1) * rc) for c in range(2 * tq // rc)]
    blocks = [(kc_ref, vc_ref, 0, kc_ref.shape[0])]
    blocks += [(kl_ref, vl_ref, j * tk, tk) for j in range(kl_ref.shape[0] // tk)]

    base = ((pl.program_id(0) * pl.num_programs(1) + pl.program_id(1)) * pl.num_programs(2)
            + pl.program_id(2)) * 2
    bounds = [bound_ref[base], bound_ref[base + 1]]
    bound_ok = jnp.maximum(bounds[0], bounds[1]) <= ATTN_BOUND_LIMIT

    @pl.when(bound_ok)
    def _():
        for rows in chunks:
            m = bounds[rows.start // tq]
            total = None
            for k_ref, v_ref, off, width in blocks:
                s = lax.dot_general(qs[rows, :], k_ref[off:off + width, :], _NT,
                                    preferred_element_type=F32)
                p = jnp.exp2(s - m).astype(BF16)
                pv = jnp.dot(p, v_ref[off:off + width, :], preferred_element_type=F32)
                total = pv if total is None else total + pv
            acc[rows, :] = total

    @pl.when(jnp.logical_not(bound_ok))
    def _():
        for rows in chunks:
            s = lax.dot_general(qs[rows, :], kc_ref[...], _NT, preferred_element_type=F32)
            m = jnp.max(s, axis=-1, keepdims=True)
            p = jnp.exp2((s - m).astype(BF16))
            acc[rows, :] = jnp.dot(p, vc_ref[...], preferred_element_type=F32)
            m_s[rows, :] = m

        def kv_block(j, carry):
            off = pl.multiple_of(j * tk, tk)
            for rows in chunks:
                s = lax.dot_general(qs[rows, :], kl_ref[pl.ds(off, tk), :], _NT,
                                    preferred_element_type=F32)
                m_prev = m_s[rows, :]
                m_new = jnp.maximum(m_prev, jnp.max(s, axis=-1, keepdims=True))
                alpha = jnp.exp2(m_prev - m_new)
                p = jnp.exp2((s - m_new).astype(BF16))
                acc[rows, :] = alpha * acc[rows, :] + jnp.dot(
                    p, vl_ref[pl.ds(off, tk), :], preferred_element_type=F32)
                m_s[rows, :] = m_new
            return carry

        lax.fori_loop(0, kl_ref.shape[0] // tk, kv_block, 0)

    lq = lq_ref[...]
    lk = lk_ref[...]
    lam = (jnp.exp(jnp.sum(lq[0:1, :] * lk[0:1, :], axis=-1, keepdims=True))
           - jnp.exp(jnp.sum(lq[1:2, :] * lk[1:2, :], axis=-1, keepdims=True)) + lam_init)
    o = (acc[0:tq, 0:hw] / acc[0:tq, hw:2 * hw]
         - lam * (acc[tq:2 * tq, 0:hw] / acc[tq:2 * tq, hw:2 * hw]))
    ms = jnp.mean(o * o, axis=-1, keepdims=True)
    o_ref[...] = ((o * lax.rsqrt(ms + EPS) * sg_ref[...]) * (1.0 - lam_init)).astype(o_ref.dtype)


def _attn_call(bounds, q, k_lat, v_lat, k_ctx, v_ctx, lam_q, lam_k, subln_g, *,
               lam_init, tq, tk, rc):
    b, l, w = q.shape
    lc = k_ctx.shape[1]
    hw = w // DIFF_HEADS
    assert l % tq == 0 and l % tk == 0 and tq % rc == 0 and hw == LANES
    assert bounds.shape == (b * DIFF_HEADS * (l // tq) * 2,)
    qmap = lambda bi, hi, qi, bnd: (bi, qi, hi)
    kvmap = lambda bi, hi, qi, bnd: (bi, 0, hi)
    const = lambda shape: pl.BlockSpec(shape, lambda bi, hi, qi, bnd: (0,) * len(shape),
                                       pipeline_mode=pl.Buffered(1))
    return pl.pallas_call(
        functools.partial(_attn_kernel, tq=tq, tk=tk, rc=rc, lam_init=lam_init),
        grid_spec=pltpu.PrefetchScalarGridSpec(
            num_scalar_prefetch=1,
            grid=(b, DIFF_HEADS, l // tq),
            in_specs=[pl.BlockSpec((None, tq, hw), qmap),
                      pl.BlockSpec((None, l, hw), kvmap), pl.BlockSpec((None, l, 2 * hw), kvmap),
                      pl.BlockSpec((None, lc, hw), kvmap), pl.BlockSpec((None, lc, 2 * hw), kvmap),
                      const(lam_q.shape), const(lam_k.shape), const(subln_g.shape)],
            out_specs=pl.BlockSpec((None, tq, hw), qmap),
            scratch_shapes=[pltpu.VMEM((2 * tq, hw), BF16), pltpu.VMEM((2 * tq, 1), F32),
                            pltpu.VMEM((2 * tq, 2 * hw), F32)]),
        out_shape=jax.ShapeDtypeStruct((b, l, w), BF16),
        name="diff_attn",
        compiler_params=_cparams("parallel", "parallel", "parallel"),
    )(bounds, q, k_lat, v_lat, k_ctx, v_ctx, lam_q, lam_k, subln_g)


def _block_diag(blocks):
    g, r, c = blocks.shape
    eye = jnp.eye(g, dtype=blocks.dtype)
    return (eye[:, None, :, None] * blocks[:, :, None, :]).reshape(g * r, g * c)


def kernel(x, c, ctx, c_ctx, w_mod, b_mod, norm_g, ffn_w_in, ffn_w_out, w_in, w_out, ssm_a_re, ssm_a_im, ssm_log_dt, ssm_b_re, ssm_b_im, ssm_c_re, ssm_c_im, ssm_d, w_glu, b_glu, lam_q, lam_k, subln_g, final_g):
    b, l, d = x.shape
    lc = ctx.shape[1]
    depth = w_mod.shape[0]
    assert depth == 1, "context-stream update between layers is not implemented"
    n_groups, n_state_g, n_in_g = ssm_b_re.shape[2:]
    ssm_w = n_groups * n_in_g
    n_state = n_groups * n_state_g
    head_dim = lam_q.shape[-1]
    ctx_row = b
    rows = SUBLANES * (-(-(b + 1) // SUBLANES))

    tm_lat = 512 if l % 512 == 0 else 256
    tm_ctx = 256
    tt_s5 = 64
    tq = 512 if l % 512 == 0 else 256
    tk = 1024 if l % 1024 == 0 else 256
    rc = 256

    layer = 0
    lam_init = 0.8 - 0.6 * math.exp(-0.3 * layer)

    c_rows = jnp.zeros((rows, d), F32).at[:b].set(c).at[b].set(c_ctx)
    mod = _mod_call(c_rows, w_mod[layer], b_mod[layer]).reshape(rows, N_MOD, d)
    lat_row = lambda bi: bi
    ctx_mod_row = lambda bi: ctx_row

    g0, g1, g2 = (norm_g[layer, i].reshape(1, d) for i in range(3))
    w1_in, w1_out = ffn_w_in[layer, 0].astype(BF16), ffn_w_out[layer, 0].astype(BF16)
    w2_in, w2_out = ffn_w_in[layer, 1].astype(BF16), ffn_w_out[layer, 1].astype(BF16)
    w_mix = w_in[layer].astype(BF16)

    x1 = _ffn_call(x, mod, g0, w1_in, w1_out, mod_i=0, mod_row=lat_row, tm=tm_lat)
    ctx1 = _ffn_call(ctx, mod, g0, w1_in, w1_out, mod_i=0, mod_row=ctx_mod_row, tm=tm_ctx)

    tabs = _rope_tables(l, head_dim)
    u_lat, q_lat, k_lat, v_lat, kn_lat, qn_lat = _inproj_call(
        x1, mod, g1, w_mix, mod_row=lat_row, tm=tm_lat, head_dim=head_dim, rope_tabs=tabs,
        q_scale=head_dim ** -0.5 * math.log2(math.e))
    u_ctx, k_ctx, v_ctx, kn_ctx = _inproj_call(ctx1, mod, g1, w_mix, mod_row=ctx_mod_row,
                                               tm=tm_ctx, head_dim=head_dim)

    flat = lambda p: p[layer].reshape(2, n_state)
    log_dt = jnp.repeat(ssm_log_dt[layer], n_state_g, axis=-1)
    b_t = lambda p: p[layer].transpose(0, 3, 1, 2).reshape(2, n_in_g, n_state)
    tab, bbar_re, bbar_im = _s5_prep_call(flat(ssm_a_re), flat(ssm_a_im), log_dt,
                                          b_t(ssm_b_re), b_t(ssm_b_im))
    ys = []
    for direction in range(2):
        to_blocks = lambda m: _block_diag(
            m[direction].reshape(n_in_g, n_groups, n_state_g).transpose(1, 0, 2)).astype(BF16)
        c_blocks = lambda p: _block_diag(p[layer, direction].transpose(0, 2, 1)).astype(BF16)
        y_dir = _s5_call(u_ctx.reshape(lc, b, ssm_w), u_lat.reshape(l, b, ssm_w),
                         to_blocks(bbar_re), to_blocks(bbar_im),
                         c_blocks(ssm_c_re), c_blocks(ssm_c_im), tab[direction],
                         reverse=direction == 1, tt=tt_s5)
        ys.append(y_dir.reshape(l, b * ssm_w))

    n_hc = 2 * DIFF_HEADS
    kn2 = jnp.maximum(kn_lat.max(axis=(1, 2)), kn_ctx.max(axis=(1, 2)))[:, None, :n_hc]
    qn2 = qn_lat[:, :, 0, :n_hc]
    if tq >= tm_lat:
        qn2 = qn2.reshape(b, l // tq, tq // tm_lat, n_hc).max(axis=2)
    else:
        qn2 = jnp.repeat(qn2, tm_lat // tq, axis=1)
    bounds = jnp.sqrt(qn2 * kn2) * ATTN_BOUND_MARGIN
    bounds = bounds.reshape(b, l // tq, DIFF_HEADS, 2).transpose(0, 2, 1, 3).reshape(-1)
    a_lat = _attn_call(bounds, q_lat, k_lat, v_lat, k_ctx, v_ctx, lam_q[layer], lam_k[layer],
                       subln_g[layer].reshape(1, -1), lam_init=lam_init, tq=tq, tk=tk, rc=rc)

    w_o = w_out[layer].astype(BF16)
    mixer = (u_lat, ys[0], ys[1], a_lat, ssm_d[layer].reshape(1, ssm_w),
             w_glu[layer].astype(BF16), b_glu[layer].reshape(1, ssm_w), w_o[:ssm_w], w_o[ssm_w:])
    return _ffn_call(x1, mod, g2, w2_in, w2_out, mod_i=2, mod_row=lat_row, tm=tm_lat,
                     final_g=final_g.reshape(1, d), mixer=mixer)
```

```python
import functools
import math

import jax
import jax.numpy as jnp
from jax import lax
from jax.experimental import pallas as pl
from jax.experimental.pallas import tpu as pltpu

F32 = jnp.float32
BF16 = jnp.bfloat16

EPS = 1e-6
N_MOD = 9
GRID_W = 64
ROPE_BASE = 10000.0
DIFF_HEADS = 4
SUBLANES = 8
LANES = 128
MXU_DIM = 256
VMEM_LIMIT = 56 * 1024 * 1024


def _cparams(*sem):
    return pltpu.CompilerParams(dimension_semantics=sem, vmem_limit_bytes=VMEM_LIMIT)


def _const_spec(shape):
    nd = len(shape)
    return pl.BlockSpec(shape, lambda *_: (0,) * nd, pipeline_mode=pl.Buffered(1))


def _rms_adaln(x, g, shift, scale):
    ms = jnp.mean(x * x, axis=-1, keepdims=True)
    return (x * lax.rsqrt(ms + EPS) * g) * (1.0 + scale) + shift


def _mod_kernel(c_ref, w_ref, b_ref, o_ref):
    c = c_ref[...]
    sc = c * jax.nn.sigmoid(c)
    o_ref[...] = jnp.dot(sc, w_ref[...], precision=lax.Precision.HIGHEST,
                         preferred_element_type=F32) + b_ref[...]


def _mod_call(c_rows, w_mod, b_mod):
    rows, d = c_rows.shape
    n = w_mod.shape[1]
    bn = d
    return pl.pallas_call(
        _mod_kernel,
        grid=(n // bn,),
        in_specs=[pl.BlockSpec((rows, d), lambda j: (0, 0)),
                  pl.BlockSpec((d, bn), lambda j: (0, j)),
                  pl.BlockSpec((1, bn), lambda j: (0, j))],
        out_specs=pl.BlockSpec((rows, bn), lambda j: (0, j)),
        out_shape=jax.ShapeDtypeStruct((rows, n), F32),
        name="mod",
        compiler_params=_cparams("arbitrary"),
    )(c_rows, w_mod, b_mod.reshape(1, n))


def _mixer_out(x, mod_ref, u_ref, yf_ref, yb_ref, a_ref, d_ref, wg_ref, bg_ref, wos_ref, woa_ref):
    y = d_ref[...] * u_ref[...] + yf_ref[...] + yb_ref[...]
    g = jax.nn.gelu(y)
    z = jnp.dot(g.astype(BF16), wg_ref[...], preferred_element_type=F32) + bg_ref[...]
    s = g * jax.nn.sigmoid(z)
    out = (jnp.dot(s.astype(BF16), wos_ref[...], preferred_element_type=F32)
           + jnp.dot(a_ref[...], woa_ref[...], preferred_element_type=F32))
    return x + mod_ref[5:6, :] * out


def _ffn_kernel(x_ref, mod_ref, *rest, mod_i, d_ff, fc, final, mixer):
    x = x_ref[...]
    if mixer:
        x = _mixer_out(x, mod_ref, *rest[:9])
        rest = rest[9:]
    g_ref, win_ref, wout_ref = rest[:3]
    rest = rest[3:]
    if final:
        fg_ref, o_ref, acc_ref = rest
    else:
        o_ref, acc_ref = rest
    shift = mod_ref[3 * mod_i:3 * mod_i + 1, :]
    scale = mod_ref[3 * mod_i + 1:3 * mod_i + 2, :]
    gate = mod_ref[3 * mod_i + 2:3 * mod_i + 3, :]
    h = _rms_adaln(x, g_ref[...], shift, scale).astype(BF16)
    for j in range(d_ff // fc):
        gt = jnp.dot(h, win_ref[:, j * fc:(j + 1) * fc], preferred_element_type=F32)
        up = jnp.dot(h, win_ref[:, d_ff + j * fc:d_ff + (j + 1) * fc], preferred_element_type=F32)
        act = (gt * jax.nn.sigmoid(gt) * up).astype(BF16)
        part = jnp.dot(act, wout_ref[j * fc:(j + 1) * fc, :], preferred_element_type=F32)
        if j == 0:
            acc_ref[...] = part
        else:
            acc_ref[...] += part
    y = x + (0.5 * gate) * acc_ref[...]
    if final:
        ms = jnp.mean(y * y, axis=-1, keepdims=True)
        y = y * lax.rsqrt(ms + EPS) * fg_ref[...]
    o_ref[...] = y


def _ffn_call(x, mod, g, w_in, w_out, *, mod_i, mod_row, tm, final_g=None, mixer=None):
    b, t, d = x.shape
    d_ff = w_out.shape[0]
    fc = MXU_DIM
    assert t % tm == 0 and d_ff % fc == 0
    final = final_g is not None
    in_specs = [pl.BlockSpec((None, tm, d), lambda bi, ti: (bi, ti, 0)),
                pl.BlockSpec((None, N_MOD, d), lambda bi, ti: (mod_row(bi), 0, 0))]
    args = [x, mod]
    if mixer is not None:
        u, yf, yb, a = mixer[:4]
        w = a.shape[2]
        tmajor = pl.BlockSpec((tm, w), lambda bi, ti: (ti, bi))
        in_specs += [tmajor, tmajor, tmajor, pl.BlockSpec((None, tm, w), lambda bi, ti: (bi, ti, 0))]
        in_specs += [_const_spec(p.shape) for p in mixer[4:]]
        args += list(mixer)
    in_specs += [_const_spec((1, d)), _const_spec(w_in.shape), _const_spec(w_out.shape)]
    args += [g, w_in, w_out]
    if final:
        in_specs.append(_const_spec((1, d)))
        args.append(final_g)
    return pl.pallas_call(
        functools.partial(_ffn_kernel, mod_i=mod_i, d_ff=d_ff, fc=fc, final=final,
                          mixer=mixer is not None),
        grid=(b, t // tm),
        in_specs=in_specs,
        out_specs=pl.BlockSpec((None, tm, d), lambda bi, ti: (bi, ti, 0)),
        out_shape=jax.ShapeDtypeStruct((b, t, d), F32),
        scratch_shapes=[pltpu.VMEM((tm, d), F32)],
        name="ffn_final" if final else "ffn",
        compiler_params=_cparams("parallel", "parallel"),
    )(*args)


def _rope(t, cos, sin_signed, first_half):
    outs = []
    for j in range(t.shape[1] // LANES):
        tj = t[:, j * LANES:(j + 1) * LANES]
        partner = jnp.where(first_half, pltpu.roll(tj, LANES - 16, 1), pltpu.roll(tj, 16, 1))
        outs.append(tj * cos + partner * sin_signed)
    return jnp.concatenate(outs, axis=1)


def _tile_max_norm2(xb, indicator, out_shape):
    xf = xb.astype(F32)
    norm2 = jnp.dot((xf * xf).astype(BF16), indicator, preferred_element_type=F32)
    return jnp.broadcast_to(jnp.max(norm2, axis=0, keepdims=True), out_shape)


def _inproj_kernel(x_ref, mod_ref, g_ref, w_ref, ind_ref, *rest, rope, width, q_scale):
    if rope:
        cos_ref, sin_ref, u_ref, q_ref, k_ref, v_ref, kn_ref, qn_ref = rest
    else:
        u_ref, k_ref, v_ref, kn_ref = rest
    h = _rms_adaln(x_ref[...], g_ref[...], mod_ref[3:4, :], mod_ref[4:5, :]).astype(BF16)
    u_ref[...] = jnp.dot(h, w_ref[:, 0:width], preferred_element_type=F32)
    k = jnp.dot(h, w_ref[:, 2 * width:3 * width], preferred_element_type=F32)
    if rope:
        q = jnp.dot(h, w_ref[:, width:2 * width], preferred_element_type=F32)
        cos = cos_ref[...]
        sin_signed = sin_ref[...]
        lane = lax.broadcasted_iota(jnp.int32, (1, LANES), 1)
        first_half = (lane % 32) < 16
        qb = (_rope(q, cos, sin_signed, first_half) * q_scale).astype(BF16)
        q_ref[...] = qb
        qn_ref[...] = _tile_max_norm2(qb, ind_ref[...], qn_ref.shape)
        k = _rope(k, cos, sin_signed, first_half)
    kb = k.astype(BF16)
    k_ref[...] = kb
    kn_ref[...] = _tile_max_norm2(kb, ind_ref[...], kn_ref.shape)
    v = jnp.dot(h, w_ref[:, 3 * width:4 * width], preferred_element_type=F32).astype(BF16)
    ones = jnp.ones((v.shape[0], LANES), BF16)
    for hd in range(width // LANES):
        v_ref[:, 2 * hd * LANES:(2 * hd + 1) * LANES] = v[:, hd * LANES:(hd + 1) * LANES]
        v_ref[:, (2 * hd + 1) * LANES:(2 * hd + 2) * LANES] = ones


def _inproj_call(x, mod, g, w, *, mod_row, tm, head_dim, rope_tabs=None, q_scale=1.0):
    b, t, d = x.shape
    width = w.shape[1] // 4
    assert t % tm == 0
    rope = rope_tabs is not None
    tok = lambda bi, ti: (bi, ti, 0)
    segment = jnp.arange(width, dtype=jnp.int32)[:, None] // head_dim
    indicator = (segment == jnp.arange(LANES, dtype=jnp.int32)[None, :]).astype(BF16)
    in_specs = [pl.BlockSpec((None, tm, d), tok),
                pl.BlockSpec((None, N_MOD, d), lambda bi, ti: (mod_row(bi), 0, 0)),
                _const_spec((1, d)), _const_spec(w.shape), _const_spec(indicator.shape)]
    args = [x, mod, g, w, indicator]
    out_specs = [pl.BlockSpec((tm, width), lambda bi, ti: (ti, bi))]
    out_shape = [jax.ShapeDtypeStruct((t, b * width), F32)]
    n_qk = 1
    if rope:
        in_specs += [pl.BlockSpec((tm, LANES), lambda bi, ti: (ti, 0))] * 2
        args += list(rope_tabs)
        n_qk = 2
    out_specs += [pl.BlockSpec((None, tm, width), tok)] * n_qk
    out_shape += [jax.ShapeDtypeStruct((b, t, width), BF16)] * n_qk
    out_specs.append(pl.BlockSpec((None, tm, 2 * width), tok))
    out_shape.append(jax.ShapeDtypeStruct((b, t, 2 * width), BF16))
    for _ in range(n_qk):
        out_specs.append(pl.BlockSpec((None, None, SUBLANES, LANES), lambda bi, ti: (bi, ti, 0, 0)))
        out_shape.append(jax.ShapeDtypeStruct((b, t // tm, SUBLANES, LANES), F32))
    return pl.pallas_call(
        functools.partial(_inproj_kernel, rope=rope, width=width, q_scale=q_scale),
        grid=(b, t // tm),
        in_specs=in_specs, out_specs=out_specs, out_shape=out_shape,
        name="inproj_rope" if rope else "inproj",
        compiler_params=_cparams("parallel", "parallel"),
    )(*args)


def _rope_tables(length, head_dim):
    n_freq = head_dim // 4
    pos = jnp.arange(length, dtype=jnp.int32)
    row = (pos // GRID_W).astype(F32)
    col = (pos % GRID_W).astype(F32)
    inv_freq = ROPE_BASE ** (-jnp.arange(n_freq, dtype=F32) / n_freq)
    ang = jnp.stack([row[:, None] * inv_freq, col[:, None] * inv_freq], axis=1)
    cos = jnp.broadcast_to(jnp.cos(ang)[:, :, None, :], (length, 2, 2, n_freq))
    sign = jnp.array([-1.0, 1.0], F32)[None, None, :, None]
    sin = jnp.sin(ang)[:, :, None, :] * sign
    reps = LANES // head_dim
    cos = jnp.tile(cos.reshape(length, head_dim), (1, reps))
    sin = jnp.tile(jnp.broadcast_to(sin, (length, 2, 2, n_freq)).reshape(length, head_dim), (1, reps))
    return cos, sin


def _s5_prep_kernel(are_ref, aim_ref, ldt_ref, bre_ref, bim_ref, tab_ref, obre_ref, obim_ref):
    n_dir = are_ref.shape[0]
    n_state = are_ref.shape[1]
    for d in range(n_dir):
        a_re = are_ref[d:d + 1, :]
        a_im = aim_ref[d:d + 1, :]
        dt = jnp.exp(ldt_ref[d:d + 1, :])
        mag = jnp.exp(dt * a_re)
        abar_re = mag * jnp.cos(dt * a_im)
        abar_im = mag * jnp.sin(dt * a_im)
        zr = abar_re - 1.0
        zi = abar_im
        den = a_re * a_re + a_im * a_im
        coef_re = (zr * a_re + zi * a_im) / den
        coef_im = (zi * a_re - zr * a_im) / den
        b_re = bre_ref[d]
        b_im = bim_ref[d]
        obre_ref[d] = coef_re * b_re - coef_im * b_im
        obim_ref[d] = coef_re * b_im + coef_im * b_re
        tab_ref[d, 0] = jnp.broadcast_to(abar_re, (SUBLANES, n_state))
        tab_ref[d, 1] = jnp.broadcast_to(abar_im, (SUBLANES, n_state))


def _s5_prep_call(a_re, a_im, log_dt, b_re_t, b_im_t):
    n_dir, n_state = a_re.shape
    h = b_re_t.shape[1]
    return pl.pallas_call(
        _s5_prep_kernel,
        out_shape=[jax.ShapeDtypeStruct((n_dir, 2, SUBLANES, n_state), F32),
                   jax.ShapeDtypeStruct((n_dir, h, n_state), F32),
                   jax.ShapeDtypeStruct((n_dir, h, n_state), F32)],
        name="s5_prep",
    )(a_re, a_im, log_dt, b_re_t, b_im_t)


S5_CHUNKS_PER_PASS = 8


def _s5_kernel(uc_ref, ul_ref, bre_ref, bim_ref, cre_ref, cim_ref, tab_ref, y_ref,
               bu0re, bu0im, bu1re, bu1im, h0re, h0im, h1re, h1im, carry, *, nc, tt, reverse):
    s = pl.program_id(0)
    n_in = bre_ref.shape[0]
    n_state = bre_ref.shape[1]
    nb = uc_ref.shape[1]
    rows_t = tt * nb
    kblocks = n_in // MXU_DIM
    sw = n_state // kblocks

    @pl.when(s == 0)
    def _():
        carry[...] = jnp.zeros_like(carry)
        for buf in (bu1re, bu1im, h0re, h0im):
            buf[...] = jnp.zeros_like(buf)

    def stage_bu(dst_re, dst_im):
        u = jnp.where(s < nc, uc_ref[...], ul_ref[...]).reshape(rows_t, n_in).astype(BF16)
        for kb in range(kblocks):
            ub = u[:, kb * MXU_DIM:(kb + 1) * MXU_DIM]
            rows = slice(kb * MXU_DIM, (kb + 1) * MXU_DIM)
            cols = slice(kb * sw, (kb + 1) * sw)
            dst_re[:, cols] = jnp.dot(ub, bre_ref[rows, cols], preferred_element_type=F32)
            dst_im[:, cols] = jnp.dot(ub, bim_ref[rows, cols], preferred_element_type=F32)

    def stage_scan(src_re, src_im, dst_re, dst_im):
        ncp = S5_CHUNKS_PER_PASS
        for lc0 in range(0, n_state // LANES, ncp):
            cols = [slice((lc0 + c) * LANES, (lc0 + c + 1) * LANES) for c in range(ncp)]
            a_re = [tab_ref[0, :, cs] for cs in cols]
            a_im = [tab_ref[1, :, cs] for cs in cols]
            state = [(carry[0, :, cs], carry[1, :, cs]) for cs in cols]
            for t0 in range(0, tt, 2):
                toks = (tt - 1 - t0, tt - 2 - t0) if reverse else (t0, t0 + 1)
                pair = slice(min(toks) * nb, (min(toks) + 2) * nb)
                for c, cs in enumerate(cols):
                    hr, hm = state[c]
                    out = {}
                    for tok in toks:
                        rows = slice(tok * nb, (tok + 1) * nb)
                        hr, hm = (a_re[c] * hr - a_im[c] * hm + src_re[rows, cs],
                                  a_re[c] * hm + a_im[c] * hr + src_im[rows, cs])
                        out[tok] = (hr, hm)
                    lo, hi = out[min(toks)], out[max(toks)]
                    dst_re[pair, cs] = jnp.concatenate([lo[0], hi[0]], axis=0).astype(BF16)
                    dst_im[pair, cs] = jnp.concatenate([lo[1], hi[1]], axis=0).astype(BF16)
                    state[c] = (hr, hm)
            for c, cs in enumerate(cols):
                carry[0, :, cs] = state[c][0]
                carry[1, :, cs] = state[c][1]

    def stage_readout(src_re, src_im):
        n_out = cre_ref.shape[1]
        nblocks = n_out // MXU_DIM
        rw = n_state // nblocks
        for ob in range(nblocks):
            rows = slice(ob * rw, (ob + 1) * rw)
            cols = slice(ob * MXU_DIM, (ob + 1) * MXU_DIM)
            y = (jnp.dot(src_re[:, rows], cre_ref[rows, cols], preferred_element_type=F32)
                 - jnp.dot(src_im[:, rows], cim_ref[rows, cols], preferred_element_type=F32))
            y_ref[:, :, cols] = y.reshape(tt, nb, MXU_DIM)

    @pl.when(s % 2 == 0)
    def _():
        stage_bu(bu0re, bu0im)
        stage_scan(bu1re, bu1im, h1re, h1im)
        stage_readout(h0re, h0im)

    @pl.when(s % 2 == 1)
    def _():
        stage_bu(bu1re, bu1im)
        stage_scan(bu0re, bu0im, h0re, h0im)
        stage_readout(h1re, h1im)


def _s5_call(u_ctx, u_lat, b_re, b_im, c_re, c_im, tab, *, reverse, tt):
    lc, b, w = u_ctx.shape
    l = u_lat.shape[0]
    n_state = b_re.shape[1]
    assert b == SUBLANES, "the scan keeps one batch row per sublane"
    assert lc % tt == 0 and l % tt == 0 and w % MXU_DIM == 0
    assert n_state % (LANES * S5_CHUNKS_PER_PASS) == 0
    nc, nl = lc // tt, l // tt

    def ctx_tile(s):
        j = jnp.minimum(s, nc - 1)
        return nc - 1 - j if reverse else j

    def lat_tile(s):
        j = jnp.clip(s - nc, 0, nl - 1)
        return nl - 1 - j if reverse else j

    bu_tile = pltpu.VMEM((tt * b, n_state), F32)
    h_tile = pltpu.VMEM((tt * b, n_state), BF16)
    return pl.pallas_call(
        functools.partial(_s5_kernel, nc=nc, tt=tt, reverse=reverse),
        grid=(nc + nl + 2,),
        in_specs=[pl.BlockSpec((tt, b, w), lambda s: (ctx_tile(s), 0, 0)),
                  pl.BlockSpec((tt, b, w), lambda s: (lat_tile(s), 0, 0)),
                  _const_spec(b_re.shape), _const_spec(b_im.shape),
                  _const_spec(c_re.shape), _const_spec(c_im.shape),
                  _const_spec(tab.shape)],
        out_specs=pl.BlockSpec((tt, b, w), lambda s: (lat_tile(s - 2), 0, 0)),
        out_shape=jax.ShapeDtypeStruct((l, b, w), F32),
        scratch_shapes=[bu_tile] * 4 + [h_tile] * 4 + [pltpu.VMEM((2, SUBLANES, n_state), F32)],
        name="s5_bwd" if reverse else "s5_fwd",
        compiler_params=_cparams("arbitrary"),
    )(u_ctx, u_lat, b_re, b_im, c_re, c_im, tab)


_NT = (((1,), (1,)), ((), ()))
ATTN_BOUND_LIMIT = 48.0
ATTN_BOUND_MARGIN = 1.02


def _attn_kernel(bound_ref, q_ref, kl_ref, vl_ref, kc_ref, vc_ref, lq_ref, lk_ref, sg_ref, o_ref,
                 qs, m_s, acc, *, tq, tk, rc, lam_init):
    hw = q_ref.shape[1]
    half = hw // 2
    q = q_ref[...]
    lane = lax.broadcasted_iota(jnp.int32, q.shape, 1)
    zero = jnp.zeros_like(q)
    qs[0:tq, :] = jnp.where(lane < half, q, zero)
    qs[tq:2 * tq, :] = jnp.where(lane >= half, q, zero)
    chunks = [slice(c * rc, (c + 1) * rc) for c in range(2 * tq // rc)]
    blocks = [(kc_ref, vc_ref, 0, kc_ref.shape[0])]
    blocks += [(kl_ref, vl_ref, j * tk, tk) for j in range(kl_ref.shape[0] // tk)]

    base = ((pl.program_id(0) * pl.num_programs(1) + pl.program_id(1)) * pl.num_programs(2)
            + pl.program_id(2)) * 2
    bounds = [bound_ref[base], bound_ref[base + 1]]
    bound_ok = jnp.maximum(bounds[0], bounds[1]) <= ATTN_BOUND_LIMIT

    @pl.when(bound_ok)
    def _():
        for rows in chunks:
            m = bounds[rows.start // tq]
            total = None
            for k_ref, v_ref, off, width in blocks:
                s = lax.dot_general(qs[rows, :], k_ref[off:off + width, :], _NT,
                                    preferred_element_type=F32)
                p = jnp.exp2(s - m).astype(BF16)
                pv = jnp.dot(p, v_ref[off:off + width, :], preferred_element_type=F32)
                total = pv if total is None else total + pv
            acc[rows, :] = total

    @pl.when(jnp.logical_not(bound_ok))
    def _():
        for rows in chunks:
            s = lax.dot_general(qs[rows, :], kc_ref[...], _NT, preferred_element_type=F32)
            m = jnp.max(s, axis=-1, keepdims=True)
            p = jnp.exp2((s - m).astype(BF16))
            acc[rows, :] = jnp.dot(p, vc_ref[...], preferred_element_type=F32)
            m_s[rows, :] = m

        def kv_block(j, carry):
            off = pl.multiple_of(j * tk, tk)
            for rows in chunks:
                s = lax.dot_general(qs[rows, :], kl_ref[pl.ds(off, tk), :], _NT,
                                    preferred_element_type=F32)
                m_prev = m_s[rows, :]
                m_new = jnp.maximum(m_prev, jnp.max(s, axis=-1, keepdims=True))
                alpha = jnp.exp2(m_prev - m_new)
                p = jnp.exp2((s - m_new).astype(BF16))
                acc[rows, :] = alpha * acc[rows, :] + jnp.dot(
                    p, vl_ref[pl.ds(off, tk), :], preferred_element_type=F32)
                m_s[rows, :] = m_new
            return carry

        lax.fori_loop(0, kl_ref.shape[0] // tk, kv_block, 0)

    lq = lq_ref[...]
    lk = lk_ref[...]
    lam = (jnp.exp(jnp.sum(lq[0:1, :] * lk[0:1, :], axis=-1, keepdims=True))
           - jnp.exp(jnp.sum(lq[1:2, :] * lk[1:2, :], axis=-1, keepdims=True)) + lam_init)
    o = (acc[0:tq, 0:hw] / acc[0:tq, hw:2 * hw]
         - lam * (acc[tq:2 * tq, 0:hw] / acc[tq:2 * tq, hw:2 * hw]))
    ms = jnp.mean(o * o, axis=-1, keepdims=True)
    o_ref[...] = ((o * lax.rsqrt(ms + EPS) * sg_ref[...]) * (1.0 - lam_init)).astype(o_ref.dtype)


def _attn_call(bounds, q, k_lat, v_lat, k_ctx, v_ctx, lam_q, lam_k, subln_g, *,
               lam_init, tq, tk, rc):
    b, l, w = q.shape
    lc = k_ctx.shape[1]
    hw = w // DIFF_HEADS
    assert l % tq == 0 and l % tk == 0 and tq % rc == 0 and hw == LANES
    assert bounds.shape == (b * DIFF_HEADS * (l // tq) * 2,)
    qmap = lambda bi, hi, qi, bnd: (bi, qi, hi)
    kvmap = lambda bi, hi, qi, bnd: (bi, 0, hi)
    const = lambda shape: pl.BlockSpec(shape, lambda bi, hi, qi, bnd: (0,) * len(shape),
                                       pipeline_mode=pl.Buffered(1))
    return pl.pallas_call(
        functools.partial(_attn_kernel, tq=tq, tk=tk, rc=rc, lam_init=lam_init),
        grid_spec=pltpu.PrefetchScalarGridSpec(
            num_scalar_prefetch=1,
            grid=(b, DIFF_HEADS, l // tq),
            in_specs=[pl.BlockSpec((None, tq, hw), qmap),
                      pl.BlockSpec((None, l, hw), kvmap), pl.BlockSpec((None, l, 2 * hw), kvmap),
                      pl.BlockSpec((None, lc, hw), kvmap), pl.BlockSpec((None, lc, 2 * hw), kvmap),
                      const(lam_q.shape), const(lam_k.shape), const(subln_g.shape)],
            out_specs=pl.BlockSpec((None, tq, hw), qmap),
            scratch_shapes=[pltpu.VMEM((2 * tq, hw), BF16), pltpu.VMEM((2 * tq, 1), F32),
                            pltpu.VMEM((2 * tq, 2 * hw), F32)]),
        out_shape=jax.ShapeDtypeStruct((b, l, w), BF16),
        name="diff_attn",
        compiler_params=_cparams("parallel", "parallel", "parallel"),
    )(bounds, q, k_lat, v_lat, k_ctx, v_ctx, lam_q, lam_k, subln_g)


def _block_diag(blocks):
    g, r, c = blocks.shape
    eye = jnp.eye(g, dtype=blocks.dtype)
    return (eye[:, None, :, None] * blocks[:, :, None, :]).reshape(g * r, g * c)


def kernel(x, c, ctx, c_ctx, w_mod, b_mod, norm_g, ffn_w_in, ffn_w_out, w_in, w_out, ssm_a_re, ssm_a_im, ssm_log_dt, ssm_b_re, ssm_b_im, ssm_c_re, ssm_c_im, ssm_d, w_glu, b_glu, lam_q, lam_k, subln_g, final_g):
    b, l, d = x.shape
    lc = ctx.shape[1]
    depth = w_mod.shape[0]
    assert depth == 1, "context-stream update between layers is not implemented"
    n_groups, n_state_g, n_in_g = ssm_b_re.shape[2:]
    ssm_w = n_groups * n_in_g
    n_state = n_groups * n_state_g
    head_dim = lam_q.shape[-1]
    ctx_row = b
    rows = SUBLANES * (-(-(b + 1) // SUBLANES))

    tm_lat = 512 if l % 512 == 0 else 256
    tm_ctx = 256
    tt_s5 = 64
    tq = 1024 if l % 1024 == 0 else 256
    tk = 1024 if l % 1024 == 0 else 256
    rc = 256

    layer = 0
    lam_init = 0.8 - 0.6 * math.exp(-0.3 * layer)

    c_rows = jnp.zeros((rows, d), F32).at[:b].set(c).at[b].set(c_ctx)
    mod = _mod_call(c_rows, w_mod[layer], b_mod[layer]).reshape(rows, N_MOD, d)
    lat_row = lambda bi: bi
    ctx_mod_row = lambda bi: ctx_row

    g0, g1, g2 = (norm_g[layer, i].reshape(1, d) for i in range(3))
    w1_in, w1_out = ffn_w_in[layer, 0].astype(BF16), ffn_w_out[layer, 0].astype(BF16)
    w2_in, w2_out = ffn_w_in[layer, 1].astype(BF16), ffn_w_out[layer, 1].astype(BF16)
    w_mix = w_in[layer].astype(BF16)

    x1 = _ffn_call(x, mod, g0, w1_in, w1_out, mod_i=0, mod_row=lat_row, tm=tm_lat)
    ctx1 = _ffn_call(ctx, mod, g0, w1_in, w1_out, mod_i=0, mod_row=ctx_mod_row, tm=tm_ctx)

    tabs = _rope_tables(l, head_dim)
    u_lat, q_lat, k_lat, v_lat, kn_lat, qn_lat = _inproj_call(
        x1, mod, g1, w_mix, mod_row=lat_row, tm=tm_lat, head_dim=head_dim, rope_tabs=tabs,
        q_scale=head_dim ** -0.5 * math.log2(math.e))
    u_ctx, k_ctx, v_ctx, kn_ctx = _inproj_call(ctx1, mod, g1, w_mix, mod_row=ctx_mod_row,
                                               tm=tm_ctx, head_dim=head_dim)

    flat = lambda p: p[layer].reshape(2, n_state)
    log_dt = jnp.repeat(ssm_log_dt[layer], n_state_g, axis=-1)
    b_t = lambda p: p[layer].transpose(0, 3, 1, 2).reshape(2, n_in_g, n_state)
    tab, bbar_re, bbar_im = _s5_prep_call(flat(ssm_a_re), flat(ssm_a_im), log_dt,
                                          b_t(ssm_b_re), b_t(ssm_b_im))
    ys = []
    for direction in range(2):
        to_blocks = lambda m: _block_diag(
            m[direction].reshape(n_in_g, n_groups, n_state_g).transpose(1, 0, 2)).astype(BF16)
        c_blocks = lambda p: _block_diag(p[layer, direction].transpose(0, 2, 1)).astype(BF16)
        y_dir = _s5_call(u_ctx.reshape(lc, b, ssm_w), u_lat.reshape(l, b, ssm_w),
                         to_blocks(bbar_re), to_blocks(bbar_im),
                         c_blocks(ssm_c_re), c_blocks(ssm_c_im), tab[direction],
                         reverse=direction == 1, tt=tt_s5)
        ys.append(y_dir.reshape(l, b * ssm_w))

    n_hc = 2 * DIFF_HEADS
    kn2 = jnp.maximum(kn_lat.max(axis=(1, 2)), kn_ctx.max(axis=(1, 2)))[:, None, :n_hc]
    qn2 = qn_lat[:, :, 0, :n_hc]
    if tq >= tm_lat:
        qn2 = qn2.reshape(b, l // tq, tq // tm_lat, n_hc).max(axis=2)
    else:
        qn2 = jnp.repeat(qn2, tm_lat // tq, axis=1)
    bounds = jnp.sqrt(qn2 * kn2) * ATTN_BOUND_MARGIN
    bounds = bounds.reshape(b, l // tq, DIFF_HEADS, 2).transpose(0, 2, 1, 3).reshape(-1)
    a_lat = _attn_call(bounds, q_lat, k_lat, v_lat, k_ctx, v_ctx, lam_q[layer], lam_k[layer],
                       subln_g[layer].reshape(1, -1), lam_init=lam_init, tq=tq, tk=tk, rc=rc)

    w_o = w_out[layer].astype(BF16)
    mixer = (u_lat, ys[0], ys[1], a_lat, ssm_d[layer].reshape(1, ssm_w),
             w_glu[layer].astype(BF16), b_glu[layer].reshape(1, ssm_w), w_o[:ssm_w], w_o[ssm_w:])
    return _ffn_call(x1, mod, g2, w2_in, w2_out, mod_i=2, mod_row=lat_row, tm=tm_lat,
                     final_g=final_g.reshape(1, d), mixer=mixer)
```

```python
import functools
import math

import jax
import jax.numpy as jnp
from jax import lax
from jax.experimental import pallas as pl
from jax.experimental.pallas import tpu as pltpu

F32 = jnp.float32
BF16 = jnp.bfloat16

EPS = 1e-6
N_MOD = 9
GRID_W = 64
ROPE_BASE = 10000.0
DIFF_HEADS = 4
SUBLANES = 8
LANES = 128
MXU_DIM = 256
VMEM_LIMIT = 56 * 1024 * 1024


def _cparams(*sem):
    return pltpu.CompilerParams(dimension_semantics=sem, vmem_limit_bytes=VMEM_LIMIT)


def _const_spec(shape):
    nd = len(shape)
    return pl.BlockSpec(shape, lambda *_: (0,) * nd, pipeline_mode=pl.Buffered(1))


def _rms_adaln(x, g, shift, scale):
    ms = jnp.mean(x * x, axis=-1, keepdims=True)
    return (x * lax.rsqrt(ms + EPS) * g) * (1.0 + scale) + shift


def _mod_kernel(c_ref, w_ref, b_ref, o_ref):
    c = c_ref[...]
    sc = c * jax.nn.sigmoid(c)
    o_ref[...] = jnp.dot(sc, w_ref[...], precision=lax.Precision.HIGHEST,
                         preferred_element_type=F32) + b_ref[...]


def _mod_call(c_rows, w_mod, b_mod):
    rows, d = c_rows.shape
    n = w_mod.shape[1]
    bn = d
    return pl.pallas_call(
        _mod_kernel,
        grid=(n // bn,),
        in_specs=[pl.BlockSpec((rows, d), lambda j: (0, 0)),
                  pl.BlockSpec((d, bn), lambda j: (0, j)),
                  pl.BlockSpec((1, bn), lambda j: (0, j))],
        out_specs=pl.BlockSpec((rows, bn), lambda j: (0, j)),
        out_shape=jax.ShapeDtypeStruct((rows, n), F32),
        name="mod",
        compiler_params=_cparams("arbitrary"),
    )(c_rows, w_mod, b_mod.reshape(1, n))


def _mixer_out(x, mod_ref, u_ref, yf_ref, yb_ref, a_ref, d_ref, wg_ref, bg_ref, wos_ref, woa_ref):
    y = d_ref[...] * u_ref[...] + yf_ref[...] + yb_ref[...]
    g = jax.nn.gelu(y)
    z = jnp.dot(g.astype(BF16), wg_ref[...], preferred_element_type=F32) + bg_ref[...]
    s = g * jax.nn.sigmoid(z)
    out = (jnp.dot(s.astype(BF16), wos_ref[...], preferred_element_type=F32)
           + jnp.dot(a_ref[...], woa_ref[...], preferred_element_type=F32))
    return x + mod_ref[5:6, :] * out


def _ffn_kernel(x_ref, mod_ref, *rest, mod_i, d_ff, fc, final, mixer):
    x = x_ref[...]
    if mixer:
        x = _mixer_out(x, mod_ref, *rest[:9])
        rest = rest[9:]
    g_ref, win_ref, wout_ref = rest[:3]
    rest = rest[3:]
    if final:
        fg_ref, o_ref, acc_ref = rest
    else:
        o_ref, acc_ref = rest
    shift = mod_ref[3 * mod_i:3 * mod_i + 1, :]
    scale = mod_ref[3 * mod_i + 1:3 * mod_i + 2, :]
    gate = mod_ref[3 * mod_i + 2:3 * mod_i + 3, :]
    h = _rms_adaln(x, g_ref[...], shift, scale).astype(BF16)
    for j in range(d_ff // fc):
        gt = jnp.dot(h, win_ref[:, j * fc:(j + 1) * fc], preferred_element_type=F32)
        up = jnp.dot(h, win_ref[:, d_ff + j * fc:d_ff + (j + 1) * fc], preferred_element_type=F32)
        act = (gt * jax.nn.sigmoid(gt) * up).astype(BF16)
        part = jnp.dot(act, wout_ref[j * fc:(j + 1) * fc, :], preferred_element_type=F32)
        if j == 0:
            acc_ref[...] = part
        else:
            acc_ref[...] += part
    y = x + (0.5 * gate) * acc_ref[...]
    if final:
        ms = jnp.mean(y * y, axis=-1, keepdims=True)
        y = y * lax.rsqrt(ms + EPS) * fg_ref[...]
    o_ref[...] = y


def _ffn_call(x, mod, g, w_in, w_out, *, mod_i, mod_row, tm, final_g=None, mixer=None):
    b, t, d = x.shape
    d_ff = w_out.shape[0]
    fc = MXU_DIM
    assert t % tm == 0 and d_ff % fc == 0
    final = final_g is not None
    in_specs = [pl.BlockSpec((None, tm, d), lambda bi, ti: (bi, ti, 0)),
                pl.BlockSpec((None, N_MOD, d), lambda bi, ti: (mod_row(bi), 0, 0))]
    args = [x, mod]
    if mixer is not None:
        u, yf, yb, a = mixer[:4]
        w = a.shape[2]
        tmajor = pl.BlockSpec((tm, w), lambda bi, ti: (ti, bi))
        in_specs += [tmajor, tmajor, tmajor, pl.BlockSpec((None, tm, w), lambda bi, ti: (bi, ti, 0))]
        in_specs += [_const_spec(p.shape) for p in mixer[4:]]
        args += list(mixer)
    in_specs += [_const_spec((1, d)), _const_spec(w_in.shape), _const_spec(w_out.shape)]
    args += [g, w_in, w_out]
    if final:
        in_specs.append(_const_spec((1, d)))
        args.append(final_g)
    return pl.pallas_call(
        functools.partial(_ffn_kernel, mod_i=mod_i, d_ff=d_ff, fc=fc, final=final,
                          mixer=mixer is not None),
        grid=(b, t // tm),
        in_specs=in_specs,
        out_specs=pl.BlockSpec((None, tm, d), lambda bi, ti: (bi, ti, 0)),
        out_shape=jax.ShapeDtypeStruct((b, t, d), F32),
        scratch_shapes=[pltpu.VMEM((tm, d), F32)],
        name="ffn_final" if final else "ffn",
        compiler_params=_cparams("parallel", "parallel"),
    )(*args)


def _rope(t, cos, sin_signed, first_half):
    outs = []
    for j in range(t.shape[1] // LANES):
        tj = t[:, j * LANES:(j + 1) * LANES]
        partner = jnp.where(first_half, pltpu.roll(tj, LANES - 16, 1), pltpu.roll(tj, 16, 1))
        outs.append(tj * cos + partner * sin_signed)
    return jnp.concatenate(outs, axis=1)


def _tile_max_norm2(xb, indicator, out_shape):
    xf = xb.astype(F32)
    norm2 = jnp.dot((xf * xf).astype(BF16), indicator, preferred_element_type=F32)
    return jnp.broadcast_to(jnp.max(norm2, axis=0, keepdims=True), out_shape)


def _inproj_kernel(x_ref, mod_ref, g_ref, w_ref, ind_ref, *rest, rope, width, q_scale):
    if rope:
        cos_ref, sin_ref, u_ref, q_ref, k_ref, v_ref, kn_ref, qn_ref = rest
    else:
        u_ref, k_ref, v_ref, kn_ref = rest
    h = _rms_adaln(x_ref[...], g_ref[...], mod_ref[3:4, :], mod_ref[4:5, :]).astype(BF16)
    u_ref[...] = jnp.dot(h, w_ref[:, 0:width], preferred_element_type=F32)
    k = jnp.dot(h, w_ref[:, 2 * width:3 * width], preferred_element_type=F32)
    if rope:
        q = jnp.dot(h, w_ref[:, width:2 * width], preferred_element_type=F32)
        cos = cos_ref[...]
        sin_signed = sin_ref[...]
        lane = lax.broadcasted_iota(jnp.int32, (1, LANES), 1)
        first_half = (lane % 32) < 16
        qb = (_rope(q, cos, sin_signed, first_half) * q_scale).astype(BF16)
        q_ref[...] = qb
        qn_ref[...] = _tile_max_norm2(qb, ind_ref[...], qn_ref.shape)
        k = _rope(k, cos, sin_signed, first_half)
    kb = k.astype(BF16)
    k_ref[...] = kb
    kn_ref[...] = _tile_max_norm2(kb, ind_ref[...], kn_ref.shape)
    v = jnp.dot(h, w_ref[:, 3 * width:4 * width], preferred_element_type=F32).astype(BF16)
    ones = jnp.ones((v.shape[0], LANES), BF16)
    for hd in range(width // LANES):
        v_ref[:, 2 * hd * LANES:(2 * hd + 1) * LANES] = v[:, hd * LANES:(hd + 1) * LANES]
        v_ref[:, (2 * hd + 1) * LANES:(2 * hd + 2) * LANES] = ones


def _inproj_call(x, mod, g, w, *, mod_row, tm, head_dim, rope_tabs=None, q_scale=1.0):
    b, t, d = x.shape
    width = w.shape[1] // 4
    assert t % tm == 0
    rope = rope_tabs is not None
    tok = lambda bi, ti: (bi, ti, 0)
    segment = jnp.arange(width, dtype=jnp.int32)[:, None] // head_dim
    indicator = (segment == jnp.arange(LANES, dtype=jnp.int32)[None, :]).astype(BF16)
    in_specs = [pl.BlockSpec((None, tm, d), tok),
                pl.BlockSpec((None, N_MOD, d), lambda bi, ti: (mod_row(bi), 0, 0)),
                _const_spec((1, d)), _const_spec(w.shape), _const_spec(indicator.shape)]
    args = [x, mod, g, w, indicator]
    out_specs = [pl.BlockSpec((tm, width), lambda bi, ti: (ti, bi))]
    out_shape = [jax.ShapeDtypeStruct((t, b * width), F32)]
    n_qk = 1
    if rope:
        in_specs += [pl.BlockSpec((tm, LANES), lambda bi, ti: (ti, 0))] * 2
        args += list(rope_tabs)
        n_qk = 2
    out_specs += [pl.BlockSpec((None, tm, width), tok)] * n_qk
    out_shape += [jax.ShapeDtypeStruct((b, t, width), BF16)] * n_qk
    out_specs.append(pl.BlockSpec((None, tm, 2 * width), tok))
    out_shape.append(jax.ShapeDtypeStruct((b, t, 2 * width), BF16))
    for _ in range(n_qk):
        out_specs.append(pl.BlockSpec((None, None, SUBLANES, LANES), lambda bi, ti: (bi, ti, 0, 0)))
        out_shape.append(jax.ShapeDtypeStruct((b, t // tm, SUBLANES, LANES), F32))
    return pl.pallas_call(
        functools.partial(_inproj_kernel, rope=rope, width=width, q_scale=q_scale),
        grid=(b, t // tm),
        in_specs=in_specs, out_specs=out_specs, out_shape=out_shape,
        name="inproj_rope" if rope else "inproj",
        compiler_params=_cparams("parallel", "parallel"),
    )(*args)


def _rope_tables(length, head_dim):
    n_freq = head_dim // 4
    pos = jnp.arange(length, dtype=jnp.int32)
    row = (pos // GRID_W).astype(F32)
    col = (pos % GRID_W).astype(F32)
    inv_freq = ROPE_BASE ** (-jnp.arange(n_freq, dtype=F32) / n_freq)
    ang = jnp.stack([row[:, None] * inv_freq, col[:, None] * inv_freq], axis=1)
    cos = jnp.broadcast_to(jnp.cos(ang)[:, :, None, :], (length, 2, 2, n_freq))
    sign = jnp.array([-1.0, 1.0], F32)[None, None, :, None]
    sin = jnp.sin(ang)[:, :, None, :] * sign
    reps = LANES // head_dim
    cos = jnp.tile(cos.reshape(length, head_dim), (1, reps))
    sin = jnp.tile(jnp.broadcast_to(sin, (length, 2, 2, n_freq)).reshape(length, head_dim), (1, reps))
    return cos, sin


def _s5_prep_kernel(are_ref, aim_ref, ldt_ref, bre_ref, bim_ref, tab_ref, obre_ref, obim_ref):
    n_dir = are_ref.shape[0]
    n_state = are_ref.shape[1]
    for d in range(n_dir):
        a_re = are_ref[d:d + 1, :]
        a_im = aim_ref[d:d + 1, :]
        dt = jnp.exp(ldt_ref[d:d + 1, :])
        mag = jnp.exp(dt * a_re)
        abar_re = mag * jnp.cos(dt * a_im)
        abar_im = mag * jnp.sin(dt * a_im)
        zr = abar_re - 1.0
        zi = abar_im
        den = a_re * a_re + a_im * a_im
        coef_re = (zr * a_re + zi * a_im) / den
        coef_im = (zi * a_re - zr * a_im) / den
        b_re = bre_ref[d]
        b_im = bim_ref[d]
        obre_ref[d] = coef_re * b_re - coef_im * b_im
        obim_ref[d] = coef_re * b_im + coef_im * b_re
        tab_ref[d, 0] = jnp.broadcast_to(abar_re, (SUBLANES, n_state))
        tab_ref[d, 1] = jnp.broadcast_to(abar_im, (SUBLANES, n_state))


def _s5_prep_call(a_re, a_im, log_dt, b_re_t, b_im_t):
    n_dir, n_state = a_re.shape
    h = b_re_t.shape[1]
    return pl.pallas_call(
        _s5_prep_kernel,
        out_shape=[jax.ShapeDtypeStruct((n_dir, 2, SUBLANES, n_state), F32),
                   jax.ShapeDtypeStruct((n_dir, h, n_state), F32),
                   jax.ShapeDtypeStruct((n_dir, h, n_state), F32)],
        name="s5_prep",
    )(a_re, a_im, log_dt, b_re_t, b_im_t)


S5_CHUNKS_PER_PASS = 8


def _s5_kernel(uc_ref, ul_ref, bre_ref, bim_ref, cre_ref, cim_ref, tab_ref, y_ref,
               bu0re, bu0im, bu1re, bu1im, h0re, h0im, h1re, h1im, carry, *, nc, tt, reverse):
    s = pl.program_id(0)
    n_in = bre_ref.shape[0]
    n_state = bre_ref.shape[1]
    nb = uc_ref.shape[1]
    rows_t = tt * nb
    kblocks = n_in // MXU_DIM
    sw = n_state // kblocks

    @pl.when(s == 0)
    def _():
        carry[...] = jnp.zeros_like(carry)
        for buf in (bu1re, bu1im, h0re, h0im):
            buf[...] = jnp.zeros_like(buf)

    def stage_bu(dst_re, dst_im):
        u = jnp.where(s < nc, uc_ref[...], ul_ref[...]).reshape(rows_t, n_in).astype(BF16)
        for kb in range(kblocks):
            ub = u[:, kb * MXU_DIM:(kb + 1) * MXU_DIM]
            rows = slice(kb * MXU_DIM, (kb + 1) * MXU_DIM)
            cols = slice(kb * sw, (kb + 1) * sw)
            dst_re[:, cols] = jnp.dot(ub, bre_ref[rows, cols], preferred_element_type=F32)
            dst_im[:, cols] = jnp.dot(ub, bim_ref[rows, cols], preferred_element_type=F32)

    def stage_scan(src_re, src_im, dst_re, dst_im):
        ncp = S5_CHUNKS_PER_PASS
        for lc0 in range(0, n_state // LANES, ncp):
            cols = [slice((lc0 + c) * LANES, (lc0 + c + 1) * LANES) for c in range(ncp)]
            a_re = [tab_ref[0, :, cs] for cs in cols]
            a_im = [tab_ref[1, :, cs] for cs in cols]
            state = [(carry[0, :, cs], carry[1, :, cs]) for cs in cols]
            for t0 in range(0, tt, 2):
                toks = (tt - 1 - t0, tt - 2 - t0) if reverse else (t0, t0 + 1)
                pair = slice(min(toks) * nb, (min(toks) + 2) * nb)
                for c, cs in enumerate(cols):
                    hr, hm = state[c]
                    out = {}
                    for tok in toks:
                        rows = slice(tok * nb, (tok + 1) * nb)
                        hr, hm = (a_re[c] * hr - a_im[c] * hm + src_re[rows, cs],
                                  a_re[c] * hm + a_im[c] * hr + src_im[rows, cs])
                        out[tok] = (hr, hm)
                    lo, hi = out[min(toks)], out[max(toks)]
                    dst_re[pair, cs] = jnp.concatenate([lo[0], hi[0]], axis=0).astype(BF16)
                    dst_im[pair, cs] = jnp.concatenate([lo[1], hi[1]], axis=0).astype(BF16)
                    state[c] = (hr, hm)
            for c, cs in enumerate(cols):
                carry[0, :, cs] = state[c][0]
                carry[1, :, cs] = state[c][1]

    def stage_readout(src_re, src_im):
        n_out = cre_ref.shape[1]
        nblocks = n_out // MXU_DIM
        rw = n_state // nblocks
        for ob in range(nblocks):
            rows = slice(ob * rw, (ob + 1) * rw)
            cols = slice(ob * MXU_DIM, (ob + 1) * MXU_DIM)
            y = (jnp.dot(src_re[:, rows], cre_ref[rows, cols], preferred_element_type=F32)
                 - jnp.dot(src_im[:, rows], cim_ref[rows, cols], preferred_element_type=F32))
            y_ref[:, :, cols] = y.reshape(tt, nb, MXU_DIM)

    @pl.when(s % 2 == 0)
    def _():
        stage_bu(bu0re, bu0im)
        stage_scan(bu1re, bu1im, h1re, h1im)
        stage_readout(h0re, h0im)

    @pl.when(s % 2 == 1)
    def _():
        stage_bu(bu1re, bu1im)
        stage_scan(bu0re, bu0im, h0re, h0im)
        stage_readout(h1re, h1im)


def _s5_call(u_ctx, u_lat, b_re, b_im, c_re, c_im, tab, *, reverse, tt):
    lc, b, w = u_ctx.shape
    l = u_lat.shape[0]
    n_state = b_re.shape[1]
    assert b == SUBLANES, "the scan keeps one batch row per sublane"
    assert lc % tt == 0 and l % tt == 0 and w % MXU_DIM == 0
    assert n_state % (LANES * S5_CHUNKS_PER_PASS) == 0
    nc, nl = lc // tt, l // tt

    def ctx_tile(s):
        j = jnp.minimum(s, nc - 1)
        return nc - 1 - j if reverse else j

    def lat_tile(s):
        j = jnp.clip(s - nc, 0, nl - 1)
        return nl - 1 - j if reverse else j

    bu_tile = pltpu.VMEM((tt * b, n_state), F32)
    h_tile = pltpu.VMEM((tt * b, n_state), BF16)
    return pl.pallas_call(
        functools.partial(_s5_kernel, nc=nc, tt=tt, reverse=reverse),
        grid=(nc + nl + 2,),
        in_specs=[pl.BlockSpec((tt, b, w), lambda s: (ctx_tile(s), 0, 0)),
                  pl.BlockSpec((tt, b, w), lambda s: (lat_tile(s), 0, 0)),
                  _const_spec(b_re.shape), _const_spec(b_im.shape),
                  _const_spec(c_re.shape), _const_spec(c_im.shape),
                  _const_spec(tab.shape)],
        out_specs=pl.BlockSpec((tt, b, w), lambda s: (lat_tile(s - 2), 0, 0)),
        out_shape=jax.ShapeDtypeStruct((l, b, w), F32),
        scratch_shapes=[bu_tile] * 4 + [h_tile] * 4 + [pltpu.VMEM((2, SUBLANES, n_state), F32)],
        name="s5_bwd" if reverse else "s5_fwd",
        compiler_params=_cparams("arbitrary"),
    )(u_ctx, u_lat, b_re, b_im, c_re, c_im, tab)


_NT = (((1,), (1,)), ((), ()))
ATTN_BOUND_LIMIT = 48.0
ATTN_BOUND_MARGIN = 1.02


def _attn_kernel(bound_ref, q_ref, kl_ref, vl_ref, kc_ref, vc_ref, lq_ref, lk_ref, sg_ref, o_ref,
                 qs, m_s, acc, *, tq, tk, rc, lam_init):
    hw = q_ref.shape[1]
    half = hw // 2
    q = q_ref[...]
    lane = lax.broadcasted_iota(jnp.int32, q.shape, 1)
    zero = jnp.zeros_like(q)
    qs[0:tq, :] = jnp.where(lane < half, q, zero)
    qs[tq:2 * tq, :] = jnp.where(lane >= half, q, zero)
    chunks = [slice(c * rc, (c + 1) * rc) for c in range(2 * tq // rc)]
    blocks = [(kc_ref, vc_ref, 0, kc_ref.shape[0])]
    blocks += [(kl_ref, vl_ref, j * tk, tk) for j in range(kl_ref.shape[0] // tk)]

    base = ((pl.program_id(0) * pl.num_programs(1) + pl.program_id(1)) * pl.num_programs(2)
            + pl.program_id(2)) * 2
    bounds = [bound_ref[base], bound_ref[base + 1]]
    bound_ok = jnp.maximum(bounds[0], bounds[1]) <= ATTN_BOUND_LIMIT

    @pl.when(bound_ok)
    def _():
        for rows in chunks:
            m = bounds[rows.start // tq]
            total = None
            for k_ref, v_ref, off, width in blocks:
                s = lax.dot_general(qs[rows, :], k_ref[off:off + width, :], _NT,
                                    preferred_element_type=F32)
                p = jnp.exp2(s - m).astype(BF16)
                pv = jnp.dot(p, v_ref[off:off + width, :], preferred_element_type=F32)
                total = pv if total is None else total + pv
            acc[rows, :] = total

    @pl.when(jnp.logical_not(bound_ok))
    def _():
        for rows in chunks:
            s = lax.dot_general(qs[rows, :], kc_ref[...], _NT, preferred_element_type=F32)
            m = jnp.max(s, axis=-1, keepdims=True)
            p = jnp.exp2((s - m).astype(BF16))
            acc[rows, :] = jnp.dot(p, vc_ref[...], preferred_element_type=F32)
            m_s[rows, :] = m

        def kv_block(j, carry):
            off = pl.multiple_of(j * tk, tk)
            for rows in chunks:
                s = lax.dot_general(qs[rows, :], kl_ref[pl.ds(off, tk), :], _NT,
                                    preferred_element_type=F32)
                m_prev = m_s[rows, :]
                m_new = jnp.maximum(m_prev, jnp.max(s, axis=-1, keepdims=True))
                alpha = jnp.exp2(m_prev - m_new)
                p = jnp.exp2((s - m_new).astype(BF16))
                acc[rows, :] = alpha * acc[rows, :] + jnp.dot(
                    p, vl_ref[pl.ds(off, tk), :], preferred_element_type=F32)
                m_s[rows, :] = m_new
            return carry

        lax.fori_loop(0, kl_ref.shape[0] // tk, kv_block, 0)

    lq = lq_ref[...]
    lk = lk_ref[...]
    lam = (jnp.exp(jnp.sum(lq[0:1, :] * lk[0:1, :], axis=-1, keepdims=True))
           - jnp.exp(jnp.sum(lq[1:2, :] * lk[1:2, :], axis=-1, keepdims=True)) + lam_init)
    o = (acc[0:tq, 0:hw] / acc[0:tq, hw:2 * hw]
         - lam * (acc[tq:2 * tq, 0:hw] / acc[tq:2 * tq, hw:2 * hw]))
    ms = jnp.mean(o * o, axis=-1, keepdims=True)
    o_ref[...] = ((o * lax.rsqrt(ms + EPS) * sg_ref[...]) * (1.0 - lam_init)).astype(o_ref.dtype)


def _attn_call(bounds, q, k_lat, v_lat, k_ctx, v_ctx, lam_q, lam_k, subln_g, *,
               lam_init, tq, tk, rc):
    b, l, w = q.shape
    lc = k_ctx.shape[1]
    hw = w // DIFF_HEADS
    assert l % tq == 0 and l % tk == 0 and tq % rc == 0 and hw == LANES
    assert bounds.shape == (b * DIFF_HEADS * (l // tq) * 2,)
    qmap = lambda bi, hi, qi, bnd: (bi, qi, hi)
    kvmap = lambda bi, hi, qi, bnd: (bi, 0, hi)
    const = lambda shape: pl.BlockSpec(shape, lambda bi, hi, qi, bnd: (0,) * len(shape),
                                       pipeline_mode=pl.Buffered(1))
    return pl.pallas_call(
        functools.partial(_attn_kernel, tq=tq, tk=tk, rc=rc, lam_init=lam_init),
        grid_spec=pltpu.PrefetchScalarGridSpec(
            num_scalar_prefetch=1,
            grid=(b, DIFF_HEADS, l // tq),
            in_specs=[pl.BlockSpec((None, tq, hw), qmap),
                      pl.BlockSpec((None, l, hw), kvmap), pl.BlockSpec((None, l, 2 * hw), kvmap),
                      pl.BlockSpec((None, lc, hw), kvmap), pl.BlockSpec((None, lc, 2 * hw), kvmap),
                      const(lam_q.shape), const(lam_k.shape), const(subln_g.shape)],
            out_specs=pl.BlockSpec((None, tq, hw), qmap),
            scratch_shapes=[pltpu.VMEM((2 * tq, hw), BF16), pltpu.VMEM((2 * tq, 1), F32),
                            pltpu.VMEM((2 * tq, 2 * hw), F32)]),
        out_shape=jax.ShapeDtypeStruct((b, l, w), BF16),
        name="diff_attn",
        compiler_params=_cparams("parallel", "parallel", "parallel"),
    )(bounds, q, k_lat, v_lat, k_ctx, v_ctx, lam_q, lam_k, subln_g)


def _block_diag(blocks):
    g, r, c = blocks.shape
    eye = jnp.eye(g, dtype=blocks.dtype)
    return (eye[:, None, :, None] * blocks[:, :, None, :]).reshape(g * r, g * c)


def kernel(x, c, ctx, c_ctx, w_mod, b_mod, norm_g, ffn_w_in, ffn_w_out, w_in, w_out, ssm_a_re, ssm_a_im, ssm_log_dt, ssm_b_re, ssm_b_im, ssm_c_re, ssm_c_im, ssm_d, w_glu, b_glu, lam_q, lam_k, subln_g, final_g):
    b, l, d = x.shape
    lc = ctx.shape[1]
    depth = w_mod.shape[0]
    assert depth == 1, "context-stream update between layers is not implemented"
    n_groups, n_state_g, n_in_g = ssm_b_re.shape[2:]
    ssm_w = n_groups * n_in_g
    n_state = n_groups * n_state_g
    head_dim = lam_q.shape[-1]
    ctx_row = b
    rows = SUBLANES * (-(-(b + 1) // SUBLANES))

    tm_lat = 512 if l % 512 == 0 else 256
    tm_ctx = 256
    tt_s5 = 32
    tq = 1024 if l % 1024 == 0 else 256
    tk = 1024 if l % 1024 == 0 else 256
    rc = 256

    layer = 0
    lam_init = 0.8 - 0.6 * math.exp(-0.3 * layer)

    c_rows = jnp.zeros((rows, d), F32).at[:b].set(c).at[b].set(c_ctx)
    mod = _mod_call(c_rows, w_mod[layer], b_mod[layer]).reshape(rows, N_MOD, d)
    lat_row = lambda bi: bi
    ctx_mod_row = lambda bi: ctx_row

    g0, g1, g2 = (norm_g[layer, i].reshape(1, d) for i in range(3))
    w1_in, w1_out = ffn_w_in[layer, 0].astype(BF16), ffn_w_out[layer, 0].astype(BF16)
    w2_in, w2_out = ffn_w_in[layer, 1].astype(BF16), ffn_w_out[layer, 1].astype(BF16)
    w_mix = w_in[layer].astype(BF16)

    x1 = _ffn_call(x, mod, g0, w1_in, w1_out, mod_i=0, mod_row=lat_row, tm=tm_lat)
    ctx1 = _ffn_call(ctx, mod, g0, w1_in, w1_out, mod_i=0, mod_row=ctx_mod_row, tm=tm_ctx)

    tabs = _rope_tables(l, head_dim)
    u_lat, q_lat, k_lat, v_lat, kn_lat, qn_lat = _inproj_call(
        x1, mod, g1, w_mix, mod_row=lat_row, tm=tm_lat, head_dim=head_dim, rope_tabs=tabs,
        q_scale=head_dim ** -0.5 * math.log2(math.e))
    u_ctx, k_ctx, v_ctx, kn_ctx = _inproj_call(ctx1, mod, g1, w_mix, mod_row=ctx_mod_row,
                                               tm=tm_ctx, head_dim=head_dim)

    flat = lambda p: p[layer].reshape(2, n_state)
    log_dt = jnp.repeat(ssm_log_dt[layer], n_state_g, axis=-1)
    b_t = lambda p: p[layer].transpose(0, 3, 1, 2).reshape(2, n_in_g, n_state)
    tab, bbar_re, bbar_im = _s5_prep_call(flat(ssm_a_re), flat(ssm_a_im), log_dt,
                                          b_t(ssm_b_re), b_t(ssm_b_im))
    ys = []
    for direction in range(2):
        to_blocks = lambda m: _block_diag(
            m[direction].reshape(n_in_g, n_groups, n_state_g).transpose(1, 0, 2)).astype(BF16)
        c_blocks = lambda p: _block_diag(p[layer, direction].transpose(0, 2, 1)).astype(BF16)
        y_dir = _s5_call(u_ctx.reshape(lc, b, ssm_w), u_lat.reshape(l, b, ssm_w),
                         to_blocks(bbar_re), to_blocks(bbar_im),
                         c_blocks(ssm_c_re), c_blocks(ssm_c_im), tab[direction],
                         reverse=direction == 1, tt=tt_s5)
        ys.append(y_dir.reshape(l, b * ssm_w))

    n_hc = 2 * DIFF_HEADS
    kn2 = jnp.maximum(kn_lat.max(axis=(1, 2)), kn_ctx.max(axis=(1, 2)))[:, None, :n_hc]
    qn2 = qn_lat[:, :, 0, :n_hc]
    if tq >= tm_lat:
        qn2 = qn2.reshape(b, l // tq, tq // tm_lat, n_hc).max(axis=2)
    else:
        qn2 = jnp.repeat(qn2, tm_lat // tq, axis=1)
    bounds = jnp.sqrt(qn2 * kn2) * ATTN_BOUND_MARGIN
    bounds = bounds.reshape(b, l // tq, DIFF_HEADS, 2).transpose(0, 2, 1, 3).reshape(-1)
    a_lat = _attn_call(bounds, q_lat, k_lat, v_lat, k_ctx, v_ctx, lam_q[layer], lam_k[layer],
                       subln_g[layer].reshape(1, -1), lam_init=lam_init, tq=tq, tk=tk, rc=rc)

    w_o = w_out[layer].astype(BF16)
    mixer = (u_lat, ys[0], ys[1], a_lat, ssm_d[layer].reshape(1, ssm_w),
             w_glu[layer].astype(BF16), b_glu[layer].reshape(1, ssm_w), w_o[:ssm_w], w_o[ssm_w:])
    return _ffn_call(x1, mod, g2, w2_in, w2_out, mod_i=2, mod_row=lat_row, tm=tm_lat,
                     final_g=final_g.reshape(1, d), mixer=mixer)
```

```python
import functools
import math

import jax
import jax.numpy as jnp
from jax import lax
from jax.experimental import pallas as pl
from jax.experimental.pallas import tpu as pltpu

F32 = jnp.float32
BF16 = jnp.bfloat16

EPS = 1e-6
N_MOD = 9
GRID_W = 64
ROPE_BASE = 10000.0
DIFF_HEADS = 4
SUBLANES = 8
LANES = 128
MXU_DIM = 256
VMEM_LIMIT = 56 * 1024 * 1024


def _cparams(*sem):
    return pltpu.CompilerParams(dimension_semantics=sem, vmem_limit_bytes=VMEM_LIMIT)


def _const_spec(shape):
    nd = len(shape)
    return pl.BlockSpec(shape, lambda *_: (0,) * nd, pipeline_mode=pl.Buffered(1))


def _rms_adaln(x, g, shift, scale):
    ms = jnp.mean(x * x, axis=-1, keepdims=True)
    return (x * lax.rsqrt(ms + EPS) * g) * (1.0 + scale) + shift


def _mod_kernel(c_ref, w_ref, b_ref, o_ref):
    c = c_ref[...]
    sc = c * jax.nn.sigmoid(c)
    o_ref[...] = jnp.dot(sc, w_ref[...], precision=lax.Precision.HIGHEST,
                         preferred_element_type=F32) + b_ref[...]


def _mod_call(c_rows, w_mod, b_mod):
    rows, d = c_rows.shape
    n = w_mod.shape[1]
    bn = d
    return pl.pallas_call(
        _mod_kernel,
        grid=(n // bn,),
        in_specs=[pl.BlockSpec((rows, d), lambda j: (0, 0)),
                  pl.BlockSpec((d, bn), lambda j: (0, j)),
                  pl.BlockSpec((1, bn), lambda j: (0, j))],
        out_specs=pl.BlockSpec((rows, bn), lambda j: (0, j)),
        out_shape=jax.ShapeDtypeStruct((rows, n), F32),
        name="mod",
        compiler_params=_cparams("arbitrary"),
    )(c_rows, w_mod, b_mod.reshape(1, n))


def _mixer_out(x, mod_ref, u_ref, yf_ref, yb_ref, a_ref, d_ref, wg_ref, bg_ref, wos_ref, woa_ref):
    y = d_ref[...] * u_ref[...] + yf_ref[...] + yb_ref[...]
    g = jax.nn.gelu(y)
    z = jnp.dot(g.astype(BF16), wg_ref[...], preferred_element_type=F32) + bg_ref[...]
    s = g * jax.nn.sigmoid(z)
    out = (jnp.dot(s.astype(BF16), wos_ref[...], preferred_element_type=F32)
           + jnp.dot(a_ref[...], woa_ref[...], preferred_element_type=F32))
    return x + mod_ref[5:6, :] * out


def _ffn_kernel(x_ref, mod_ref, *rest, mod_i, d_ff, fc, final, mixer):
    x = x_ref[...]
    if mixer:
        x = _mixer_out(x, mod_ref, *rest[:9])
        rest = rest[9:]
    g_ref, win_ref, wout_ref = rest[:3]
    rest = rest[3:]
    if final:
        fg_ref, o_ref, acc_ref = rest
    else:
        o_ref, acc_ref = rest
    shift = mod_ref[3 * mod_i:3 * mod_i + 1, :]
    scale = mod_ref[3 * mod_i + 1:3 * mod_i + 2, :]
    gate = mod_ref[3 * mod_i + 2:3 * mod_i + 3, :]
    h = _rms_adaln(x, g_ref[...], shift, scale).astype(BF16)
    for j in range(d_ff // fc):
        gt = jnp.dot(h, win_ref[:, j * fc:(j + 1) * fc], preferred_element_type=F32)
        up = jnp.dot(h, win_ref[:, d_ff + j * fc:d_ff + (j + 1) * fc], preferred_element_type=F32)
        act = (gt * jax.nn.sigmoid(gt) * up).astype(BF16)
        part = jnp.dot(act, wout_ref[j * fc:(j + 1) * fc, :], preferred_element_type=F32)
        if j == 0:
            acc_ref[...] = part
        else:
            acc_ref[...] += part
    y = x + (0.5 * gate) * acc_ref[...]
    if final:
        ms = jnp.mean(y * y, axis=-1, keepdims=True)
        y = y * lax.rsqrt(ms + EPS) * fg_ref[...]
    o_ref[...] = y


def _ffn_call(x, mod, g, w_in, w_out, *, mod_i, mod_row, tm, final_g=None, mixer=None):
    b, t, d = x.shape
    d_ff = w_out.shape[0]
    fc = MXU_DIM
    assert t % tm == 0 and d_ff % fc == 0
    final = final_g is not None
    in_specs = [pl.BlockSpec((None, tm, d), lambda bi, ti: (bi, ti, 0)),
                pl.BlockSpec((None, N_MOD, d), lambda bi, ti: (mod_row(bi), 0, 0))]
    args = [x, mod]
    if mixer is not None:
        u, yf, yb, a = mixer[:4]
        w = a.shape[2]
        tmajor = pl.BlockSpec((tm, w), lambda bi, ti: (ti, bi))
        in_specs += [tmajor, tmajor, tmajor, pl.BlockSpec((None, tm, w), lambda bi, ti: (bi, ti, 0))]
        in_specs += [_const_spec(p.shape) for p in mixer[4:]]
        args += list(mixer)
    in_specs += [_const_spec((1, d)), _const_spec(w_in.shape), _const_spec(w_out.shape)]
    args += [g, w_in, w_out]
    if final:
        in_specs.append(_const_spec((1, d)))
        args.append(final_g)
    return pl.pallas_call(
        functools.partial(_ffn_kernel, mod_i=mod_i, d_ff=d_ff, fc=fc, final=final,
                          mixer=mixer is not None),
        grid=(b, t // tm),
        in_specs=in_specs,
        out_specs=pl.BlockSpec((None, tm, d), lambda bi, ti: (bi, ti, 0)),
        out_shape=jax.ShapeDtypeStruct((b, t, d), F32),
        scratch_shapes=[pltpu.VMEM((tm, d), F32)],
        name="ffn_final" if final else "ffn",
        compiler_params=_cparams("parallel", "parallel"),
    )(*args)


def _rope(t, cos, sin_signed, first_half):
    outs = []
    for j in range(t.shape[1] // LANES):
        tj = t[:, j * LANES:(j + 1) * LANES]
        partner = jnp.where(first_half, pltpu.roll(tj, LANES - 16, 1), pltpu.roll(tj, 16, 1))
        outs.append(tj * cos + partner * sin_signed)
    return jnp.concatenate(outs, axis=1)


def _tile_max_norm2(xb, indicator, out_shape):
    xf = xb.astype(F32)
    norm2 = jnp.dot((xf * xf).astype(BF16), indicator, preferred_element_type=F32)
    return jnp.broadcast_to(jnp.max(norm2, axis=0, keepdims=True), out_shape)


def _inproj_kernel(x_ref, mod_ref, g_ref, w_ref, ind_ref, *rest, rope, width, q_scale):
    if rope:
        cos_ref, sin_ref, u_ref, q_ref, k_ref, v_ref, kn_ref, qn_ref = rest
    else:
        u_ref, k_ref, v_ref, kn_ref = rest
    h = _rms_adaln(x_ref[...], g_ref[...], mod_ref[3:4, :], mod_ref[4:5, :]).astype(BF16)
    u_ref[...] = jnp.dot(h, w_ref[:, 0:width], preferred_element_type=F32)
    k = jnp.dot(h, w_ref[:, 2 * width:3 * width], preferred_element_type=F32)
    if rope:
        q = jnp.dot(h, w_ref[:, width:2 * width], preferred_element_type=F32)
        cos = cos_ref[...]
        sin_signed = sin_ref[...]
        lane = lax.broadcasted_iota(jnp.int32, (1, LANES), 1)
        first_half = (lane % 32) < 16
        qb = (_rope(q, cos, sin_signed, first_half) * q_scale).astype(BF16)
        q_ref[...] = qb
        qn_ref[...] = _tile_max_norm2(qb, ind_ref[...], qn_ref.shape)
        k = _rope(k, cos, sin_signed, first_half)
    kb = k.astype(BF16)
    k_ref[...] = kb
    kn_ref[...] = _tile_max_norm2(kb, ind_ref[...], kn_ref.shape)
    v = jnp.dot(h, w_ref[:, 3 * width:4 * width], preferred_element_type=F32).astype(BF16)
    ones = jnp.ones((v.shape[0], LANES), BF16)
    for hd in range(width // LANES):
        v_ref[:, 2 * hd * LANES:(2 * hd + 1) * LANES] = v[:, hd * LANES:(hd + 1) * LANES]
        v_ref[:, (2 * hd + 1) * LANES:(2 * hd + 2) * LANES] = ones


def _inproj_call(x, mod, g, w, *, mod_row, tm, head_dim, rope_tabs=None, q_scale=1.0):
    b, t, d = x.shape
    width = w.shape[1] // 4
    assert t % tm == 0
    rope = rope_tabs is not None
    tok = lambda bi, ti: (bi, ti, 0)
    segment = jnp.arange(width, dtype=jnp.int32)[:, None] // head_dim
    indicator = (segment == jnp.arange(LANES, dtype=jnp.int32)[None, :]).astype(BF16)
    in_specs = [pl.BlockSpec((None, tm, d), tok),
                pl.BlockSpec((None, N_MOD, d), lambda bi, ti: (mod_row(bi), 0, 0)),
                _const_spec((1, d)), _const_spec(w.shape), _const_spec(indicator.shape)]
    args = [x, mod, g, w, indicator]
    out_specs = [pl.BlockSpec((tm, width), lambda bi, ti: (ti, bi))]
    out_shape = [jax.ShapeDtypeStruct((t, b * width), F32)]
    n_qk = 1
    if rope:
        in_specs += [pl.BlockSpec((tm, LANES), lambda bi, ti: (ti, 0))] * 2
        args += list(rope_tabs)
        n_qk = 2
    out_specs += [pl.BlockSpec((None, tm, width), tok)] * n_qk
    out_shape += [jax.ShapeDtypeStruct((b, t, width), BF16)] * n_qk
    out_specs.append(pl.BlockSpec((None, tm, 2 * width), tok))
    out_shape.append(jax.ShapeDtypeStruct((b, t, 2 * width), BF16))
    for _ in range(n_qk):
        out_specs.append(pl.BlockSpec((None, None, SUBLANES, LANES), lambda bi, ti: (bi, ti, 0, 0)))
        out_shape.append(jax.ShapeDtypeStruct((b, t // tm, SUBLANES, LANES), F32))
    return pl.pallas_call(
        functools.partial(_inproj_kernel, rope=rope, width=width, q_scale=q_scale),
        grid=(b, t // tm),
        in_specs=in_specs, out_specs=out_specs, out_shape=out_shape,
        name="inproj_rope" if rope else "inproj",
        compiler_params=_cparams("parallel", "parallel"),
    )(*args)


def _rope_tables(length, head_dim):
    n_freq = head_dim // 4
    pos = jnp.arange(length, dtype=jnp.int32)
    row = (pos // GRID_W).astype(F32)
    col = (pos % GRID_W).astype(F32)
    inv_freq = ROPE_BASE ** (-jnp.arange(n_freq, dtype=F32) / n_freq)
    ang = jnp.stack([row[:, None] * inv_freq, col[:, None] * inv_freq], axis=1)
    cos = jnp.broadcast_to(jnp.cos(ang)[:, :, None, :], (length, 2, 2, n_freq))
    sign = jnp.array([-1.0, 1.0], F32)[None, None, :, None]
    sin = jnp.sin(ang)[:, :, None, :] * sign
    reps = LANES // head_dim
    cos = jnp.tile(cos.reshape(length, head_dim), (1, reps))
    sin = jnp.tile(jnp.broadcast_to(sin, (length, 2, 2, n_freq)).reshape(length, head_dim), (1, reps))
    return cos, sin


def _s5_prep_kernel(are_ref, aim_ref, ldt_ref, bre_ref, bim_ref, tab_ref, obre_ref, obim_ref):
    n_dir = are_ref.shape[0]
    n_state = are_ref.shape[1]
    for d in range(n_dir):
        a_re = are_ref[d:d + 1, :]
        a_im = aim_ref[d:d + 1, :]
        dt = jnp.exp(ldt_ref[d:d + 1, :])
        mag = jnp.exp(dt * a_re)
        abar_re = mag * jnp.cos(dt * a_im)
        abar_im = mag * jnp.sin(dt * a_im)
        zr = abar_re - 1.0
        zi = abar_im
        den = a_re * a_re + a_im * a_im
        coef_re = (zr * a_re + zi * a_im) / den
        coef_im = (zi * a_re - zr * a_im) / den
        b_re = bre_ref[d]
        b_im = bim_ref[d]
        obre_ref[d] = coef_re * b_re - coef_im * b_im
        obim_ref[d] = coef_re * b_im + coef_im * b_re
        tab_ref[d, 0] = jnp.broadcast_to(abar_re, (SUBLANES, n_state))
        tab_ref[d, 1] = jnp.broadcast_to(abar_im, (SUBLANES, n_state))


def _s5_prep_call(a_re, a_im, log_dt, b_re_t, b_im_t):
    n_dir, n_state = a_re.shape
    h = b_re_t.shape[1]
    return pl.pallas_call(
        _s5_prep_kernel,
        out_shape=[jax.ShapeDtypeStruct((n_dir, 2, SUBLANES, n_state), F32),
                   jax.ShapeDtypeStruct((n_dir, h, n_state), F32),
                   jax.ShapeDtypeStruct((n_dir, h, n_state), F32)],
        name="s5_prep",
    )(a_re, a_im, log_dt, b_re_t, b_im_t)


S5_CHUNKS_PER_PASS = 8


def _s5_kernel(uc_ref, ul_ref, bre_ref, bim_ref, cre_ref, cim_ref, tab_ref, y_ref,
               bu0re, bu0im, bu1re, bu1im, h0re, h0im, h1re, h1im, carry, *, nc, tt, reverse):
    s = pl.program_id(0)
    n_in = bre_ref.shape[0]
    n_state = bre_ref.shape[1]
    nb = uc_ref.shape[1]
    rows_t = tt * nb
    kblocks = n_in // MXU_DIM
    sw = n_state // kblocks

    @pl.when(s == 0)
    def _():
        carry[...] = jnp.zeros_like(carry)
        for buf in (bu1re, bu1im, h0re, h0im):
            buf[...] = jnp.zeros_like(buf)

    def stage_bu(dst_re, dst_im):
        u = jnp.where(s < nc, uc_ref[...], ul_ref[...]).reshape(rows_t, n_in).astype(BF16)
        for kb in range(kblocks):
            ub = u[:, kb * MXU_DIM:(kb + 1) * MXU_DIM]
            rows = slice(kb * MXU_DIM, (kb + 1) * MXU_DIM)
            cols = slice(kb * sw, (kb + 1) * sw)
            dst_re[:, cols] = jnp.dot(ub, bre_ref[rows, cols], preferred_element_type=F32)
            dst_im[:, cols] = jnp.dot(ub, bim_ref[rows, cols], preferred_element_type=F32)

    def stage_scan(src_re, src_im, dst_re, dst_im):
        ncp = S5_CHUNKS_PER_PASS
        for lc0 in range(0, n_state // LANES, ncp):
            cols = [slice((lc0 + c) * LANES, (lc0 + c + 1) * LANES) for c in range(ncp)]
            a_re = [tab_ref[0, :, cs] for cs in cols]
            a_im = [tab_ref[1, :, cs] for cs in cols]
            state = [(carry[0, :, cs], carry[1, :, cs]) for cs in cols]
            for t0 in range(0, tt, 2):
                toks = (tt - 1 - t0, tt - 2 - t0) if reverse else (t0, t0 + 1)
                pair = slice(min(toks) * nb, (min(toks) + 2) * nb)
                for c, cs in enumerate(cols):
                    hr, hm = state[c]
                    out = {}
                    for tok in toks:
                        rows = slice(tok * nb, (tok + 1) * nb)
                        hr, hm = (a_re[c] * hr - a_im[c] * hm + src_re[rows, cs],
                                  a_re[c] * hm + a_im[c] * hr + src_im[rows, cs])
                        out[tok] = (hr, hm)
                    lo, hi = out[min(toks)], out[max(toks)]
                    dst_re[pair, cs] = jnp.concatenate([lo[0], hi[0]], axis=0).astype(BF16)
                    dst_im[pair, cs] = jnp.concatenate([lo[1], hi[1]], axis=0).astype(BF16)
                    state[c] = (hr, hm)
            for c, cs in enumerate(cols):
                carry[0, :, cs] = state[c][0]
                carry[1, :, cs] = state[c][1]

    def stage_readout(src_re, src_im):
        n_out = cre_ref.shape[1]
        nblocks = n_out // MXU_DIM
        rw = n_state // nblocks
        for ob in range(nblocks):
            rows = slice(ob * rw, (ob + 1) * rw)
            cols = slice(ob * MXU_DIM, (ob + 1) * MXU_DIM)
            y = (jnp.dot(src_re[:, rows], cre_ref[rows, cols], preferred_element_type=F32)
                 - jnp.dot(src_im[:, rows], cim_ref[rows, cols], preferred_element_type=F32))
            y_ref[:, :, cols] = y.reshape(tt, nb, MXU_DIM)

    @pl.when(s % 2 == 0)
    def _():
        stage_bu(bu0re, bu0im)
        stage_scan(bu1re, bu1im, h1re, h1im)
        stage_readout(h0re, h0im)

    @pl.when(s % 2 == 1)
    def _():
        stage_bu(bu1re, bu1im)
        stage_scan(bu0re, bu0im, h0re, h0im)
        stage_readout(h1re, h1im)


def _s5_call(u_ctx, u_lat, b_re, b_im, c_re, c_im, tab, *, reverse, tt):
    lc, b, w = u_ctx.shape
    l = u_lat.shape[0]
    n_state = b_re.shape[1]
    assert b == SUBLANES, "the scan keeps one batch row per sublane"
    assert lc % tt == 0 and l % tt == 0 and w % MXU_DIM == 0
    assert n_state % (LANES * S5_CHUNKS_PER_PASS) == 0
    nc, nl = lc // tt, l // tt

    def ctx_tile(s):
        j = jnp.minimum(s, nc - 1)
        return nc - 1 - j if reverse else j

    def lat_tile(s):
        j = jnp.clip(s - nc, 0, nl - 1)
        return nl - 1 - j if reverse else j

    bu_tile = pltpu.VMEM((tt * b, n_state), F32)
    h_tile = pltpu.VMEM((tt * b, n_state), BF16)
    return pl.pallas_call(
        functools.partial(_s5_kernel, nc=nc, tt=tt, reverse=reverse),
        grid=(nc + nl + 2,),
        in_specs=[pl.BlockSpec((tt, b, w), lambda s: (ctx_tile(s), 0, 0)),
                  pl.BlockSpec((tt, b, w), lambda s: (lat_tile(s), 0, 0)),
                  _const_spec(b_re.shape), _const_spec(b_im.shape),
                  _const_spec(c_re.shape), _const_spec(c_im.shape),
                  _const_spec(tab.shape)],
        out_specs=pl.BlockSpec((tt, b, w), lambda s: (lat_tile(s - 2), 0, 0)),
        out_shape=jax.ShapeDtypeStruct((l, b, w), F32),
        scratch_shapes=[bu_tile] * 4 + [h_tile] * 4 + [pltpu.VMEM((2, SUBLANES, n_state), F32)],
        name="s5_bwd" if reverse else "s5_fwd",
        compiler_params=_cparams("arbitrary"),
    )(u_ctx, u_lat, b_re, b_im, c_re, c_im, tab)


_NT = (((1,), (1,)), ((), ()))
ATTN_BOUND_LIMIT = 48.0
ATTN_BOUND_MARGIN = 1.02


def _attn_kernel(bound_ref, q_ref, kl_ref, vl_ref, kc_ref, vc_ref, lq_ref, lk_ref, sg_ref, o_ref,
                 qs, m_s, acc, *, tq, tk, rc, lam_init):
    hw = q_ref.shape[1]
    half = hw // 2
    q = q_ref[...]
    lane = lax.broadcasted_iota(jnp.int32, q.shape, 1)
    zero = jnp.zeros_like(q)
    qs[0:tq, :] = jnp.where(lane < half, q, zero)
    qs[tq:2 * tq, :] = jnp.where(lane >= half, q, zero)
    chunks = [slice(c * rc, (c + 1) * rc) for c in range(2 * tq // rc)]
    blocks = [(kc_ref, vc_ref, 0, kc_ref.shape[0])]
    blocks += [(kl_ref, vl_ref, j * tk, tk) for j in range(kl_ref.shape[0] // tk)]

    base = ((pl.program_id(0) * pl.num_programs(1) + pl.program_id(1)) * pl.num_programs(2)
            + pl.program_id(2)) * 2
    bounds = [bound_ref[base], bound_ref[base + 1]]
    bound_ok = jnp.maximum(bounds[0], bounds[1]) <= ATTN_BOUND_LIMIT

    @pl.when(bound_ok)
    def _():
        for rows in chunks:
            m = bounds[rows.start // tq]
            total = None
            for k_ref, v_ref, off, width in blocks:
                s = lax.dot_general(qs[rows, :], k_ref[off:off + width, :], _NT,
                                    preferred_element_type=F32)
                p = jnp.exp2(s - m).astype(BF16)
                pv = jnp.dot(p, v_ref[off:off + width, :], preferred_element_type=F32)
                total = pv if total is None else total + pv
            acc[rows, :] = total

    @pl.when(jnp.logical_not(bound_ok))
    def _():
        for rows in chunks:
            s = lax.dot_general(qs[rows, :], kc_ref[...], _NT, preferred_element_type=F32)
            m = jnp.max(s, axis=-1, keepdims=True)
            p = jnp.exp2((s - m).astype(BF16))
            acc[rows, :] = jnp.dot(p, vc_ref[...], preferred_element_type=F32)
            m_s[rows, :] = m

        def kv_block(j, carry):
            off = pl.multiple_of(j * tk, tk)
            for rows in chunks:
                s = lax.dot_general(qs[rows, :], kl_ref[pl.ds(off, tk), :], _NT,
                                    preferred_element_type=F32)
                m_prev = m_s[rows, :]
                m_new = jnp.maximum(m_prev, jnp.max(s, axis=-1, keepdims=True))
                alpha = jnp.exp2(m_prev - m_new)
                p = jnp.exp2((s - m_new).astype(BF16))
                acc[rows, :] = alpha * acc[rows, :] + jnp.dot(
                    p, vl_ref[pl.ds(off, tk), :], preferred_element_type=F32)
                m_s[rows, :] = m_new
            return carry

        lax.fori_loop(0, kl_ref.shape[0] // tk, kv_block, 0)

    lq = lq_ref[...]
    lk = lk_ref[...]
    lam = (jnp.exp(jnp.sum(lq[0:1, :] * lk[0:1, :], axis=-1, keepdims=True))
           - jnp.exp(jnp.sum(lq[1:2, :] * lk[1:2, :], axis=-1, keepdims=True)) + lam_init)
    o = (acc[0:tq, 0:hw] / acc[0:tq, hw:2 * hw]
         - lam * (acc[tq:2 * tq, 0:hw] / acc[tq:2 * tq, hw:2 * hw]))
    ms = jnp.mean(o * o, axis=-1, keepdims=True)
    o_ref[...] = ((o * lax.rsqrt(ms + EPS) * sg_ref[...]) * (1.0 - lam_init)).astype(o_ref.dtype)


def _attn_call(bounds, q, k_lat, v_lat, k_ctx, v_ctx, lam_q, lam_k, subln_g, *,
               lam_init, tq, tk, rc):
    b, l, w = q.shape
    lc = k_ctx.shape[1]
    hw = w // DIFF_HEADS
    assert l % tq == 0 and l % tk == 0 and tq % rc == 0 and hw == LANES
    assert bounds.shape == (b * DIFF_HEADS * (l // tq) * 2,)
    qmap = lambda bi, hi, qi, bnd: (bi, qi, hi)
    kvmap = lambda bi, hi, qi, bnd: (bi, 0, hi)
    const = lambda shape: pl.BlockSpec(shape, lambda bi, hi, qi, bnd: (0,) * len(shape),
                                       pipeline_mode=pl.Buffered(1))
    return pl.pallas_call(
        functools.partial(_attn_kernel, tq=tq, tk=tk, rc=rc, lam_init=lam_init),
        grid_spec=pltpu.PrefetchScalarGridSpec(
            num_scalar_prefetch=1,
            grid=(b, DIFF_HEADS, l // tq),
            in_specs=[pl.BlockSpec((None, tq, hw), qmap),
                      pl.BlockSpec((None, l, hw), kvmap), pl.BlockSpec((None, l, 2 * hw), kvmap),
                      pl.BlockSpec((None, lc, hw), kvmap), pl.BlockSpec((None, lc, 2 * hw), kvmap),
                      const(lam_q.shape), const(lam_k.shape), const(subln_g.shape)],
            out_specs=pl.BlockSpec((None, tq, hw), qmap),
            scratch_shapes=[pltpu.VMEM((2 * tq, hw), BF16), pltpu.VMEM((2 * tq, 1), F32),
                            pltpu.VMEM((2 * tq, 2 * hw), F32)]),
        out_shape=jax.ShapeDtypeStruct((b, l, w), BF16),
        name="diff_attn",
        compiler_params=_cparams("parallel", "parallel", "parallel"),
    )(bounds, q, k_lat, v_lat, k_ctx, v_ctx, lam_q, lam_k, subln_g)


def _block_diag(blocks):
    g, r, c = blocks.shape
    eye = jnp.eye(g, dtype=blocks.dtype)
    return (eye[:, None, :, None] * blocks[:, :, None, :]).reshape(g * r, g * c)


def kernel(x, c, ctx, c_ctx, w_mod, b_mod, norm_g, ffn_w_in, ffn_w_out, w_in, w_out, ssm_a_re, ssm_a_im, ssm_log_dt, ssm_b_re, ssm_b_im, ssm_c_re, ssm_c_im, ssm_d, w_glu, b_glu, lam_q, lam_k, subln_g, final_g):
    b, l, d = x.shape
    lc = ctx.shape[1]
    depth = w_mod.shape[0]
    assert depth == 1, "context-stream update between layers is not implemented"
    n_groups, n_state_g, n_in_g = ssm_b_re.shape[2:]
    ssm_w = n_groups * n_in_g
    n_state = n_groups * n_state_g
    head_dim = lam_q.shape[-1]
    ctx_row = b
    rows = SUBLANES * (-(-(b + 1) // SUBLANES))

    tm_lat = 512 if l % 512 == 0 else 256
    tm_ctx = 256
    tt_s5 = 64
    tq = 1024 if l % 1024 == 0 else 256
    tk = 1024 if l % 1024 == 0 else 256
    rc = 256

    layer = 0
    lam_init = 0.8 - 0.6 * math.exp(-0.3 * layer)

    c_rows = jnp.zeros((rows, d), F32).at[:b].set(c).at[b].set(c_ctx)
    mod = _mod_call(c_rows, w_mod[layer], b_mod[layer]).reshape(rows, N_MOD, d)
    lat_row = lambda bi: bi
    ctx_mod_row = lambda bi: ctx_row

    g0, g1, g2 = (norm_g[layer, i].reshape(1, d) for i in range(3))
    w1_in, w1_out = ffn_w_in[layer, 0].astype(BF16), ffn_w_out[layer, 0].astype(BF16)
    w2_in, w2_out = ffn_w_in[layer, 1].astype(BF16), ffn_w_out[layer, 1].astype(BF16)
    w_mix = w_in[layer].astype(BF16)

    x1 = _ffn_call(x, mod, g0, w1_in, w1_out, mod_i=0, mod_row=lat_row, tm=tm_lat)
    ctx1 = _ffn_call(ctx, mod, g0, w1_in, w1_out, mod_i=0, mod_row=ctx_mod_row, tm=tm_ctx)

    tabs = _rope_tables(l, head_dim)
    u_lat, q_lat, k_lat, v_lat, kn_lat, qn_lat = _inproj_call(
        x1, mod, g1, w_mix, mod_row=lat_row, tm=tm_lat, head_dim=head_dim, rope_tabs=tabs,
        q_scale=head_dim ** -0.5 * math.log2(math.e))
    u_ctx, k_ctx, v_ctx, kn_ctx = _inproj_call(ctx1, mod, g1, w_mix, mod_row=ctx_mod_row,
                                               tm=tm_ctx, head_dim=head_dim)

    flat = lambda p: p[layer].reshape(2, n_state)
    log_dt = jnp.repeat(ssm_log_dt[layer], n_state_g, axis=-1)
    b_t = lambda p: p[layer].transpose(0, 3, 1, 2).reshape(2, n_in_g, n_state)
    tab, bbar_re, bbar_im = _s5_prep_call(flat(ssm_a_re), flat(ssm_a_im), log_dt,
                                          b_t(ssm_b_re), b_t(ssm_b_im))
    ys = []
    for direction in range(2):
        to_blocks = lambda m: _block_diag(
            m[direction].reshape(n_in_g, n_groups, n_state_g).transpose(1, 0, 2)).astype(BF16)
        c_blocks = lambda p: _block_diag(p[layer, direction].transpose(0, 2, 1)).astype(BF16)
        y_dir = _s5_call(u_ctx.reshape(lc, b, ssm_w), u_lat.reshape(l, b, ssm_w),
                         to_blocks(bbar_re), to_blocks(bbar_im),
                         c_blocks(ssm_c_re), c_blocks(ssm_c_im), tab[direction],
                         reverse=direction == 1, tt=tt_s5)
        ys.append(y_dir.reshape(l, b * ssm_w))

    n_hc = 2 * DIFF_HEADS
    kn2 = jnp.maximum(kn_lat.max(axis=(1, 2)), kn_ctx.max(axis=(1, 2)))[:, None, :n_hc]
    qn2 = qn_lat[:, :, 0, :n_hc]
    if tq >= tm_lat:
        qn2 = qn2.reshape(b, l // tq, tq // tm_lat, n_hc).max(axis=2)
    else:
        qn2 = jnp.repeat(qn2, tm_lat // tq, axis=1)
    bounds = jnp.sqrt(qn2 * kn2) * ATTN_BOUND_MARGIN
    bounds = bounds.reshape(b, l // tq, DIFF_HEADS, 2).transpose(0, 2, 1, 3).reshape(-1)
    a_lat = _attn_call(bounds, q_lat, k_lat, v_lat, k_ctx, v_ctx, lam_q[layer], lam_k[layer],
                       subln_g[layer].reshape(1, -1), lam_init=lam_init, tq=tq, tk=tk, rc=rc)

    w_o = w_out[layer].astype(BF16)
    mixer = (u_lat, ys[0], ys[1], a_lat, ssm_d[layer].reshape(1, ssm_w),
             w_glu[layer].astype(BF16), b_glu[layer].reshape(1, ssm_w), w_o[:ssm_w], w_o[ssm_w:])
    return _ffn_call(x1, mod, g2, w2_in, w2_out, mod_i=2, mod_row=lat_row, tm=tm_lat,
                     final_g=final_g.reshape(1, d), mixer=mixer)
```

```python
import functools
import math

import jax
import jax.numpy as jnp
from jax import lax
from jax.experimental import pallas as pl
from jax.experimental.pallas import tpu as pltpu

F32 = jnp.float32
BF16 = jnp.bfloat16

EPS = 1e-6
N_MOD = 9
GRID_W = 64
ROPE_BASE = 10000.0
DIFF_HEADS = 4
SUBLANES = 8
LANES = 128
MXU_DIM = 256
VMEM_LIMIT = 56 * 1024 * 1024


def _cparams(*sem):
    return pltpu.CompilerParams(dimension_semantics=sem, vmem_limit_bytes=VMEM_LIMIT)


def _const_spec(shape):
    nd = len(shape)
    return pl.BlockSpec(shape, lambda *_: (0,) * nd, pipeline_mode=pl.Buffered(1))


def _rms_adaln(x, g, shift, scale):
    ms = jnp.mean(x * x, axis=-1, keepdims=True)
    return (x * lax.rsqrt(ms + EPS) * g) * (1.0 + scale) + shift


def _mod_kernel(c_ref, w_ref, b_ref, o_ref):
    c = c_ref[...]
    sc = c * jax.nn.sigmoid(c)
    o_ref[...] = jnp.dot(sc, w_ref[...], precision=lax.Precision.HIGHEST,
                         preferred_element_type=F32) + b_ref[...]


def _mod_call(c_rows, w_mod, b_mod):
    rows, d = c_rows.shape
    n = w_mod.shape[1]
    bn = d
    return pl.pallas_call(
        _mod_kernel,
        grid=(n // bn,),
        in_specs=[pl.BlockSpec((rows, d), lambda j: (0, 0)),
                  pl.BlockSpec((d, bn), lambda j: (0, j)),
                  pl.BlockSpec((1, bn), lambda j: (0, j))],
        out_specs=pl.BlockSpec((rows, bn), lambda j: (0, j)),
        out_shape=jax.ShapeDtypeStruct((rows, n), F32),
        name="mod",
        compiler_params=_cparams("arbitrary"),
    )(c_rows, w_mod, b_mod.reshape(1, n))


def _mixer_out(x, mod_ref, u_ref, yf_ref, yb_ref, a_ref, d_ref, wg_ref, bg_ref, wos_ref, woa_ref):
    y = d_ref[...] * u_ref[...] + yf_ref[...] + yb_ref[...]
    g = jax.nn.gelu(y)
    z = jnp.dot(g.astype(BF16), wg_ref[...], preferred_element_type=F32) + bg_ref[...]
    s = g * jax.nn.sigmoid(z)
    out = (jnp.dot(s.astype(BF16), wos_ref[...], preferred_element_type=F32)
           + jnp.dot(a_ref[...], woa_ref[...], preferred_element_type=F32))
    return x + mod_ref[5:6, :] * out


def _ffn_kernel(x_ref, mod_ref, *rest, mod_i, d_ff, fc, final, mixer):
    x = x_ref[...]
    if mixer:
        x = _mixer_out(x, mod_ref, *rest[:9])
        rest = rest[9:]
    g_ref, win_ref, wout_ref = rest[:3]
    rest = rest[3:]
    if final:
        fg_ref, o_ref, acc_ref = rest
    else:
        o_ref, acc_ref = rest
    shift = mod_ref[3 * mod_i:3 * mod_i + 1, :]
    scale = mod_ref[3 * mod_i + 1:3 * mod_i + 2, :]
    gate = mod_ref[3 * mod_i + 2:3 * mod_i + 3, :]
    h = _rms_adaln(x, g_ref[...], shift, scale).astype(BF16)
    for j in range(d_ff // fc):
        gt = jnp.dot(h, win_ref[:, j * fc:(j + 1) * fc], preferred_element_type=F32)
        up = jnp.dot(h, win_ref[:, d_ff + j * fc:d_ff + (j + 1) * fc], preferred_element_type=F32)
        act = (gt * jax.nn.sigmoid(gt) * up).astype(BF16)
        part = jnp.dot(act, wout_ref[j * fc:(j + 1) * fc, :], preferred_element_type=F32)
        if j == 0:
            acc_ref[...] = part
        else:
            acc_ref[...] += part
    y = x + (0.5 * gate) * acc_ref[...]
    if final:
        ms = jnp.mean(y * y, axis=-1, keepdims=True)
        y = y * lax.rsqrt(ms + EPS) * fg_ref[...]
    o_ref[...] = y


def _ffn_call(x, mod, g, w_in, w_out, *, mod_i, mod_row, tm, final_g=None, mixer=None):
    b, t, d = x.shape
    d_ff = w_out.shape[0]
    fc = MXU_DIM
    assert t % tm == 0 and d_ff % fc == 0
    final = final_g is not None
    in_specs = [pl.BlockSpec((None, tm, d), lambda bi, ti: (bi, ti, 0)),
                pl.BlockSpec((None, N_MOD, d), lambda bi, ti: (mod_row(bi), 0, 0))]
    args = [x, mod]
    if mixer is not None:
        u, yf, yb, a = mixer[:4]
        w = a.shape[2]
        tmajor = pl.BlockSpec((tm, w), lambda bi, ti: (ti, bi))
        in_specs += [tmajor, tmajor, tmajor, pl.BlockSpec((None, tm, w), lambda bi, ti: (bi, ti, 0))]
        in_specs += [_const_spec(p.shape) for p in mixer[4:]]
        args += list(mixer)
    in_specs += [_const_spec((1, d)), _const_spec(w_in.shape), _const_spec(w_out.shape)]
    args += [g, w_in, w_out]
    if final:
        in_specs.append(_const_spec((1, d)))
        args.append(final_g)
    return pl.pallas_call(
        functools.partial(_ffn_kernel, mod_i=mod_i, d_ff=d_ff, fc=fc, final=final,
                          mixer=mixer is not None),
        grid=(b, t // tm),
        in_specs=in_specs,
        out_specs=pl.BlockSpec((None, tm, d), lambda bi, ti: (bi, ti, 0)),
        out_shape=jax.ShapeDtypeStruct((b, t, d), F32),
        scratch_shapes=[pltpu.VMEM((tm, d), F32)],
        name="ffn_final" if final else "ffn",
        compiler_params=_cparams("parallel", "parallel"),
    )(*args)


def _rope(t, cos, sin_signed, first_half):
    outs = []
    for j in range(t.shape[1] // LANES):
        tj = t[:, j * LANES:(j + 1) * LANES]
        partner = jnp.where(first_half, pltpu.roll(tj, LANES - 16, 1), pltpu.roll(tj, 16, 1))
        outs.append(tj * cos + partner * sin_signed)
    return jnp.concatenate(outs, axis=1)


def _tile_max_norm2(xb, indicator, out_shape):
    xf = xb.astype(F32)
    norm2 = jnp.dot((xf * xf).astype(BF16), indicator, preferred_element_type=F32)
    return jnp.broadcast_to(jnp.max(norm2, axis=0, keepdims=True), out_shape)


def _inproj_kernel(x_ref, mod_ref, g_ref, w_ref, ind_ref, *rest, rope, width, q_scale):
    if rope:
        cos_ref, sin_ref, u_ref, q_ref, k_ref, v_ref, kn_ref, qn_ref = rest
    else:
        u_ref, k_ref, v_ref, kn_ref = rest
    h = _rms_adaln(x_ref[...], g_ref[...], mod_ref[3:4, :], mod_ref[4:5, :]).astype(BF16)
    u_ref[...] = jnp.dot(h, w_ref[:, 0:width], preferred_element_type=F32)
    k = jnp.dot(h, w_ref[:, 2 * width:3 * width], preferred_element_type=F32)
    if rope:
        q = jnp.dot(h, w_ref[:, width:2 * width], preferred_element_type=F32)
        cos = cos_ref[...]
        sin_signed = sin_ref[...]
        lane = lax.broadcasted_iota(jnp.int32, (1, LANES), 1)
        first_half = (lane % 32) < 16
        qb = (_rope(q, cos, sin_signed, first_half) * q_scale).astype(BF16)
        q_ref[...] = qb
        qn_ref[...] = _tile_max_norm2(qb, ind_ref[...], qn_ref.shape)
        k = _rope(k, cos, sin_signed, first_half)
    kb = k.astype(BF16)
    k_ref[...] = kb
    kn_ref[...] = _tile_max_norm2(kb, ind_ref[...], kn_ref.shape)
    v = jnp.dot(h, w_ref[:, 3 * width:4 * width], preferred_element_type=F32).astype(BF16)
    ones = jnp.ones((v.shape[0], LANES), BF16)
    for hd in range(width // LANES):
        v_ref[:, 2 * hd * LANES:(2 * hd + 1) * LANES] = v[:, hd * LANES:(hd + 1) * LANES]
        v_ref[:, (2 * hd + 1) * LANES:(2 * hd + 2) * LANES] = ones


def _inproj_call(x, mod, g, w, *, mod_row, tm, head_dim, rope_tabs=None, q_scale=1.0):
    b, t, d = x.shape
    width = w.shape[1] // 4
    assert t % tm == 0
    rope = rope_tabs is not None
    tok = lambda bi, ti: (bi, ti, 0)
    segment = jnp.arange(width, dtype=jnp.int32)[:, None] // head_dim
    indicator = (segment == jnp.arange(LANES, dtype=jnp.int32)[None, :]).astype(BF16)
    in_specs = [pl.BlockSpec((None, tm, d), tok),
                pl.BlockSpec((None, N_MOD, d), lambda bi, ti: (mod_row(bi), 0, 0)),
                _const_spec((1, d)), _const_spec(w.shape), _const_spec(indicator.shape)]
    args = [x, mod, g, w, indicator]
    out_specs = [pl.BlockSpec((tm, width), lambda bi, ti: (ti, bi))]
    out_shape = [jax.ShapeDtypeStruct((t, b * width), F32)]
    n_qk = 1
    if rope:
        in_specs += [pl.BlockSpec((tm, LANES), lambda bi, ti: (ti, 0))] * 2
        args += list(rope_tabs)
        n_qk = 2
    out_specs += [pl.BlockSpec((None, tm, width), tok)] * n_qk
    out_shape += [jax.ShapeDtypeStruct((b, t, width), BF16)] * n_qk
    out_specs.append(pl.BlockSpec((None, tm, 2 * width), tok))
    out_shape.append(jax.ShapeDtypeStruct((b, t, 2 * width), BF16))
    for _ in range(n_qk):
        out_specs.append(pl.BlockSpec((None, None, SUBLANES, LANES), lambda bi, ti: (bi, ti, 0, 0)))
        out_shape.append(jax.ShapeDtypeStruct((b, t // tm, SUBLANES, LANES), F32))
    return pl.pallas_call(
        functools.partial(_inproj_kernel, rope=rope, width=width, q_scale=q_scale),
        grid=(b, t // tm),
        in_specs=in_specs, out_specs=out_specs, out_shape=out_shape,
        name="inproj_rope" if rope else "inproj",
        compiler_params=_cparams("parallel", "parallel"),
    )(*args)


def _rope_tables(length, head_dim):
    n_freq = head_dim // 4
    pos = jnp.arange(length, dtype=jnp.int32)
    row = (pos // GRID_W).astype(F32)
    col = (pos % GRID_W).astype(F32)
    inv_freq = ROPE_BASE ** (-jnp.arange(n_freq, dtype=F32) / n_freq)
    ang = jnp.stack([row[:, None] * inv_freq, col[:, None] * inv_freq], axis=1)
    cos = jnp.broadcast_to(jnp.cos(ang)[:, :, None, :], (length, 2, 2, n_freq))
    sign = jnp.array([-1.0, 1.0], F32)[None, None, :, None]
    sin = jnp.sin(ang)[:, :, None, :] * sign
    reps = LANES // head_dim
    cos = jnp.tile(cos.reshape(length, head_dim), (1, reps))
    sin = jnp.tile(jnp.broadcast_to(sin, (length, 2, 2, n_freq)).reshape(length, head_dim), (1, reps))
    return cos, sin


def _s5_prep_kernel(are_ref, aim_ref, ldt_ref, bre_ref, bim_ref, tab_ref, obre_ref, obim_ref):
    n_dir = are_ref.shape[0]
    n_state = are_ref.shape[1]
    for d in range(n_dir):
        a_re = are_ref[d:d + 1, :]
        a_im = aim_ref[d:d + 1, :]
        dt = jnp.exp(ldt_ref[d:d + 1, :])
        mag = jnp.exp(dt * a_re)
        abar_re = mag * jnp.cos(dt * a_im)
        abar_im = mag * jnp.sin(dt * a_im)
        zr = abar_re - 1.0
        zi = abar_im
        den = a_re * a_re + a_im * a_im
        coef_re = (zr * a_re + zi * a_im) / den
        coef_im = (zi * a_re - zr * a_im) / den
        b_re = bre_ref[d]
        b_im = bim_ref[d]
        obre_ref[d] = coef_re * b_re - coef_im * b_im
        obim_ref[d] = coef_re * b_im + coef_im * b_re
        tab_ref[d, 0] = jnp.broadcast_to(abar_re, (SUBLANES, n_state))
        tab_ref[d, 1] = jnp.broadcast_to(abar_im, (SUBLANES, n_state))


def _s5_prep_call(a_re, a_im, log_dt, b_re_t, b_im_t):
    n_dir, n_state = a_re.shape
    h = b_re_t.shape[1]
    return pl.pallas_call(
        _s5_prep_kernel,
        out_shape=[jax.ShapeDtypeStruct((n_dir, 2, SUBLANES, n_state), F32),
                   jax.ShapeDtypeStruct((n_dir, h, n_state), F32),
                   jax.ShapeDtypeStruct((n_dir, h, n_state), F32)],
        name="s5_prep",
    )(a_re, a_im, log_dt, b_re_t, b_im_t)


S5_CHUNKS_PER_PASS = 4


def _s5_kernel(uc_ref, ul_ref, bre_ref, bim_ref, cre_ref, cim_ref, tab_ref, y_ref,
               bu0re, bu0im, bu1re, bu1im, h0re, h0im, h1re, h1im, carry, *, nc, tt, reverse):
    s = pl.program_id(0)
    n_in = bre_ref.shape[0]
    n_state = bre_ref.shape[1]
    nb = uc_ref.shape[1]
    rows_t = tt * nb
    kblocks = n_in // MXU_DIM
    sw = n_state // kblocks

    @pl.when(s == 0)
    def _():
        carry[...] = jnp.zeros_like(carry)
        for buf in (bu1re, bu1im, h0re, h0im):
            buf[...] = jnp.zeros_like(buf)

    def stage_bu(dst_re, dst_im):
        u = jnp.where(s < nc, uc_ref[...], ul_ref[...]).reshape(rows_t, n_in).astype(BF16)
        for kb in range(kblocks):
            ub = u[:, kb * MXU_DIM:(kb + 1) * MXU_DIM]
            rows = slice(kb * MXU_DIM, (kb + 1) * MXU_DIM)
            cols = slice(kb * sw, (kb + 1) * sw)
            dst_re[:, cols] = jnp.dot(ub, bre_ref[rows, cols], preferred_element_type=F32)
            dst_im[:, cols] = jnp.dot(ub, bim_ref[rows, cols], preferred_element_type=F32)

    def stage_scan(src_re, src_im, dst_re, dst_im):
        ncp = S5_CHUNKS_PER_PASS
        for lc0 in range(0, n_state // LANES, ncp):
            cols = [slice((lc0 + c) * LANES, (lc0 + c + 1) * LANES) for c in range(ncp)]
            a_re = [tab_ref[0, :, cs] for cs in cols]
            a_im = [tab_ref[1, :, cs] for cs in cols]
            state = [(carry[0, :, cs], carry[1, :, cs]) for cs in cols]
            for t0 in range(0, tt, 2):
                toks = (tt - 1 - t0, tt - 2 - t0) if reverse else (t0, t0 + 1)
                pair = slice(min(toks) * nb, (min(toks) + 2) * nb)
                for c, cs in enumerate(cols):
                    hr, hm = state[c]
                    out = {}
                    for tok in toks:
                        rows = slice(tok * nb, (tok + 1) * nb)
                        hr, hm = (a_re[c] * hr - a_im[c] * hm + src_re[rows, cs],
                                  a_re[c] * hm + a_im[c] * hr + src_im[rows, cs])
                        out[tok] = (hr, hm)
                    lo, hi = out[min(toks)], out[max(toks)]
                    dst_re[pair, cs] = jnp.concatenate([lo[0], hi[0]], axis=0).astype(BF16)
                    dst_im[pair, cs] = jnp.concatenate([lo[1], hi[1]], axis=0).astype(BF16)
                    state[c] = (hr, hm)
            for c, cs in enumerate(cols):
                carry[0, :, cs] = state[c][0]
                carry[1, :, cs] = state[c][1]

    def stage_readout(src_re, src_im):
        n_out = cre_ref.shape[1]
        nblocks = n_out // MXU_DIM
        rw = n_state // nblocks
        for ob in range(nblocks):
            rows = slice(ob * rw, (ob + 1) * rw)
            cols = slice(ob * MXU_DIM, (ob + 1) * MXU_DIM)
            y = (jnp.dot(src_re[:, rows], cre_ref[rows, cols], preferred_element_type=F32)
                 - jnp.dot(src_im[:, rows], cim_ref[rows, cols], preferred_element_type=F32))
            y_ref[:, :, cols] = y.reshape(tt, nb, MXU_DIM)

    @pl.when(s % 2 == 0)
    def _():
        stage_bu(bu0re, bu0im)
        stage_scan(bu1re, bu1im, h1re, h1im)
        stage_readout(h0re, h0im)

    @pl.when(s % 2 == 1)
    def _():
        stage_bu(bu1re, bu1im)
        stage_scan(bu0re, bu0im, h0re, h0im)
        stage_readout(h1re, h1im)


def _s5_call(u_ctx, u_lat, b_re, b_im, c_re, c_im, tab, *, reverse, tt):
    lc, b, w = u_ctx.shape
    l = u_lat.shape[0]
    n_state = b_re.shape[1]
    assert b == SUBLANES, "the scan keeps one batch row per sublane"
    assert lc % tt == 0 and l % tt == 0 and w % MXU_DIM == 0
    assert n_state % (LANES * S5_CHUNKS_PER_PASS) == 0
    nc, nl = lc // tt, l // tt

    def ctx_tile(s):
        j = jnp.minimum(s, nc - 1)
        return nc - 1 - j if reverse else j

    def lat_tile(s):
        j = jnp.clip(s - nc, 0, nl - 1)
        return nl - 1 - j if reverse else j

    bu_tile = pltpu.VMEM((tt * b, n_state), F32)
    h_tile = pltpu.VMEM((tt * b, n_state), BF16)
    return pl.pallas_call(
        functools.partial(_s5_kernel, nc=nc, tt=tt, reverse=reverse),
        grid=(nc + nl + 2,),
        in_specs=[pl.BlockSpec((tt, b, w), lambda s: (ctx_tile(s), 0, 0)),
                  pl.BlockSpec((tt, b, w), lambda s: (lat_tile(s), 0, 0)),
                  _const_spec(b_re.shape), _const_spec(b_im.shape),
                  _const_spec(c_re.shape), _const_spec(c_im.shape),
                  _const_spec(tab.shape)],
        out_specs=pl.BlockSpec((tt, b, w), lambda s: (lat_tile(s - 2), 0, 0)),
        out_shape=jax.ShapeDtypeStruct((l, b, w), F32),
        scratch_shapes=[bu_tile] * 4 + [h_tile] * 4 + [pltpu.VMEM((2, SUBLANES, n_state), F32)],
        name="s5_bwd" if reverse else "s5_fwd",
        compiler_params=_cparams("arbitrary"),
    )(u_ctx, u_lat, b_re, b_im, c_re, c_im, tab)


_NT = (((1,), (1,)), ((), ()))
ATTN_BOUND_LIMIT = 48.0
ATTN_BOUND_MARGIN = 1.02


def _attn_kernel(bound_ref, q_ref, kl_ref, vl_ref, kc_ref, vc_ref, lq_ref, lk_ref, sg_ref, o_ref,
                 qs, m_s, acc, *, tq, tk, rc, lam_init):
    hw = q_ref.shape[1]
    half = hw // 2
    q = q_ref[...]
    lane = lax.broadcasted_iota(jnp.int32, q.shape, 1)
    zero = jnp.zeros_like(q)
    qs[0:tq, :] = jnp.where(lane < half, q, zero)
    qs[tq:2 * tq, :] = jnp.where(lane >= half, q, zero)
    chunks = [slice(c * rc, (c + 1) * rc) for c in range(2 * tq // rc)]
    blocks = [(kc_ref, vc_ref, 0, kc_ref.shape[0])]
    blocks += [(kl_ref, vl_ref, j * tk, tk) for j in range(kl_ref.shape[0] // tk)]

    base = ((pl.program_id(0) * pl.num_programs(1) + pl.program_id(1)) * pl.num_programs(2)
            + pl.program_id(2)) * 2
    bounds = [bound_ref[base], bound_ref[base + 1]]
    bound_ok = jnp.maximum(bounds[0], bounds[1]) <= ATTN_BOUND_LIMIT

    @pl.when(bound_ok)
    def _():
        for rows in chunks:
            m = bounds[rows.start // tq]
            total = None
            for k_ref, v_ref, off, width in blocks:
                s = lax.dot_general(qs[rows, :], k_ref[off:off + width, :], _NT,
                                    preferred_element_type=F32)
                p = jnp.exp2(s - m).astype(BF16)
                pv = jnp.dot(p, v_ref[off:off + width, :], preferred_element_type=F32)
                total = pv if total is None else total + pv
            acc[rows, :] = total

    @pl.when(jnp.logical_not(bound_ok))
    def _():
        for rows in chunks:
            s = lax.dot_general(qs[rows, :], kc_ref[...], _NT, preferred_element_type=F32)
            m = jnp.max(s, axis=-1, keepdims=True)
            p = jnp.exp2((s - m).astype(BF16))
            acc[rows, :] = jnp.dot(p, vc_ref[...], preferred_element_type=F32)
            m_s[rows, :] = m

        def kv_block(j, carry):
            off = pl.multiple_of(j * tk, tk)
            for rows in chunks:
                s = lax.dot_general(qs[rows, :], kl_ref[pl.ds(off, tk), :], _NT,
                                    preferred_element_type=F32)
                m_prev = m_s[rows, :]
                m_new = jnp.maximum(m_prev, jnp.max(s, axis=-1, keepdims=True))
                alpha = jnp.exp2(m_prev - m_new)
                p = jnp.exp2((s - m_new).astype(BF16))
                acc[rows, :] = alpha * acc[rows, :] + jnp.dot(
                    p, vl_ref[pl.ds(off, tk), :], preferred_element_type=F32)
                m_s[rows, :] = m_new
            return carry

        lax.fori_loop(0, kl_ref.shape[0] // tk, kv_block, 0)

    lq = lq_ref[...]
    lk = lk_ref[...]
    lam = (jnp.exp(jnp.sum(lq[0:1, :] * lk[0:1, :], axis=-1, keepdims=True))
           - jnp.exp(jnp.sum(lq[1:2, :] * lk[1:2, :], axis=-1, keepdims=True)) + lam_init)
    o = (acc[0:tq, 0:hw] / acc[0:tq, hw:2 * hw]
         - lam * (acc[tq:2 * tq, 0:hw] / acc[tq:2 * tq, hw:2 * hw]))
    ms = jnp.mean(o * o, axis=-1, keepdims=True)
    o_ref[...] = ((o * lax.rsqrt(ms + EPS) * sg_ref[...]) * (1.0 - lam_init)).astype(o_ref.dtype)


def _attn_call(bounds, q, k_lat, v_lat, k_ctx, v_ctx, lam_q, lam_k, subln_g, *,
               lam_init, tq, tk, rc):
    b, l, w = q.shape
    lc = k_ctx.shape[1]
    hw = w // DIFF_HEADS
    assert l % tq == 0 and l % tk == 0 and tq % rc == 0 and hw == LANES
    assert bounds.shape == (b * DIFF_HEADS * (l // tq) * 2,)
    qmap = lambda bi, hi, qi, bnd: (bi, qi, hi)
    kvmap = lambda bi, hi, qi, bnd: (bi, 0, hi)
    const = lambda shape: pl.BlockSpec(shape, lambda bi, hi, qi, bnd: (0,) * len(shape),
                                       pipeline_mode=pl.Buffered(1))
    return pl.pallas_call(
        functools.partial(_attn_kernel, tq=tq, tk=tk, rc=rc, lam_init=lam_init),
        grid_spec=pltpu.PrefetchScalarGridSpec(
            num_scalar_prefetch=1,
            grid=(b, DIFF_HEADS, l // tq),
            in_specs=[pl.BlockSpec((None, tq, hw), qmap),
                      pl.BlockSpec((None, l, hw), kvmap), pl.BlockSpec((None, l, 2 * hw), kvmap),
                      pl.BlockSpec((None, lc, hw), kvmap), pl.BlockSpec((None, lc, 2 * hw), kvmap),
                      const(lam_q.shape), const(lam_k.shape), const(subln_g.shape)],
            out_specs=pl.BlockSpec((None, tq, hw), qmap),
            scratch_shapes=[pltpu.VMEM((2 * tq, hw), BF16), pltpu.VMEM((2 * tq, 1), F32),
                            pltpu.VMEM((2 * tq, 2 * hw), F32)]),
        out_shape=jax.ShapeDtypeStruct((b, l, w), BF16),
        name="diff_attn",
        compiler_params=_cparams("parallel", "parallel", "parallel"),
    )(bounds, q, k_lat, v_lat, k_ctx, v_ctx, lam_q, lam_k, subln_g)


def _block_diag(blocks):
    g, r, c = blocks.shape
    eye = jnp.eye(g, dtype=blocks.dtype)
    return (eye[:, None, :, None] * blocks[:, :, None, :]).reshape(g * r, g * c)


def kernel(x, c, ctx, c_ctx, w_mod, b_mod, norm_g, ffn_w_in, ffn_w_out, w_in, w_out, ssm_a_re, ssm_a_im, ssm_log_dt, ssm_b_re, ssm_b_im, ssm_c_re, ssm_c_im, ssm_d, w_glu, b_glu, lam_q, lam_k, subln_g, final_g):
    b, l, d = x.shape
    lc = ctx.shape[1]
    depth = w_mod.shape[0]
    assert depth == 1, "context-stream update between layers is not implemented"
    n_groups, n_state_g, n_in_g = ssm_b_re.shape[2:]
    ssm_w = n_groups * n_in_g
    n_state = n_groups * n_state_g
    head_dim = lam_q.shape[-1]
    ctx_row = b
    rows = SUBLANES * (-(-(b + 1) // SUBLANES))

    tm_lat = 512 if l % 512 == 0 else 256
    tm_ctx = 256
    tt_s5 = 64
    tq = 1024 if l % 1024 == 0 else 256
    tk = 1024 if l % 1024 == 0 else 256
    rc = 256

    layer = 0
    lam_init = 0.8 - 0.6 * math.exp(-0.3 * layer)

    c_rows = jnp.zeros((rows, d), F32).at[:b].set(c).at[b].set(c_ctx)
    mod = _mod_call(c_rows, w_mod[layer], b_mod[layer]).reshape(rows, N_MOD, d)
    lat_row = lambda bi: bi
    ctx_mod_row = lambda bi: ctx_row

    g0, g1, g2 = (norm_g[layer, i].reshape(1, d) for i in range(3))
    w1_in, w1_out = ffn_w_in[layer, 0].astype(BF16), ffn_w_out[layer, 0].astype(BF16)
    w2_in, w2_out = ffn_w_in[layer, 1].astype(BF16), ffn_w_out[layer, 1].astype(BF16)
    w_mix = w_in[layer].astype(BF16)

    x1 = _ffn_call(x, mod, g0, w1_in, w1_out, mod_i=0, mod_row=lat_row, tm=tm_lat)
    ctx1 = _ffn_call(ctx, mod, g0, w1_in, w1_out, mod_i=0, mod_row=ctx_mod_row, tm=tm_ctx)

    tabs = _rope_tables(l, head_dim)
    u_lat, q_lat, k_lat, v_lat, kn_lat, qn_lat = _inproj_call(
        x1, mod, g1, w_mix, mod_row=lat_row, tm=tm_lat, head_dim=head_dim, rope_tabs=tabs,
        q_scale=head_dim ** -0.5 * math.log2(math.e))
    u_ctx, k_ctx, v_ctx, kn_ctx = _inproj_call(ctx1, mod, g1, w_mix, mod_row=ctx_mod_row,
                                               tm=tm_ctx, head_dim=head_dim)

    flat = lambda p: p[layer].reshape(2, n_state)
    log_dt = jnp.repeat(ssm_log_dt[layer], n_state_g, axis=-1)
    b_t = lambda p: p[layer].transpose(0, 3, 1, 2).reshape(2, n_in_g, n_state)
    tab, bbar_re, bbar_im = _s5_prep_call(flat(ssm_a_re), flat(ssm_a_im), log_dt,
                                          b_t(ssm_b_re), b_t(ssm_b_im))
    ys = []
    for direction in range(2):
        to_blocks = lambda m: _block_diag(
            m[direction].reshape(n_in_g, n_groups, n_state_g).transpose(1, 0, 2)).astype(BF16)
        c_blocks = lambda p: _block_diag(p[layer, direction].transpose(0, 2, 1)).astype(BF16)
        y_dir = _s5_call(u_ctx.reshape(lc, b, ssm_w), u_lat.reshape(l, b, ssm_w),
                         to_blocks(bbar_re), to_blocks(bbar_im),
                         c_blocks(ssm_c_re), c_blocks(ssm_c_im), tab[direction],
                         reverse=direction == 1, tt=tt_s5)
        ys.append(y_dir.reshape(l, b * ssm_w))

    n_hc = 2 * DIFF_HEADS
    kn2 = jnp.maximum(kn_lat.max(axis=(1, 2)), kn_ctx.max(axis=(1, 2)))[:, None, :n_hc]
    qn2 = qn_lat[:, :, 0, :n_hc]
    if tq >= tm_lat:
        qn2 = qn2.reshape(b, l // tq, tq // tm_lat, n_hc).max(axis=2)
    else:
        qn2 = jnp.repeat(qn2, tm_lat // tq, axis=1)
    bounds = jnp.sqrt(qn2 * kn2) * ATTN_BOUND_MARGIN
    bounds = bounds.reshape(b, l // tq, DIFF_HEADS, 2).transpose(0, 2, 1, 3).reshape(-1)
    a_lat = _attn_call(bounds, q_lat, k_lat, v_lat, k_ctx, v_ctx, lam_q[layer], lam_k[layer],
                       subln_g[layer].reshape(1, -1), lam_init=lam_init, tq=tq, tk=tk, rc=rc)

    w_o = w_out[layer].astype(BF16)
    mixer = (u_lat, ys[0], ys[1], a_lat, ssm_d[layer].reshape(1, ssm_w),
             w_glu[layer].astype(BF16), b_glu[layer].reshape(1, ssm_w), w_o[:ssm_w], w_o[ssm_w:])
    return _ffn_call(x1, mod, g2, w2_in, w2_out, mod_i=2, mod_row=lat_row, tm=tm_lat,
                     final_g=final_g.reshape(1, d), mixer=mixer)
```

```python
import functools
import math

import jax
import jax.numpy as jnp
from jax import lax
from jax.experimental import pallas as pl
from jax.experimental.pallas import tpu as pltpu

F32 = jnp.float32
BF16 = jnp.bfloat16

EPS = 1e-6
N_MOD = 9
GRID_W = 64
ROPE_BASE = 10000.0
DIFF_HEADS = 4
SUBLANES = 8
LANES = 128
MXU_DIM = 256
VMEM_LIMIT = 56 * 1024 * 1024


def _cparams(*sem):
    return pltpu.CompilerParams(dimension_semantics=sem, vmem_limit_bytes=VMEM_LIMIT)


def _const_spec(shape):
    nd = len(shape)
    return pl.BlockSpec(shape, lambda *_: (0,) * nd, pipeline_mode=pl.Buffered(1))


def _rms_adaln(x, g, shift, scale):
    ms = jnp.mean(x * x, axis=-1, keepdims=True)
    return (x * lax.rsqrt(ms + EPS) * g) * (1.0 + scale) + shift


def _mod_kernel(c_ref, w_ref, b_ref, o_ref):
    c = c_ref[...]
    sc = c * jax.nn.sigmoid(c)
    o_ref[...] = jnp.dot(sc, w_ref[...], precision=lax.Precision.HIGHEST,
                         preferred_element_type=F32) + b_ref[...]


def _mod_call(c_rows, w_mod, b_mod):
    rows, d = c_rows.shape
    n = w_mod.shape[1]
    bn = d
    return pl.pallas_call(
        _mod_kernel,
        grid=(n // bn,),
        in_specs=[pl.BlockSpec((rows, d), lambda j: (0, 0)),
                  pl.BlockSpec((d, bn), lambda j: (0, j)),
                  pl.BlockSpec((1, bn), lambda j: (0, j))],
        out_specs=pl.BlockSpec((rows, bn), lambda j: (0, j)),
        out_shape=jax.ShapeDtypeStruct((rows, n), F32),
        name="mod",
        compiler_params=_cparams("arbitrary"),
    )(c_rows, w_mod, b_mod.reshape(1, n))


def _mixer_out(x, mod_ref, u_ref, yf_ref, yb_ref, a_ref, d_ref, wg_ref, bg_ref, wos_ref, woa_ref):
    y = d_ref[...] * u_ref[...] + yf_ref[...] + yb_ref[...]
    g = jax.nn.gelu(y)
    z = jnp.dot(g.astype(BF16), wg_ref[...], preferred_element_type=F32) + bg_ref[...]
    s = g * jax.nn.sigmoid(z)
    out = (jnp.dot(s.astype(BF16), wos_ref[...], preferred_element_type=F32)
           + jnp.dot(a_ref[...], woa_ref[...], preferred_element_type=F32))
    return x + mod_ref[5:6, :] * out


def _ffn_kernel(x_ref, mod_ref, *rest, mod_i, d_ff, fc, final, mixer):
    x = x_ref[...]
    if mixer:
        x = _mixer_out(x, mod_ref, *rest[:9])
        rest = rest[9:]
    g_ref, win_ref, wout_ref = rest[:3]
    rest = rest[3:]
    if final:
        fg_ref, o_ref, acc_ref = rest
    else:
        o_ref, acc_ref = rest
    shift = mod_ref[3 * mod_i:3 * mod_i + 1, :]
    scale = mod_ref[3 * mod_i + 1:3 * mod_i + 2, :]
    gate = mod_ref[3 * mod_i + 2:3 * mod_i + 3, :]
    h = _rms_adaln(x, g_ref[...], shift, scale).astype(BF16)
    for j in range(d_ff // fc):
        gt = jnp.dot(h, win_ref[:, j * fc:(j + 1) * fc], preferred_element_type=F32)
        up = jnp.dot(h, win_ref[:, d_ff + j * fc:d_ff + (j + 1) * fc], preferred_element_type=F32)
        act = (gt * jax.nn.sigmoid(gt) * up).astype(BF16)
        part = jnp.dot(act, wout_ref[j * fc:(j + 1) * fc, :], preferred_element_type=F32)
        if j == 0:
            acc_ref[...] = part
        else:
            acc_ref[...] += part
    y = x + (0.5 * gate) * acc_ref[...]
    if final:
        ms = jnp.mean(y * y, axis=-1, keepdims=True)
        y = y * lax.rsqrt(ms + EPS) * fg_ref[...]
    o_ref[...] = y


def _ffn_call(x, mod, g, w_in, w_out, *, mod_i, mod_row, tm, final_g=None, mixer=None):
    b, t, d = x.shape
    d_ff = w_out.shape[0]
    fc = MXU_DIM
    assert t % tm == 0 and d_ff % fc == 0
    final = final_g is not None
    in_specs = [pl.BlockSpec((None, tm, d), lambda bi, ti: (bi, ti, 0)),
                pl.BlockSpec((None, N_MOD, d), lambda bi, ti: (mod_row(bi), 0, 0))]
    args = [x, mod]
    if mixer is not None:
        u, yf, yb, a = mixer[:4]
        w = a.shape[2]
        tmajor = pl.BlockSpec((tm, w), lambda bi, ti: (ti, bi))
        in_specs += [tmajor, tmajor, tmajor, pl.BlockSpec((None, tm, w), lambda bi, ti: (bi, ti, 0))]
        in_specs += [_const_spec(p.shape) for p in mixer[4:]]
        args += list(mixer)
    in_specs += [_const_spec((1, d)), _const_spec(w_in.shape), _const_spec(w_out.shape)]
    args += [g, w_in, w_out]
    if final:
        in_specs.append(_const_spec((1, d)))
        args.append(final_g)
    return pl.pallas_call(
        functools.partial(_ffn_kernel, mod_i=mod_i, d_ff=d_ff, fc=fc, final=final,
                          mixer=mixer is not None),
        grid=(b, t // tm),
        in_specs=in_specs,
        out_specs=pl.BlockSpec((None, tm, d), lambda bi, ti: (bi, ti, 0)),
        out_shape=jax.ShapeDtypeStruct((b, t, d), F32),
        scratch_shapes=[pltpu.VMEM((tm, d), F32)],
        name="ffn_final" if final else "ffn",
        compiler_params=_cparams("parallel", "parallel"),
    )(*args)


def _rope(t, cos, sin_signed, first_half):
    outs = []
    for j in range(t.shape[1] // LANES):
        tj = t[:, j * LANES:(j + 1) * LANES]
        partner = jnp.where(first_half, pltpu.roll(tj, LANES - 16, 1), pltpu.roll(tj, 16, 1))
        outs.append(tj * cos + partner * sin_signed)
    return jnp.concatenate(outs, axis=1)


def _tile_max_norm2(xb, indicator, out_shape):
    xf = xb.astype(F32)
    norm2 = jnp.dot((xf * xf).astype(BF16), indicator, preferred_element_type=F32)
    return jnp.broadcast_to(jnp.max(norm2, axis=0, keepdims=True), out_shape)


def _inproj_kernel(x_ref, mod_ref, g_ref, w_ref, ind_ref, *rest, rope, width, q_scale):
    if rope:
        cos_ref, sin_ref, u_ref, q_ref, k_ref, v_ref, kn_ref, qn_ref = rest
    else:
        u_ref, k_ref, v_ref, kn_ref = rest
    h = _rms_adaln(x_ref[...], g_ref[...], mod_ref[3:4, :], mod_ref[4:5, :]).astype(BF16)
    u_ref[...] = jnp.dot(h, w_ref[:, 0:width], preferred_element_type=F32)
    k = jnp.dot(h, w_ref[:, 2 * width:3 * width], preferred_element_type=F32)
    if rope:
        q = jnp.dot(h, w_ref[:, width:2 * width], preferred_element_type=F32)
        cos = cos_ref[...]
        sin_signed = sin_ref[...]
        lane = lax.broadcasted_iota(jnp.int32, (1, LANES), 1)
        first_half = (lane % 32) < 16
        qb = (_rope(q, cos, sin_signed, first_half) * q_scale).astype(BF16)
        q_ref[...] = qb
        qn_ref[...] = _tile_max_norm2(qb, ind_ref[...], qn_ref.shape)
        k = _rope(k, cos, sin_signed, first_half)
    kb = k.astype(BF16)
    k_ref[...] = kb
    kn_ref[...] = _tile_max_norm2(kb, ind_ref[...], kn_ref.shape)
    v = jnp.dot(h, w_ref[:, 3 * width:4 * width], preferred_element_type=F32).astype(BF16)
    ones = jnp.ones((v.shape[0], LANES), BF16)
    for hd in range(width // LANES):
        v_ref[:, 2 * hd * LANES:(2 * hd + 1) * LANES] = v[:, hd * LANES:(hd + 1) * LANES]
        v_ref[:, (2 * hd + 1) * LANES:(2 * hd + 2) * LANES] = ones


def _inproj_call(x, mod, g, w, *, mod_row, tm, head_dim, rope_tabs=None, q_scale=1.0):
    b, t, d = x.shape
    width = w.shape[1] // 4
    assert t % tm == 0
    rope = rope_tabs is not None
    tok = lambda bi, ti: (bi, ti, 0)
    segment = jnp.arange(width, dtype=jnp.int32)[:, None] // head_dim
    indicator = (segment == jnp.arange(LANES, dtype=jnp.int32)[None, :]).astype(BF16)
    in_specs = [pl.BlockSpec((None, tm, d), tok),
                pl.BlockSpec((None, N_MOD, d), lambda bi, ti: (mod_row(bi), 0, 0)),
                _const_spec((1, d)), _const_spec(w.shape), _const_spec(indicator.shape)]
    args = [x, mod, g, w, indicator]
    out_specs = [pl.BlockSpec((tm, width), lambda bi, ti: (ti, bi))]
    out_shape = [jax.ShapeDtypeStruct((t, b * width), F32)]
    n_qk = 1
    if rope:
        in_specs += [pl.BlockSpec((tm, LANES), lambda bi, ti: (ti, 0))] * 2
        args += list(rope_tabs)
        n_qk = 2
    out_specs += [pl.BlockSpec((None, tm, width), tok)] * n_qk
    out_shape += [jax.ShapeDtypeStruct((b, t, width), BF16)] * n_qk
    out_specs.append(pl.BlockSpec((None, tm, 2 * width), tok))
    out_shape.append(jax.ShapeDtypeStruct((b, t, 2 * width), BF16))
    for _ in range(n_qk):
        out_specs.append(pl.BlockSpec((None, None, SUBLANES, LANES), lambda bi, ti: (bi, ti, 0, 0)))
        out_shape.append(jax.ShapeDtypeStruct((b, t // tm, SUBLANES, LANES), F32))
    return pl.pallas_call(
        functools.partial(_inproj_kernel, rope=rope, width=width, q_scale=q_scale),
        grid=(b, t // tm),
        in_specs=in_specs, out_specs=out_specs, out_shape=out_shape,
        name="inproj_rope" if rope else "inproj",
        compiler_params=_cparams("parallel", "parallel"),
    )(*args)


def _rope_tables(length, head_dim):
    n_freq = head_dim // 4
    pos = jnp.arange(length, dtype=jnp.int32)
    row = (pos // GRID_W).astype(F32)
    col = (pos % GRID_W).astype(F32)
    inv_freq = ROPE_BASE ** (-jnp.arange(n_freq, dtype=F32) / n_freq)
    ang = jnp.stack([row[:, None] * inv_freq, col[:, None] * inv_freq], axis=1)
    cos = jnp.broadcast_to(jnp.cos(ang)[:, :, None, :], (length, 2, 2, n_freq))
    sign = jnp.array([-1.0, 1.0], F32)[None, None, :, None]
    sin = jnp.sin(ang)[:, :, None, :] * sign
    reps = LANES // head_dim
    cos = jnp.tile(cos.reshape(length, head_dim), (1, reps))
    sin = jnp.tile(jnp.broadcast_to(sin, (length, 2, 2, n_freq)).reshape(length, head_dim), (1, reps))
    return cos, sin


def _s5_prep_kernel(are_ref, aim_ref, ldt_ref, bre_ref, bim_ref, tab_ref, obre_ref, obim_ref):
    n_dir = are_ref.shape[0]
    n_state = are_ref.shape[1]
    for d in range(n_dir):
        a_re = are_ref[d:d + 1, :]
        a_im = aim_ref[d:d + 1, :]
        dt = jnp.exp(ldt_ref[d:d + 1, :])
        mag = jnp.exp(dt * a_re)
        abar_re = mag * jnp.cos(dt * a_im)
        abar_im = mag * jnp.sin(dt * a_im)
        zr = abar_re - 1.0
        zi = abar_im
        den = a_re * a_re + a_im * a_im
        coef_re = (zr * a_re + zi * a_im) / den
        coef_im = (zi * a_re - zr * a_im) / den
        b_re = bre_ref[d]
        b_im = bim_ref[d]
        obre_ref[d] = coef_re * b_re - coef_im * b_im
        obim_ref[d] = coef_re * b_im + coef_im * b_re
        tab_ref[d, 0] = jnp.broadcast_to(abar_re, (SUBLANES, n_state))
        tab_ref[d, 1] = jnp.broadcast_to(abar_im, (SUBLANES, n_state))


def _s5_prep_call(a_re, a_im, log_dt, b_re_t, b_im_t):
    n_dir, n_state = a_re.shape
    h = b_re_t.shape[1]
    return pl.pallas_call(
        _s5_prep_kernel,
        out_shape=[jax.ShapeDtypeStruct((n_dir, 2, SUBLANES, n_state), F32),
                   jax.ShapeDtypeStruct((n_dir, h, n_state), F32),
                   jax.ShapeDtypeStruct((n_dir, h, n_state), F32)],
        name="s5_prep",
    )(a_re, a_im, log_dt, b_re_t, b_im_t)


S5_CHUNKS_PER_PASS = 8


def _s5_kernel(uc_ref, ul_ref, bre_ref, bim_ref, cre_ref, cim_ref, tab_ref, y_ref,
               bu0re, bu0im, bu1re, bu1im, h0re, h0im, h1re, h1im, carry, *, nc, tt, reverse):
    s = pl.program_id(0)
    n_in = bre_ref.shape[0]
    n_state = bre_ref.shape[1]
    nb = uc_ref.shape[1]
    rows_t = tt * nb
    kblocks = n_in // MXU_DIM
    sw = n_state // kblocks

    @pl.when(s == 0)
    def _():
        carry[...] = jnp.zeros_like(carry)
        for buf in (bu1re, bu1im, h0re, h0im):
            buf[...] = jnp.zeros_like(buf)

    def stage_bu(dst_re, dst_im):
        u = jnp.where(s < nc, uc_ref[...], ul_ref[...]).reshape(rows_t, n_in).astype(BF16)
        for kb in range(kblocks):
            ub = u[:, kb * MXU_DIM:(kb + 1) * MXU_DIM]
            rows = slice(kb * MXU_DIM, (kb + 1) * MXU_DIM)
            cols = slice(kb * sw, (kb + 1) * sw)
            dst_re[:, cols] = jnp.dot(ub, bre_ref[rows, cols], preferred_element_type=F32)
            dst_im[:, cols] = jnp.dot(ub, bim_ref[rows, cols], preferred_element_type=F32)

    def stage_scan(src_re, src_im, dst_re, dst_im):
        ncp = S5_CHUNKS_PER_PASS
        for lc0 in range(0, n_state // LANES, ncp):
            cols = [slice((lc0 + c) * LANES, (lc0 + c + 1) * LANES) for c in range(ncp)]
            a_re = [tab_ref[0, :, cs] for cs in cols]
            a_im = [tab_ref[1, :, cs] for cs in cols]
            state = [(carry[0, :, cs], carry[1, :, cs]) for cs in cols]
            for t0 in range(0, tt, 2):
                toks = (tt - 1 - t0, tt - 2 - t0) if reverse else (t0, t0 + 1)
                pair = slice(min(toks) * nb, (min(toks) + 2) * nb)
                for c, cs in enumerate(cols):
                    hr, hm = state[c]
                    out = {}
                    for tok in toks:
                        rows = slice(tok * nb, (tok + 1) * nb)
                        hr, hm = (a_re[c] * hr - a_im[c] * hm + src_re[rows, cs],
                                  a_re[c] * hm + a_im[c] * hr + src_im[rows, cs])
                        out[tok] = (hr, hm)
                    lo, hi = out[min(toks)], out[max(toks)]
                    dst_re[pair, cs] = jnp.concatenate([lo[0], hi[0]], axis=0).astype(BF16)
                    dst_im[pair, cs] = jnp.concatenate([lo[1], hi[1]], axis=0).astype(BF16)
                    state[c] = (hr, hm)
            for c, cs in enumerate(cols):
                carry[0, :, cs] = state[c][0]
                carry[1, :, cs] = state[c][1]

    def stage_readout(src_re, src_im):
        n_out = cre_ref.shape[1]
        nblocks = n_out // MXU_DIM
        rw = n_state // nblocks
        for ob in range(nblocks):
            rows = slice(ob * rw, (ob + 1) * rw)
            cols = slice(ob * MXU_DIM, (ob + 1) * MXU_DIM)
            y = (jnp.dot(src_re[:, rows], cre_ref[rows, cols], preferred_element_type=F32)
                 - jnp.dot(src_im[:, rows], cim_ref[rows, cols], preferred_element_type=F32))
            y_ref[:, :, cols] = y.reshape(tt, nb, MXU_DIM)

    @pl.when(s % 2 == 0)
    def _():
        stage_bu(bu0re, bu0im)
        stage_scan(bu1re, bu1im, h1re, h1im)
        stage_readout(h0re, h0im)

    @pl.when(s % 2 == 1)
    def _():
        stage_bu(bu1re, bu1im)
        stage_scan(bu0re, bu0im, h0re, h0im)
        stage_readout(h1re, h1im)


def _s5_call(u_ctx, u_lat, b_re, b_im, c_re, c_im, tab, *, reverse, tt):
    lc, b, w = u_ctx.shape
    l = u_lat.shape[0]
    n_state = b_re.shape[1]
    assert b == SUBLANES, "the scan keeps one batch row per sublane"
    assert lc % tt == 0 and l % tt == 0 and w % MXU_DIM == 0
    assert n_state % (LANES * S5_CHUNKS_PER_PASS) == 0
    nc, nl = lc // tt, l // tt

    def ctx_tile(s):
        j = jnp.minimum(s, nc - 1)
        return nc - 1 - j if reverse else j

    def lat_tile(s):
        j = jnp.clip(s - nc, 0, nl - 1)
        return nl - 1 - j if reverse else j

    bu_tile = pltpu.VMEM((tt * b, n_state), F32)
    h_tile = pltpu.VMEM((tt * b, n_state), BF16)
    return pl.pallas_call(
        functools.partial(_s5_kernel, nc=nc, tt=tt, reverse=reverse),
        grid=(nc + nl + 2,),
        in_specs=[pl.BlockSpec((tt, b, w), lambda s: (ctx_tile(s), 0, 0)),
                  pl.BlockSpec((tt, b, w), lambda s: (lat_tile(s), 0, 0)),
                  _const_spec(b_re.shape), _const_spec(b_im.shape),
                  _const_spec(c_re.shape), _const_spec(c_im.shape),
                  _const_spec(tab.shape)],
        out_specs=pl.BlockSpec((tt, b, w), lambda s: (lat_tile(s - 2), 0, 0)),
        out_shape=jax.ShapeDtypeStruct((l, b, w), F32),
        scratch_shapes=[bu_tile] * 4 + [h_tile] * 4 + [pltpu.VMEM((2, SUBLANES, n_state), F32)],
        name="s5_bwd" if reverse else "s5_fwd",
        compiler_params=_cparams("arbitrary"),
    )(u_ctx, u_lat, b_re, b_im, c_re, c_im, tab)


_NT = (((1,), (1,)), ((), ()))
ATTN_BOUND_LIMIT = 48.0
ATTN_BOUND_MARGIN = 1.02


def _attn_kernel(bound_ref, q_ref, kl_ref, vl_ref, kc_ref, vc_ref, lq_ref, lk_ref, sg_ref, o_ref,
                 qs, m_s, acc, *, tq, tk, rc, lam_init):
    hw = q_ref.shape[1]
    half = hw // 2
    q = q_ref[...]
    lane = lax.broadcasted_iota(jnp.int32, q.shape, 1)
    zero = jnp.zeros_like(q)
    qs[0:tq, :] = jnp.where(lane < half, q, zero)
    qs[tq:2 * tq, :] = jnp.where(lane >= half, q, zero)
    chunks = [slice(c * rc, (c + 1) * rc) for c in range(2 * tq // rc)]
    blocks = [(kc_ref, vc_ref, 0, kc_ref.shape[0])]
    blocks += [(kl_ref, vl_ref, j * tk, tk) for j in range(kl_ref.shape[0] // tk)]

    base = ((pl.program_id(0) * pl.num_programs(1) + pl.program_id(1)) * pl.num_programs(2)
            + pl.program_id(2)) * 2
    bounds = [bound_ref[base], bound_ref[base + 1]]
    bound_ok = jnp.maximum(bounds[0], bounds[1]) <= ATTN_BOUND_LIMIT

    @pl.when(bound_ok)
    def _():
        for rows in chunks:
            m = bounds[rows.start // tq]
            total = None
            for k_ref, v_ref, off, width in blocks:
                s = lax.dot_general(qs[rows, :], k_ref[off:off + width, :], _NT,
                                    preferred_element_type=F32)
                p = jnp.exp2(s - m).astype(BF16)
                pv = jnp.dot(p, v_ref[off:off + width, :], preferred_element_type=F32)
                total = pv if total is None else total + pv
            acc[rows, :] = total

    @pl.when(jnp.logical_not(bound_ok))
    def _():
        for rows in chunks:
            s = lax.dot_general(qs[rows, :], kc_ref[...], _NT, preferred_element_type=F32)
            m = jnp.max(s, axis=-1, keepdims=True)
            p = jnp.exp2((s - m).astype(BF16))
            acc[rows, :] = jnp.dot(p, vc_ref[...], preferred_element_type=F32)
            m_s[rows, :] = m

        def kv_block(j, carry):
            off = pl.multiple_of(j * tk, tk)
            for rows in chunks:
                s = lax.dot_general(qs[rows, :], kl_ref[pl.ds(off, tk), :], _NT,
                                    preferred_element_type=F32)
                m_prev = m_s[rows, :]
                m_new = jnp.maximum(m_prev, jnp.max(s, axis=-1, keepdims=True))
                alpha = jnp.exp2(m_prev - m_new)
                p = jnp.exp2((s - m_new).astype(BF16))
                acc[rows, :] = alpha * acc[rows, :] + jnp.dot(
                    p, vl_ref[pl.ds(off, tk), :], preferred_element_type=F32)
                m_s[rows, :] = m_new
            return carry

        lax.fori_loop(0, kl_ref.shape[0] // tk, kv_block, 0)

    lq = lq_ref[...]
    lk = lk_ref[...]
    lam = (jnp.exp(jnp.sum(lq[0:1, :] * lk[0:1, :], axis=-1, keepdims=True))
           - jnp.exp(jnp.sum(lq[1:2, :] * lk[1:2, :], axis=-1, keepdims=True)) + lam_init)
    o = (acc[0:tq, 0:hw] / acc[0:tq, hw:2 * hw]
         - lam * (acc[tq:2 * tq, 0:hw] / acc[tq:2 * tq, hw:2 * hw]))
    ms = jnp.mean(o * o, axis=-1, keepdims=True)
    o_ref[...] = ((o * lax.rsqrt(ms + EPS) * sg_ref[...]) * (1.0 - lam_init)).astype(o_ref.dtype)


def _attn_call(bounds, q, k_lat, v_lat, k_ctx, v_ctx, lam_q, lam_k, subln_g, *,
               lam_init, tq, tk, rc):
    b, l, w = q.shape
    lc = k_ctx.shape[1]
    hw = w // DIFF_HEADS
    assert l % tq == 0 and l % tk == 0 and tq % rc == 0 and hw == LANES
    assert bounds.shape == (b * DIFF_HEADS * (l // tq) * 2,)
    qmap = lambda bi, hi, qi, bnd: (bi, qi, hi)
    kvmap = lambda bi, hi, qi, bnd: (bi, 0, hi)
    const = lambda shape: pl.BlockSpec(shape, lambda bi, hi, qi, bnd: (0,) * len(shape),
                                       pipeline_mode=pl.Buffered(1))
    return pl.pallas_call(
        functools.partial(_attn_kernel, tq=tq, tk=tk, rc=rc, lam_init=lam_init),
        grid_spec=pltpu.PrefetchScalarGridSpec(
            num_scalar_prefetch=1,
            grid=(b, DIFF_HEADS, l // tq),
            in_specs=[pl.BlockSpec((None, tq, hw), qmap),
                      pl.BlockSpec((None, l, hw), kvmap), pl.BlockSpec((None, l, 2 * hw), kvmap),
                      pl.BlockSpec((None, lc, hw), kvmap), pl.BlockSpec((None, lc, 2 * hw), kvmap),
                      const(lam_q.shape), const(lam_k.shape), const(subln_g.shape)],
            out_specs=pl.BlockSpec((None, tq, hw), qmap),
            scratch_shapes=[pltpu.VMEM((2 * tq, hw), BF16), pltpu.VMEM((2 * tq, 1), F32),
                            pltpu.VMEM((2 * tq, 2 * hw), F32)]),
        out_shape=jax.ShapeDtypeStruct((b, l, w), BF16),
        name="diff_attn",
        compiler_params=_cparams("parallel", "parallel", "parallel"),
    )(bounds, q, k_lat, v_lat, k_ctx, v_ctx, lam_q, lam_k, subln_g)


def _block_diag(blocks):
    g, r, c = blocks.shape
    eye = jnp.eye(g, dtype=blocks.dtype)
    return (eye[:, None, :, None] * blocks[:, :, None, :]).reshape(g * r, g * c)


def kernel(x, c, ctx, c_ctx, w_mod, b_mod, norm_g, ffn_w_in, ffn_w_out, w_in, w_out, ssm_a_re, ssm_a_im, ssm_log_dt, ssm_b_re, ssm_b_im, ssm_c_re, ssm_c_im, ssm_d, w_glu, b_glu, lam_q, lam_k, subln_g, final_g):
    b, l, d = x.shape
    lc = ctx.shape[1]
    depth = w_mod.shape[0]
    assert depth == 1, "context-stream update between layers is not implemented"
    n_groups, n_state_g, n_in_g = ssm_b_re.shape[2:]
    ssm_w = n_groups * n_in_g
    n_state = n_groups * n_state_g
    head_dim = lam_q.shape[-1]
    ctx_row = b
    rows = SUBLANES * (-(-(b + 1) // SUBLANES))

    tm_lat = 512 if l % 512 == 0 else 256
    tm_ctx = 256
    tt_s5 = 64
    tq = 1024 if l % 1024 == 0 else 256
    tk = 1024 if l % 1024 == 0 else 256
    rc = 512

    layer = 0
    lam_init = 0.8 - 0.6 * math.exp(-0.3 * layer)

    c_rows = jnp.zeros((rows, d), F32).at[:b].set(c).at[b].set(c_ctx)
    mod = _mod_call(c_rows, w_mod[layer], b_mod[layer]).reshape(rows, N_MOD, d)
    lat_row = lambda bi: bi
    ctx_mod_row = lambda bi: ctx_row

    g0, g1, g2 = (norm_g[layer, i].reshape(1, d) for i in range(3))
    w1_in, w1_out = ffn_w_in[layer, 0].astype(BF16), ffn_w_out[layer, 0].astype(BF16)
    w2_in, w2_out = ffn_w_in[layer, 1].astype(BF16), ffn_w_out[layer, 1].astype(BF16)
    w_mix = w_in[layer].astype(BF16)

    x1 = _ffn_call(x, mod, g0, w1_in, w1_out, mod_i=0, mod_row=lat_row, tm=tm_lat)
    ctx1 = _ffn_call(ctx, mod, g0, w1_in, w1_out, mod_i=0, mod_row=ctx_mod_row, tm=tm_ctx)

    tabs = _rope_tables(l, head_dim)
    u_lat, q_lat, k_lat, v_lat, kn_lat, qn_lat = _inproj_call(
        x1, mod, g1, w_mix, mod_row=lat_row, tm=tm_lat, head_dim=head_dim, rope_tabs=tabs,
        q_scale=head_dim ** -0.5 * math.log2(math.e))
    u_ctx, k_ctx, v_ctx, kn_ctx = _inproj_call(ctx1, mod, g1, w_mix, mod_row=ctx_mod_row,
                                               tm=tm_ctx, head_dim=head_dim)

    flat = lambda p: p[layer].reshape(2, n_state)
    log_dt = jnp.repeat(ssm_log_dt[layer], n_state_g, axis=-1)
    b_t = lambda p: p[layer].transpose(0, 3, 1, 2).reshape(2, n_in_g, n_state)
    tab, bbar_re, bbar_im = _s5_prep_call(flat(ssm_a_re), flat(ssm_a_im), log_dt,
                                          b_t(ssm_b_re), b_t(ssm_b_im))
    ys = []
    for direction in range(2):
        to_blocks = lambda m: _block_diag(
            m[direction].reshape(n_in_g, n_groups, n_state_g).transpose(1, 0, 2)).astype(BF16)
        c_blocks = lambda p: _block_diag(p[layer, direction].transpose(0, 2, 1)).astype(BF16)
        y_dir = _s5_call(u_ctx.reshape(lc, b, ssm_w), u_lat.reshape(l, b, ssm_w),
                         to_blocks(bbar_re), to_blocks(bbar_im),
                         c_blocks(ssm_c_re), c_blocks(ssm_c_im), tab[direction],
                         reverse=direction == 1, tt=tt_s5)
        ys.append(y_dir.reshape(l, b * ssm_w))

    n_hc = 2 * DIFF_HEADS
    kn2 = jnp.maximum(kn_lat.max(axis=(1, 2)), kn_ctx.max(axis=(1, 2)))[:, None, :n_hc]
    qn2 = qn_lat[:, :, 0, :n_hc]
    if tq >= tm_lat:
        qn2 = qn2.reshape(b, l // tq, tq // tm_lat, n_hc).max(axis=2)
    else:
        qn2 = jnp.repeat(qn2, tm_lat // tq, axis=1)
    bounds = jnp.sqrt(qn2 * kn2) * ATTN_BOUND_MARGIN
    bounds = bounds.reshape(b, l // tq, DIFF_HEADS, 2).transpose(0, 2, 1, 3).reshape(-1)
    a_lat = _attn_call(bounds, q_lat, k_lat, v_lat, k_ctx, v_ctx, lam_q[layer], lam_k[layer],
                       subln_g[layer].reshape(1, -1), lam_init=lam_init, tq=tq, tk=tk, rc=rc)

    w_o = w_out[layer].astype(BF16)
    mixer = (u_lat, ys[0], ys[1], a_lat, ssm_d[layer].reshape(1, ssm_w),
             w_glu[layer].astype(BF16), b_glu[layer].reshape(1, ssm_w), w_o[:ssm_w], w_o[ssm_w:])
    return _ffn_call(x1, mod, g2, w2_in, w2_out, mod_i=2, mod_row=lat_row, tm=tm_lat,
                     final_g=final_g.reshape(1, d), mixer=mixer)
```

```python
import functools
import math

import jax
import jax.numpy as jnp
from jax import lax
from jax.experimental import pallas as pl
from jax.experimental.pallas import tpu as pltpu

F32 = jnp.float32
BF16 = jnp.bfloat16

EPS = 1e-6
N_MOD = 9
GRID_W = 64
ROPE_BASE = 10000.0
DIFF_HEADS = 4
SUBLANES = 8
LANES = 128
MXU_DIM = 256
VMEM_LIMIT = 56 * 1024 * 1024


def _cparams(*sem):
    return pltpu.CompilerParams(dimension_semantics=sem, vmem_limit_bytes=VMEM_LIMIT)


def _const_spec(shape):
    nd = len(shape)
    return pl.BlockSpec(shape, lambda *_: (0,) * nd, pipeline_mode=pl.Buffered(1))


def _rms_adaln(x, g, shift, scale):
    ms = jnp.mean(x * x, axis=-1, keepdims=True)
    return (x * lax.rsqrt(ms + EPS) * g) * (1.0 + scale) + shift


def _mod_kernel(c_ref, w_ref, b_ref, o_ref):
    c = c_ref[...]
    sc = c * jax.nn.sigmoid(c)
    o_ref[...] = jnp.dot(sc, w_ref[...], precision=lax.Precision.HIGHEST,
                         preferred_element_type=F32) + b_ref[...]


def _mod_call(c_rows, w_mod, b_mod):
    rows, d = c_rows.shape
    n = w_mod.shape[1]
    bn = d
    return pl.pallas_call(
        _mod_kernel,
        grid=(n // bn,),
        in_specs=[pl.BlockSpec((rows, d), lambda j: (0, 0)),
                  pl.BlockSpec((d, bn), lambda j: (0, j)),
                  pl.BlockSpec((1, bn), lambda j: (0, j))],
        out_specs=pl.BlockSpec((rows, bn), lambda j: (0, j)),
        out_shape=jax.ShapeDtypeStruct((rows, n), F32),
        name="mod",
        compiler_params=_cparams("arbitrary"),
    )(c_rows, w_mod, b_mod.reshape(1, n))


def _mixer_out(x, mod_ref, u_ref, yf_ref, yb_ref, a_ref, d_ref, wg_ref, bg_ref, wos_ref, woa_ref):
    y = d_ref[...] * u_ref[...] + yf_ref[...] + yb_ref[...]
    g = jax.nn.gelu(y)
    z = jnp.dot(g.astype(BF16), wg_ref[...], preferred_element_type=F32) + bg_ref[...]
    s = g * jax.nn.sigmoid(z)
    out = (jnp.dot(s.astype(BF16), wos_ref[...], preferred_element_type=F32)
           + jnp.dot(a_ref[...], woa_ref[...], preferred_element_type=F32))
    return x + mod_ref[5:6, :] * out


def _ffn_kernel(x_ref, mod_ref, *rest, mod_i, d_ff, fc, final, mixer):
    x = x_ref[...]
    if mixer:
        x = _mixer_out(x, mod_ref, *rest[:9])
        rest = rest[9:]
    g_ref, win_ref, wout_ref = rest[:3]
    rest = rest[3:]
    if final:
        fg_ref, o_ref, acc_ref = rest
    else:
        o_ref, acc_ref = rest
    shift = mod_ref[3 * mod_i:3 * mod_i + 1, :]
    scale = mod_ref[3 * mod_i + 1:3 * mod_i + 2, :]
    gate = mod_ref[3 * mod_i + 2:3 * mod_i + 3, :]
    h = _rms_adaln(x, g_ref[...], shift, scale).astype(BF16)
    for j in range(d_ff // fc):
        gt = jnp.dot(h, win_ref[:, j * fc:(j + 1) * fc], preferred_element_type=F32)
        up = jnp.dot(h, win_ref[:, d_ff + j * fc:d_ff + (j + 1) * fc], preferred_element_type=F32)
        act = (gt * jax.nn.sigmoid(gt) * up).astype(BF16)
        part = jnp.dot(act, wout_ref[j * fc:(j + 1) * fc, :], preferred_element_type=F32)
        if j == 0:
            acc_ref[...] = part
        else:
            acc_ref[...] += part
    y = x + (0.5 * gate) * acc_ref[...]
    if final:
        ms = jnp.mean(y * y, axis=-1, keepdims=True)
        y = y * lax.rsqrt(ms + EPS) * fg_ref[...]
    o_ref[...] = y


def _ffn_call(x, mod, g, w_in, w_out, *, mod_i, mod_row, tm, final_g=None, mixer=None):
    b, t, d = x.shape
    d_ff = w_out.shape[0]
    fc = MXU_DIM
    assert t % tm == 0 and d_ff % fc == 0
    final = final_g is not None
    in_specs = [pl.BlockSpec((None, tm, d), lambda bi, ti: (bi, ti, 0)),
                pl.BlockSpec((None, N_MOD, d), lambda bi, ti: (mod_row(bi), 0, 0))]
    args = [x, mod]
    if mixer is not None:
        u, yf, yb, a = mixer[:4]
        w = a.shape[2]
        tmajor = pl.BlockSpec((tm, w), lambda bi, ti: (ti, bi))
        in_specs += [tmajor, tmajor, tmajor, pl.BlockSpec((None, tm, w), lambda bi, ti: (bi, ti, 0))]
        in_specs += [_const_spec(p.shape) for p in mixer[4:]]
        args += list(mixer)
    in_specs += [_const_spec((1, d)), _const_spec(w_in.shape), _const_spec(w_out.shape)]
    args += [g, w_in, w_out]
    if final:
        in_specs.append(_const_spec((1, d)))
        args.append(final_g)
    return pl.pallas_call(
        functools.partial(_ffn_kernel, mod_i=mod_i, d_ff=d_ff, fc=fc, final=final,
                          mixer=mixer is not None),
        grid=(b, t // tm),
        in_specs=in_specs,
        out_specs=pl.BlockSpec((None, tm, d), lambda bi, ti: (bi, ti, 0)),
        out_shape=jax.ShapeDtypeStruct((b, t, d), F32),
        scratch_shapes=[pltpu.VMEM((tm, d), F32)],
        name="ffn_final" if final else "ffn",
        compiler_params=_cparams("parallel", "parallel"),
    )(*args)


def _rope(t, cos, sin_signed, first_half):
    outs = []
    for j in range(t.shape[1] // LANES):
        tj = t[:, j * LANES:(j + 1) * LANES]
        partner = jnp.where(first_half, pltpu.roll(tj, LANES - 16, 1), pltpu.roll(tj, 16, 1))
        outs.append(tj * cos + partner * sin_signed)
    return jnp.concatenate(outs, axis=1)


def _tile_max_norm2(xb, indicator, out_shape):
    xf = xb.astype(F32)
    norm2 = jnp.dot((xf * xf).astype(BF16), indicator, preferred_element_type=F32)
    return jnp.broadcast_to(jnp.max(norm2, axis=0, keepdims=True), out_shape)


def _inproj_kernel(x_ref, mod_ref, g_ref, w_ref, ind_ref, *rest, rope, width, q_scale):
    if rope:
        cos_ref, sin_ref, u_ref, q_ref, k_ref, v_ref, kn_ref, qn_ref = rest
    else:
        u_ref, k_ref, v_ref, kn_ref = rest
    h = _rms_adaln(x_ref[...], g_ref[...], mod_ref[3:4, :], mod_ref[4:5, :]).astype(BF16)
    u_ref[...] = jnp.dot(h, w_ref[:, 0:width], preferred_element_type=F32)
    k = jnp.dot(h, w_ref[:, 2 * width:3 * width], preferred_element_type=F32)
    if rope:
        q = jnp.dot(h, w_ref[:, width:2 * width], preferred_element_type=F32)
        cos = cos_ref[...]
        sin_signed = sin_ref[...]
        lane = lax.broadcasted_iota(jnp.int32, (1, LANES), 1)
        first_half = (lane % 32) < 16
        qb = (_rope(q, cos, sin_signed, first_half) * q_scale).astype(BF16)
        q_ref[...] = qb
        qn_ref[...] = _tile_max_norm2(qb, ind_ref[...], qn_ref.shape)
        k = _rope(k, cos, sin_signed, first_half)
    kb = k.astype(BF16)
    k_ref[...] = kb
    kn_ref[...] = _tile_max_norm2(kb, ind_ref[...], kn_ref.shape)
    v = jnp.dot(h, w_ref[:, 3 * width:4 * width], preferred_element_type=F32).astype(BF16)
    ones = jnp.ones((v.shape[0], LANES), BF16)
    for hd in range(width // LANES):
        v_ref[:, 2 * hd * LANES:(2 * hd + 1) * LANES] = v[:, hd * LANES:(hd + 1) * LANES]
        v_ref[:, (2 * hd + 1) * LANES:(2 * hd + 2) * LANES] = ones


def _inproj_call(x, mod, g, w, *, mod_row, tm, head_dim, rope_tabs=None, q_scale=1.0):
    b, t, d = x.shape
    width = w.shape[1] // 4
    assert t % tm == 0
    rope = rope_tabs is not None
    tok = lambda bi, ti: (bi, ti, 0)
    segment = jnp.arange(width, dtype=jnp.int32)[:, None] // head_dim
    indicator = (segment == jnp.arange(LANES, dtype=jnp.int32)[None, :]).astype(BF16)
    in_specs = [pl.BlockSpec((None, tm, d), tok),
                pl.BlockSpec((None, N_MOD, d), lambda bi, ti: (mod_row(bi), 0, 0)),
                _const_spec((1, d)), _const_spec(w.shape), _const_spec(indicator.shape)]
    args = [x, mod, g, w, indicator]
    out_specs = [pl.BlockSpec((tm, width), lambda bi, ti: (ti, bi))]
    out_shape = [jax.ShapeDtypeStruct((t, b * width), F32)]
    n_qk = 1
    if rope:
        in_specs += [pl.BlockSpec((tm, LANES), lambda bi, ti: (ti, 0))] * 2
        args += list(rope_tabs)
        n_qk = 2
    out_specs += [pl.BlockSpec((None, tm, width), tok)] * n_qk
    out_shape += [jax.ShapeDtypeStruct((b, t, width), BF16)] * n_qk
    out_specs.append(pl.BlockSpec((None, tm, 2 * width), tok))
    out_shape.append(jax.ShapeDtypeStruct((b, t, 2 * width), BF16))
    for _ in range(n_qk):
        out_specs.append(pl.BlockSpec((None, None, SUBLANES, LANES), lambda bi, ti: (bi, ti, 0, 0)))
        out_shape.append(jax.ShapeDtypeStruct((b, t // tm, SUBLANES, LANES), F32))
    return pl.pallas_call(
        functools.partial(_inproj_kernel, rope=rope, width=width, q_scale=q_scale),
        grid=(b, t // tm),
        in_specs=in_specs, out_specs=out_specs, out_shape=out_shape,
        name="inproj_rope" if rope else "inproj",
        compiler_params=_cparams("parallel", "parallel"),
    )(*args)


def _rope_tables(length, head_dim):
    n_freq = head_dim // 4
    pos = jnp.arange(length, dtype=jnp.int32)
    row = (pos // GRID_W).astype(F32)
    col = (pos % GRID_W).astype(F32)
    inv_freq = ROPE_BASE ** (-jnp.arange(n_freq, dtype=F32) / n_freq)
    ang = jnp.stack([row[:, None] * inv_freq, col[:, None] * inv_freq], axis=1)
    cos = jnp.broadcast_to(jnp.cos(ang)[:, :, None, :], (length, 2, 2, n_freq))
    sign = jnp.array([-1.0, 1.0], F32)[None, None, :, None]
    sin = jnp.sin(ang)[:, :, None, :] * sign
    reps = LANES // head_dim
    cos = jnp.tile(cos.reshape(length, head_dim), (1, reps))
    sin = jnp.tile(jnp.broadcast_to(sin, (length, 2, 2, n_freq)).reshape(length, head_dim), (1, reps))
    return cos, sin


def _s5_prep_kernel(are_ref, aim_ref, ldt_ref, bre_ref, bim_ref, tab_ref, obre_ref, obim_ref):
    n_dir = are_ref.shape[0]
    n_state = are_ref.shape[1]
    for d in range(n_dir):
        a_re = are_ref[d:d + 1, :]
        a_im = aim_ref[d:d + 1, :]
        dt = jnp.exp(ldt_ref[d:d + 1, :])
        mag = jnp.exp(dt * a_re)
        abar_re = mag * jnp.cos(dt * a_im)
        abar_im = mag * jnp.sin(dt * a_im)
        zr = abar_re - 1.0
        zi = abar_im
        den = a_re * a_re + a_im * a_im
        coef_re = (zr * a_re + zi * a_im) / den
        coef_im = (zi * a_re - zr * a_im) / den
        b_re = bre_ref[d]
        b_im = bim_ref[d]
        obre_ref[d] = coef_re * b_re - coef_im * b_im
        obim_ref[d] = coef_re * b_im + coef_im * b_re
        tab_ref[d, 0] = jnp.broadcast_to(abar_re, (SUBLANES, n_state))
        tab_ref[d, 1] = jnp.broadcast_to(abar_im, (SUBLANES, n_state))


def _s5_prep_call(a_re, a_im, log_dt, b_re_t, b_im_t):
    n_dir, n_state = a_re.shape
    h = b_re_t.shape[1]
    return pl.pallas_call(
        _s5_prep_kernel,
        out_shape=[jax.ShapeDtypeStruct((n_dir, 2, SUBLANES, n_state), F32),
                   jax.ShapeDtypeStruct((n_dir, h, n_state), F32),
                   jax.ShapeDtypeStruct((n_dir, h, n_state), F32)],
        name="s5_prep",
    )(a_re, a_im, log_dt, b_re_t, b_im_t)


S5_CHUNKS_PER_PASS = 8


def _s5_kernel(uc_ref, ul_ref, bre_ref, bim_ref, cre_ref, cim_ref, tab_ref, y_ref,
               bu0re, bu0im, bu1re, bu1im, h0re, h0im, h1re, h1im, carry, *, nc, tt, reverse):
    s = pl.program_id(0)
    n_in = bre_ref.shape[0]
    n_state = bre_ref.shape[1]
    nb = uc_ref.shape[1]
    rows_t = tt * nb
    kblocks = n_in // MXU_DIM
    sw = n_state // kblocks

    @pl.when(s == 0)
    def _():
        carry[...] = jnp.zeros_like(carry)
        for buf in (bu1re, bu1im, h0re, h0im):
            buf[...] = jnp.zeros_like(buf)

    def stage_bu(dst_re, dst_im):
        u = jnp.where(s < nc, uc_ref[...], ul_ref[...]).reshape(rows_t, n_in).astype(BF16)
        for kb in range(kblocks):
            ub = u[:, kb * MXU_DIM:(kb + 1) * MXU_DIM]
            rows = slice(kb * MXU_DIM, (kb + 1) * MXU_DIM)
            cols = slice(kb * sw, (kb + 1) * sw)
            dst_re[:, cols] = jnp.dot(ub, bre_ref[rows, cols], preferred_element_type=F32)
            dst_im[:, cols] = jnp.dot(ub, bim_ref[rows, cols], preferred_element_type=F32)

    def stage_scan(src_re, src_im, dst_re, dst_im):
        ncp = S5_CHUNKS_PER_PASS
        for lc0 in range(0, n_state // LANES, ncp):
            cols = [slice((lc0 + c) * LANES, (lc0 + c + 1) * LANES) for c in range(ncp)]
            a_re = [tab_ref[0, :, cs] for cs in cols]
            a_im = [tab_ref[1, :, cs] for cs in cols]
            state = [(carry[0, :, cs], carry[1, :, cs]) for cs in cols]
            for t0 in range(0, tt, 2):
                toks = (tt - 1 - t0, tt - 2 - t0) if reverse else (t0, t0 + 1)
                pair = slice(min(toks) * nb, (min(toks) + 2) * nb)
                for c, cs in enumerate(cols):
                    hr, hm = state[c]
                    out = {}
                    for tok in toks:
                        rows = slice(tok * nb, (tok + 1) * nb)
                        hr, hm = (a_re[c] * hr - a_im[c] * hm + src_re[rows, cs],
                                  a_re[c] * hm + a_im[c] * hr + src_im[rows, cs])
                        out[tok] = (hr, hm)
                    lo, hi = out[min(toks)], out[max(toks)]
                    dst_re[pair, cs] = jnp.concatenate([lo[0], hi[0]], axis=0).astype(BF16)
                    dst_im[pair, cs] = jnp.concatenate([lo[1], hi[1]], axis=0).astype(BF16)
                    state[c] = (hr, hm)
            for c, cs in enumerate(cols):
                carry[0, :, cs] = state[c][0]
                carry[1, :, cs] = state[c][1]

    def stage_readout(src_re, src_im):
        n_out = cre_ref.shape[1]
        nblocks = n_out // MXU_DIM
        rw = n_state // nblocks
        for ob in range(nblocks):
            rows = slice(ob * rw, (ob + 1) * rw)
            cols = slice(ob * MXU_DIM, (ob + 1) * MXU_DIM)
            y = (jnp.dot(src_re[:, rows], cre_ref[rows, cols], preferred_element_type=F32)
                 - jnp.dot(src_im[:, rows], cim_ref[rows, cols], preferred_element_type=F32))
            y_ref[:, :, cols] = y.reshape(tt, nb, MXU_DIM)

    @pl.when(s % 2 == 0)
    def _():
        stage_bu(bu0re, bu0im)
        stage_scan(bu1re, bu1im, h1re, h1im)
        stage_readout(h0re, h0im)

    @pl.when(s % 2 == 1)
    def _():
        stage_bu(bu1re, bu1im)
        stage_scan(bu0re, bu0im, h0re, h0im)
        stage_readout(h1re, h1im)


def _s5_call(u_ctx, u_lat, b_re, b_im, c_re, c_im, tab, *, reverse, tt):
    lc, b, w = u_ctx.shape
    l = u_lat.shape[0]
    n_state = b_re.shape[1]
    assert b == SUBLANES, "the scan keeps one batch row per sublane"
    assert lc % tt == 0 and l % tt == 0 and w % MXU_DIM == 0
    assert n_state % (LANES * S5_CHUNKS_PER_PASS) == 0
    nc, nl = lc // tt, l // tt

    def ctx_tile(s):
        j = jnp.minimum(s, nc - 1)
        return nc - 1 - j if reverse else j

    def lat_tile(s):
        j = jnp.clip(s - nc, 0, nl - 1)
        return nl - 1 - j if reverse else j

    bu_tile = pltpu.VMEM((tt * b, n_state), F32)
    h_tile = pltpu.VMEM((tt * b, n_state), BF16)
    return pl.pallas_call(
        functools.partial(_s5_kernel, nc=nc, tt=tt, reverse=reverse),
        grid=(nc + nl + 2,),
        in_specs=[pl.BlockSpec((tt, b, w), lambda s: (ctx_tile(s), 0, 0)),
                  pl.BlockSpec((tt, b, w), lambda s: (lat_tile(s), 0, 0)),
                  _const_spec(b_re.shape), _const_spec(b_im.shape),
                  _const_spec(c_re.shape), _const_spec(c_im.shape),
                  _const_spec(tab.shape)],
        out_specs=pl.BlockSpec((tt, b, w), lambda s: (lat_tile(s - 2), 0, 0)),
        out_shape=jax.ShapeDtypeStruct((l, b, w), F32),
        scratch_shapes=[bu_tile] * 4 + [h_tile] * 4 + [pltpu.VMEM((2, SUBLANES, n_state), F32)],
        name="s5_bwd" if reverse else "s5_fwd",
        compiler_params=_cparams("arbitrary"),
    )(u_ctx, u_lat, b_re, b_im, c_re, c_im, tab)


_NT = (((1,), (1,)), ((), ()))
ATTN_BOUND_LIMIT = 48.0
ATTN_BOUND_MARGIN = 1.02


def _attn_kernel(bound_ref, q_ref, kl_ref, vl_ref, kc_ref, vc_ref, lq_ref, lk_ref, sg_ref, o_ref,
                 qs, m_s, acc, *, tq, tk, rc, lam_init):
    hw = q_ref.shape[1]
    half = hw // 2
    q = q_ref[...]
    lane = lax.broadcasted_iota(jnp.int32, q.shape, 1)
    zero = jnp.zeros_like(q)
    qs[0:tq, :] = jnp.where(lane < half, q, zero)
    qs[tq:2 * tq, :] = jnp.where(lane >= half, q, zero)
    chunks = [slice(c * rc, (c + 1) * rc) for c in range(2 * tq // rc)]
    blocks = [(kc_ref, vc_ref, 0, kc_ref.shape[0])]
    blocks += [(kl_ref, vl_ref, j * tk, tk) for j in range(kl_ref.shape[0] // tk)]

    base = ((pl.program_id(0) * pl.num_programs(1) + pl.program_id(1)) * pl.num_programs(2)
            + pl.program_id(2)) * 2
    bounds = [bound_ref[base], bound_ref[base + 1]]
    bound_ok = jnp.maximum(bounds[0], bounds[1]) <= ATTN_BOUND_LIMIT

    @pl.when(bound_ok)
    def _():
        for rows in chunks:
            m = bounds[rows.start // tq]
            total = None
            for k_ref, v_ref, off, width in blocks:
                s = lax.dot_general(qs[rows, :], k_ref[off:off + width, :], _NT,
                                    preferred_element_type=F32)
                p = jnp.exp2(s - m).astype(BF16)
                pv = jnp.dot(p, v_ref[off:off + width, :], preferred_element_type=F32)
                total = pv if total is None else total + pv
            acc[rows, :] = total

    @pl.when(jnp.logical_not(bound_ok))
    def _():
        for rows in chunks:
            s = lax.dot_general(qs[rows, :], kc_ref[...], _NT, preferred_element_type=F32)
            m = jnp.max(s, axis=-1, keepdims=True)
            p = jnp.exp2((s - m).astype(BF16))
            acc[rows, :] = jnp.dot(p, vc_ref[...], preferred_element_type=F32)
            m_s[rows, :] = m

        def kv_block(j, carry):
            off = pl.multiple_of(j * tk, tk)
            for rows in chunks:
                s = lax.dot_general(qs[rows, :], kl_ref[pl.ds(off, tk), :], _NT,
                                    preferred_element_type=F32)
                m_prev = m_s[rows, :]
                m_new = jnp.maximum(m_prev, jnp.max(s, axis=-1, keepdims=True))
                alpha = jnp.exp2(m_prev - m_new)
                p = jnp.exp2((s - m_new).astype(BF16))
                acc[rows, :] = alpha * acc[rows, :] + jnp.dot(
                    p, vl_ref[pl.ds(off, tk), :], preferred_element_type=F32)
                m_s[rows, :] = m_new
            return carry

        lax.fori_loop(0, kl_ref.shape[0] // tk, kv_block, 0)

    lq = lq_ref[...]
    lk = lk_ref[...]
    lam = (jnp.exp(jnp.sum(lq[0:1, :] * lk[0:1, :], axis=-1, keepdims=True))
           - jnp.exp(jnp.sum(lq[1:2, :] * lk[1:2, :], axis=-1, keepdims=True)) + lam_init)
    o = (acc[0:tq, 0:hw] / acc[0:tq, hw:2 * hw]
         - lam * (acc[tq:2 * tq, 0:hw] / acc[tq:2 * tq, hw:2 * hw]))
    ms = jnp.mean(o * o, axis=-1, keepdims=True)
    o_ref[...] = ((o * lax.rsqrt(ms + EPS) * sg_ref[...]) * (1.0 - lam_init)).astype(o_ref.dtype)


def _attn_call(bounds, q, k_lat, v_lat, k_ctx, v_ctx, lam_q, lam_k, subln_g, *,
               lam_init, tq, tk, rc):
    b, l, w = q.shape
    lc = k_ctx.shape[1]
    hw = w // DIFF_HEADS
    assert l % tq == 0 and l % tk == 0 and tq % rc == 0 and hw == LANES
    assert bounds.shape == (b * DIFF_HEADS * (l // tq) * 2,)
    qmap = lambda bi, hi, qi, bnd: (bi, qi, hi)
    kvmap = lambda bi, hi, qi, bnd: (bi, 0, hi)
    const = lambda shape: pl.BlockSpec(shape, lambda bi, hi, qi, bnd: (0,) * len(shape),
                                       pipeline_mode=pl.Buffered(1))
    return pl.pallas_call(
        functools.partial(_attn_kernel, tq=tq, tk=tk, rc=rc, lam_init=lam_init),
        grid_spec=pltpu.PrefetchScalarGridSpec(
            num_scalar_prefetch=1,
            grid=(b, DIFF_HEADS, l // tq),
            in_specs=[pl.BlockSpec((None, tq, hw), qmap),
                      pl.BlockSpec((None, l, hw), kvmap), pl.BlockSpec((None, l, 2 * hw), kvmap),
                      pl.BlockSpec((None, lc, hw), kvmap), pl.BlockSpec((None, lc, 2 * hw), kvmap),
                      const(lam_q.shape), const(lam_k.shape), const(subln_g.shape)],
            out_specs=pl.BlockSpec((None, tq, hw), qmap),
            scratch_shapes=[pltpu.VMEM((2 * tq, hw), BF16), pltpu.VMEM((2 * tq, 1), F32),
                            pltpu.VMEM((2 * tq, 2 * hw), F32)]),
        out_shape=jax.ShapeDtypeStruct((b, l, w), BF16),
        name="diff_attn",
        compiler_params=_cparams("parallel", "parallel", "parallel"),
    )(bounds, q, k_lat, v_lat, k_ctx, v_ctx, lam_q, lam_k, subln_g)


def _block_diag(blocks):
    g, r, c = blocks.shape
    eye = jnp.eye(g, dtype=blocks.dtype)
    return (eye[:, None, :, None] * blocks[:, :, None, :]).reshape(g * r, g * c)


def kernel(x, c, ctx, c_ctx, w_mod, b_mod, norm_g, ffn_w_in, ffn_w_out, w_in, w_out, ssm_a_re, ssm_a_im, ssm_log_dt, ssm_b_re, ssm_b_im, ssm_c_re, ssm_c_im, ssm_d, w_glu, b_glu, lam_q, lam_k, subln_g, final_g):
    b, l, d = x.shape
    lc = ctx.shape[1]
    depth = w_mod.shape[0]
    assert depth == 1, "context-stream update between layers is not implemented"
    n_groups, n_state_g, n_in_g = ssm_b_re.shape[2:]
    ssm_w = n_groups * n_in_g
    n_state = n_groups * n_state_g
    head_dim = lam_q.shape[-1]
    ctx_row = b
    rows = SUBLANES * (-(-(b + 1) // SUBLANES))

    tm_lat = 512 if l % 512 == 0 else 256
    tm_ctx = 256
    tt_s5 = 64
    tq = 1024 if l % 1024 == 0 else 256
    tk = 2048 if l % 2048 == 0 else 256
    rc = 256

    layer = 0
    lam_init = 0.8 - 0.6 * math.exp(-0.3 * layer)

    c_rows = jnp.zeros((rows, d), F32).at[:b].set(c).at[b].set(c_ctx)
    mod = _mod_call(c_rows, w_mod[layer], b_mod[layer]).reshape(rows, N_MOD, d)
    lat_row = lambda bi: bi
    ctx_mod_row = lambda bi: ctx_row

    g0, g1, g2 = (norm_g[layer, i].reshape(1, d) for i in range(3))
    w1_in, w1_out = ffn_w_in[layer, 0].astype(BF16), ffn_w_out[layer, 0].astype(BF16)
    w2_in, w2_out = ffn_w_in[layer, 1].astype(BF16), ffn_w_out[layer, 1].astype(BF16)
    w_mix = w_in[layer].astype(BF16)

    x1 = _ffn_call(x, mod, g0, w1_in, w1_out, mod_i=0, mod_row=lat_row, tm=tm_lat)
    ctx1 = _ffn_call(ctx, mod, g0, w1_in, w1_out, mod_i=0, mod_row=ctx_mod_row, tm=tm_ctx)

    tabs = _rope_tables(l, head_dim)
    u_lat, q_lat, k_lat, v_lat, kn_lat, qn_lat = _inproj_call(
        x1, mod, g1, w_mix, mod_row=lat_row, tm=tm_lat, head_dim=head_dim, rope_tabs=tabs,
        q_scale=head_dim ** -0.5 * math.log2(math.e))
    u_ctx, k_ctx, v_ctx, kn_ctx = _inproj_call(ctx1, mod, g1, w_mix, mod_row=ctx_mod_row,
                                               tm=tm_ctx, head_dim=head_dim)

    flat = lambda p: p[layer].reshape(2, n_state)
    log_dt = jnp.repeat(ssm_log_dt[layer], n_state_g, axis=-1)
    b_t = lambda p: p[layer].transpose(0, 3, 1, 2).reshape(2, n_in_g, n_state)
    tab, bbar_re, bbar_im = _s5_prep_call(flat(ssm_a_re), flat(ssm_a_im), log_dt,
                                          b_t(ssm_b_re), b_t(ssm_b_im))
    ys = []
    for direction in range(2):
        to_blocks = lambda m: _block_diag(
            m[direction].reshape(n_in_g, n_groups, n_state_g).transpose(1, 0, 2)).astype(BF16)
        c_blocks = lambda p: _block_diag(p[layer, direction].transpose(0, 2, 1)).astype(BF16)
        y_dir = _s5_call(u_ctx.reshape(lc, b, ssm_w), u_lat.reshape(l, b, ssm_w),
                         to_blocks(bbar_re), to_blocks(bbar_im),
                         c_blocks(ssm_c_re), c_blocks(ssm_c_im), tab[direction],
                         reverse=direction == 1, tt=tt_s5)
        ys.append(y_dir.reshape(l, b * ssm_w))

    n_hc = 2 * DIFF_HEADS
    kn2 = jnp.maximum(kn_lat.max(axis=(1, 2)), kn_ctx.max(axis=(1, 2)))[:, None, :n_hc]
    qn2 = qn_lat[:, :, 0, :n_hc]
    if tq >= tm_lat:
        qn2 = qn2.reshape(b, l // tq, tq // tm_lat, n_hc).max(axis=2)
    else:
        qn2 = jnp.repeat(qn2, tm_lat // tq, axis=1)
    bounds = jnp.sqrt(qn2 * kn2) * ATTN_BOUND_MARGIN
    bounds = bounds.reshape(b, l // tq, DIFF_HEADS, 2).transpose(0, 2, 1, 3).reshape(-1)
    a_lat = _attn_call(bounds, q_lat, k_lat, v_lat, k_ctx, v_ctx, lam_q[layer], lam_k[layer],
                       subln_g[layer].reshape(1, -1), lam_init=lam_init, tq=tq, tk=tk, rc=rc)

    w_o = w_out[layer].astype(BF16)
    mixer = (u_lat, ys[0], ys[1], a_lat, ssm_d[layer].reshape(1, ssm_w),
             w_glu[layer].astype(BF16), b_glu[layer].reshape(1, ssm_w), w_o[:ssm_w], w_o[ssm_w:])
    return _ffn_call(x1, mod, g2, w2_in, w2_out, mod_i=2, mod_row=lat_row, tm=tm_lat,
                     final_g=final_g.reshape(1, d), mixer=mixer)
```
